```python
import jax, jax.numpy as jnp
from jax import lax
import numpy as np

D_MODEL = 1024
BATCH = 2
SEQ = 8192
DEPTH = 2

D_LRU = 1024
LRU_HEADS = 16
LRU_BW = D_LRU // LRU_HEADS
RG_C = 8.0
CONV_K = 4
D_SSD = 1024
SSD_HEAD_DIM = 64
SSD_HEADS = D_SSD // SSD_HEAD_DIM
SSD_GROUPS = 4
SSD_STATE = 128
SSD_CHUNK = 128
SSD_CONV_DIM = D_SSD + 2 * SSD_GROUPS * SSD_STATE
D_MIX = D_LRU + D_SSD
IN_COLS = 2 * D_LRU + D_SSD + SSD_CONV_DIM + SSD_HEADS
MOE_GROUPS = 4
EXPERTS_PER_GROUP = 8
N_EXPERTS = MOE_GROUPS * EXPERTS_PER_GROUP
TOP_K = 2
D_EXPERT = 512
MOE_BLOCK = 128
EPS = 1e-6

kernel_name = "hybrid_rglru_ssd_hmoe_trunk"


def rms_norm(x, w):
    xf = x.astype(jnp.float32)
    y = xf * lax.rsqrt(jnp.mean(xf * xf, axis=-1, keepdims=True) + EPS)
    return (y * w.astype(jnp.float32)).astype(x.dtype)


def causal_dwconv(x, w, b):
    c = x.shape[-1]
    y = lax.conv_general_dilated(
        x, w[:, None, :].astype(x.dtype), window_strides=(1,),
        padding=[(CONV_K - 1, 0)], dimension_numbers=('NWC', 'WIO', 'NWC'),
        feature_group_count=c)
    return y + b.astype(x.dtype)


def linear_recurrence(a, u):
    def combine(l, r):
        a1, b1 = l
        a2, b2 = r
        return a1 * a2, a2 * b1 + b2
    _, h = lax.associative_scan(combine, (a, u), axis=1)
    return h


def rg_lru(u, wa, ba, wx, bx, lam):
    bsz, s, _ = u.shape
    ub = u.reshape(bsz, s, LRU_HEADS, LRU_BW)
    r = jax.nn.sigmoid(jnp.einsum('bshi,hij->bshj', ub, wa.astype(jnp.float32)) + ba).reshape(bsz, s, D_LRU)
    i = jax.nn.sigmoid(jnp.einsum('bshi,hij->bshj', ub, wx.astype(jnp.float32)) + bx).reshape(bsz, s, D_LRU)
    log_a = RG_C * r * jax.nn.log_sigmoid(lam.astype(jnp.float32))
    a = jnp.exp(log_a)
    mult = jnp.sqrt(-jnp.expm1(2.0 * log_a))
    return linear_recurrence(a, mult * (i * u))


def ssd_chunked(x, dt, A, Bm, Cm):
    b, s, h, p = x.shape
    g, n = Bm.shape[2], Bm.shape[3]
    c, l = s // SSD_CHUNK, SSD_CHUNK
    Bh = jnp.repeat(Bm, h // g, axis=2).reshape(b, c, l, h, n)
    Ch = jnp.repeat(Cm, h // g, axis=2).reshape(b, c, l, h, n)
    xdt = (x * dt[..., None]).reshape(b, c, l, h, p)
    a_cs = jnp.cumsum((dt * A).reshape(b, c, l, h), axis=2)
    seg = a_cs[:, :, :, None, :] - a_cs[:, :, None, :, :]
    mask = jnp.tril(jnp.ones((l, l), dtype=bool))[None, None, :, :, None]
    Lmat = jnp.exp(jnp.where(mask, seg, -jnp.inf))
    scores = jnp.einsum('bclhn,bcshn->bclsh', Ch, Bh)
    y_diag = jnp.einsum('bclsh,bcshp->bclhp', scores * Lmat, xdt)
    decay_states = jnp.exp(a_cs[:, :, -1:, :] - a_cs)
    states = jnp.einsum('bclhn,bclh,bclhp->bchpn', Bh, decay_states, xdt)
    chunk_decay = jnp.exp(a_cs[:, :, -1, :])

    def step(prev, inp):
        st, dec = inp
        return prev * dec[:, :, None, None] + st, prev

    _, prev_states = lax.scan(step, jnp.zeros((b, h, p, n), jnp.float32),
                              (states.transpose(1, 0, 2, 3, 4), chunk_decay.transpose(1, 0, 2)))
    prev_states = prev_states.transpose(1, 0, 2, 3, 4)
    y_off = jnp.einsum('bclhn,bchpn,bclh->bclhp', Ch, prev_states, jnp.exp(a_cs))
    return (y_diag + y_off).reshape(b, s, h, p)


def hybrid_mixer(h, w_in, lru_conv_w, lru_conv_b, lru_wa, lru_ba, lru_wx, lru_bx, lru_lambda,
                 lru_norm, ssd_conv_w, ssd_conv_b, ssd_dt_bias, ssd_a_log, ssd_d, ssd_norm, w_out):
    bsz, s, _ = h.shape
    proj = h @ w_in
    o1 = D_LRU
    o2 = o1 + D_LRU
    o3 = o2 + D_SSD
    o4 = o3 + SSD_CONV_DIM
    lru_x, lru_gate, ssd_z, ssd_xbc, ssd_dt = (proj[..., :o1], proj[..., o1:o2], proj[..., o2:o3],
                                               proj[..., o3:o4], proj[..., o4:])
    u = causal_dwconv(lru_x, lru_conv_w, lru_conv_b).astype(jnp.float32)
    hr = rg_lru(u, lru_wa, lru_ba, lru_wx, lru_bx, lru_lambda)
    y_lru = rms_norm((jax.nn.gelu(lru_gate.astype(jnp.float32)) * hr).astype(h.dtype), lru_norm)
    xbc = jax.nn.silu(causal_dwconv(ssd_xbc, ssd_conv_w, ssd_conv_b).astype(jnp.float32))
    xs = xbc[..., :D_SSD].reshape(bsz, s, SSD_HEADS, SSD_HEAD_DIM)
    Bm = xbc[..., D_SSD:D_SSD + SSD_GROUPS * SSD_STATE].reshape(bsz, s, SSD_GROUPS, SSD_STATE)
    Cm = xbc[..., D_SSD + SSD_GROUPS * SSD_STATE:].reshape(bsz, s, SSD_GROUPS, SSD_STATE)
    dt = jax.nn.softplus(ssd_dt.astype(jnp.float32) + ssd_dt_bias.astype(jnp.float32))
    A = -jnp.exp(ssd_a_log.astype(jnp.float32))
    y = ssd_chunked(xs, dt, A, Bm, Cm) + xs * ssd_d.astype(jnp.float32)[:, None]
    y = y.reshape(bsz, s, D_SSD) * jax.nn.silu(ssd_z.astype(jnp.float32))
    y_ssd = rms_norm(y.astype(h.dtype), ssd_norm)
    return jnp.concatenate([y_lru, y_ssd], axis=-1) @ w_out


def hier_moe(h, w_coarse, b_coarse, w_fine, b_fine, w_gate, w_up, w_down):
    bsz, s, d = h.shape
    xt = h.reshape(-1, d)
    n = xt.shape[0]
    xf = xt.astype(jnp.float32)
    c_logits = xf @ w_coarse.astype(jnp.float32) + b_coarse.astype(jnp.float32)
    c_prob = jax.nn.softmax(c_logits, axis=-1)
    g_idx = jnp.argmax(c_logits, axis=-1)
    g_w = jnp.take_along_axis(c_prob, g_idx[:, None], axis=1)[:, 0]
    f_all = jnp.einsum('nd,gde->nge', xf, w_fine.astype(jnp.float32)) + b_fine.astype(jnp.float32)
    f_logits = jnp.take_along_axis(f_all, g_idx[:, None, None], axis=1)[:, 0]
    f_val, f_idx = lax.top_k(f_logits, TOP_K)
    f_w = jax.nn.softmax(f_val, axis=-1)
    expert = (g_idx[:, None] * EXPERTS_PER_GROUP + f_idx).reshape(-1).astype(jnp.int32)
    weight = (g_w[:, None] * f_w).reshape(-1)
    token = jnp.repeat(jnp.arange(n, dtype=jnp.int32), TOP_K)
    n_assign = n * TOP_K
    n_blocks = -(-(n_assign + N_EXPERTS * (MOE_BLOCK - 1)) // MOE_BLOCK)
    n_rows = n_blocks * MOE_BLOCK
    order = jnp.argsort(expert)
    e_sorted = expert[order]
    counts = jnp.bincount(expert, length=N_EXPERTS).astype(jnp.int32)
    padded = (counts + MOE_BLOCK - 1) // MOE_BLOCK * MOE_BLOCK
    pad_end = jnp.cumsum(padded)
    pad_start = pad_end - padded
    start = jnp.cumsum(counts) - counts
    dest = pad_start[e_sorted] + jnp.arange(n_assign, dtype=jnp.int32) - start[e_sorted]
    row_tok = jnp.zeros((n_rows,), jnp.int32).at[dest].set(token[order])
    row_w = jnp.zeros((n_rows,), jnp.float32).at[dest].set(weight[order])
    blk_expert = jnp.minimum(
        jnp.searchsorted(pad_end, jnp.arange(n_blocks, dtype=jnp.int32) * MOE_BLOCK, side='right'),
        N_EXPERTS - 1)
    xb = xt[row_tok].reshape(n_blocks, MOE_BLOCK, d)

    def expert_block(args):
        xi, e = args
        hid = jax.nn.silu(xi @ w_gate[e]) * (xi @ w_up[e])
        return hid @ w_down[e]

    yb = lax.map(expert_block, (xb, blk_expert)).reshape(n_rows, d)
    out = jnp.zeros_like(xt).at[row_tok].add(yb * row_w[:, None].astype(yb.dtype))
    return out.reshape(bsz, s, d)


def setup_inputs(seed: int = 0) -> dict:
    key = jax.random.key(seed)
    ks = jax.random.split(key, 32)
    f32 = jnp.float32
    nrm = lambda k, shape, scale: jax.random.normal(k, shape, f32) * scale
    gain = lambda k, shape: 1.0 + 0.01 * jax.random.normal(k, shape, f32)
    u = jax.random.uniform(ks[9], (DEPTH, D_LRU), f32, 0.9, 0.999)
    s_l = u ** (1.0 / RG_C)
    lam = jnp.log(s_l) - jnp.log1p(-s_l)
    dt0 = jnp.exp(jax.random.uniform(ks[13], (DEPTH, SSD_HEADS), f32) * (jnp.log(0.1) - jnp.log(0.001)) + jnp.log(0.001))
    dt_bias = dt0 + jnp.log(-jnp.expm1(-dt0))
    return {
        "x": jax.random.normal(ks[0], (BATCH, SEQ, D_MODEL), f32),
        "norm_mix": gain(ks[1], (DEPTH, D_MODEL)),
        "w_in": nrm(ks[2], (DEPTH, D_MODEL, IN_COLS), D_MODEL ** -0.5),
        "lru_conv_w": nrm(ks[3], (DEPTH, CONV_K, D_LRU), CONV_K ** -0.5),
        "lru_conv_b": nrm(ks[4], (DEPTH, D_LRU), 0.02),
        "lru_wa": nrm(ks[5], (DEPTH, LRU_HEADS, LRU_BW, LRU_BW), LRU_BW ** -0.5),
        "lru_ba": nrm(ks[6], (DEPTH, LRU_HEADS, LRU_BW), 0.02),
        "lru_wx": nrm(ks[7], (DEPTH, LRU_HEADS, LRU_BW, LRU_BW), LRU_BW ** -0.5),
        "lru_bx": nrm(ks[8], (DEPTH, LRU_HEADS, LRU_BW), 0.02),
        "lru_lambda": lam,
        "lru_norm": gain(ks[10], (DEPTH, D_LRU)),
        "ssd_conv_w": nrm(ks[11], (DEPTH, CONV_K, SSD_CONV_DIM), CONV_K ** -0.5),
        "ssd_conv_b": nrm(ks[12], (DEPTH, SSD_CONV_DIM), 0.02),
        "ssd_dt_bias": dt_bias,
        "ssd_a_log": jnp.log(jax.random.uniform(ks[14], (DEPTH, SSD_HEADS), f32, 1.0, 16.0)),
        "ssd_d": gain(ks[15], (DEPTH, SSD_HEADS)),
        "ssd_norm": gain(ks[16], (DEPTH, D_SSD)),
        "w_out": nrm(ks[17], (DEPTH, D_MIX, D_MODEL), D_MIX ** -0.5),
        "norm_ffn": gain(ks[18], (DEPTH, D_MODEL)),
        "w_coarse": nrm(ks[19], (DEPTH, D_MODEL, MOE_GROUPS), D_MODEL ** -0.5),
        "b_coarse": nrm(ks[20], (DEPTH, MOE_GROUPS), 0.01),
        "w_fine": nrm(ks[21], (DEPTH, MOE_GROUPS, D_MODEL, EXPERTS_PER_GROUP), D_MODEL ** -0.5),
        "b_fine": nrm(ks[22], (DEPTH, MOE_GROUPS, EXPERTS_PER_GROUP), 0.01),
        "w_gate": nrm(ks[23], (DEPTH, N_EXPERTS, D_MODEL, D_EXPERT), D_MODEL ** -0.5),
        "w_up": nrm(ks[24], (DEPTH, N_EXPERTS, D_MODEL, D_EXPERT), D_MODEL ** -0.5),
        "w_down": nrm(ks[25], (DEPTH, N_EXPERTS, D_EXPERT, D_MODEL), D_EXPERT ** -0.5),
        "final_norm": gain(ks[26], (D_MODEL,)),
    }


def reference(x, norm_mix, w_in, lru_conv_w, lru_conv_b, lru_wa, lru_ba, lru_wx, lru_bx, lru_lambda,
              lru_norm, ssd_conv_w, ssd_conv_b, ssd_dt_bias, ssd_a_log, ssd_d, ssd_norm, w_out,
              norm_ffn, w_coarse, b_coarse, w_fine, b_fine, w_gate, w_up, w_down, final_norm):
    for i in range(DEPTH):
        h = rms_norm(x, norm_mix[i])
        x = x + hybrid_mixer(h, w_in[i], lru_conv_w[i], lru_conv_b[i], lru_wa[i], lru_ba[i],
                             lru_wx[i], lru_bx[i], lru_lambda[i], lru_norm[i], ssd_conv_w[i],
                             ssd_conv_b[i], ssd_dt_bias[i], ssd_a_log[i], ssd_d[i], ssd_norm[i], w_out[i])
        h = rms_norm(x, norm_ffn[i])
        x = x + hier_moe(h, w_coarse[i], b_coarse[i], w_fine[i], b_fine[i], w_gate[i], w_up[i], w_down[i])
    return rms_norm(x, final_norm)
```

```python
import functools

import jax
import jax.numpy as jnp
from jax import lax
from jax.experimental import pallas as pl
from jax.experimental.pallas import tpu as pltpu

F32 = jnp.float32
BF16 = jnp.bfloat16

D_MODEL = 1024
D_LRU = 1024
LRU_HEADS = 16
LRU_BW = 64
RG_C = 8.0
CONV_K = 4
D_SSD = 1024
SSD_HEAD_DIM = 64
SSD_HEADS = 16
SSD_GROUPS = 4
SSD_STATE = 128
SSD_CHUNK = 128
MOE_GROUPS = 4
EXPERTS_PER_GROUP = 8
N_EXPERTS = 32
TOP_K = 2
D_EXPERT = 512
EPS = 1e-6

LANES = 128
SUBLANES = 8
VMEM_LIMIT = 56 * 1024 * 1024

TM_PROJ = 512
T_LRU = 256
TM_EXP = 256
T_TOK = 256
GATE_W = 256


def _cparams(sem):
    return pltpu.CompilerParams(dimension_semantics=sem, vmem_limit_bytes=VMEM_LIMIT)


def _const_spec(shape):
    n = len(shape)
    return pl.BlockSpec(shape, lambda *_: (0,) * n)


def _sigmoid(x):
    return 1.0 / (1.0 + jnp.exp(-x))


def _softplus(x):
    return jnp.maximum(x, 0.0) + jnp.log1p(jnp.exp(-jnp.abs(x)))


def _rms(x, w):
    ms = jnp.mean(x * x, axis=-1, keepdims=True)
    return x * lax.rsqrt(ms + EPS) * w


def _inproj_kernel(x_ref, nw_ref, wlx_ref, wlg_ref, wz_ref, wxbc_ref, wdt_ref,
                   lx_ref, lg_ref, z_ref, xbc_ref, dt_ref):
    h = _rms(x_ref[...], nw_ref[...]).astype(BF16)
    lx_ref[...] = jnp.dot(h, wlx_ref[...], preferred_element_type=F32).astype(BF16)
    lg_ref[...] = jnp.dot(h, wlg_ref[...], preferred_element_type=F32).astype(BF16)
    z_ref[...] = jnp.dot(h, wz_ref[...], preferred_element_type=F32).astype(BF16)
    xbc_ref[...] = jnp.dot(h, wxbc_ref[...], preferred_element_type=F32).astype(BF16)
    dt_ref[...] = jnp.dot(h, wdt_ref[...], preferred_element_type=F32)


def _inproj(x, nw, wlx, wlg, wz, wxbc, wdt):
    n = x.shape[0]
    tm = TM_PROJ
    dx = wxbc.shape[1]
    row = lambda w: pl.BlockSpec((tm, w), lambda i: (i, 0))
    return pl.pallas_call(
        _inproj_kernel,
        grid=(n // tm,),
        in_specs=[row(D_MODEL), _const_spec((1, D_MODEL)),
                  _const_spec(wlx.shape), _const_spec(wlg.shape), _const_spec(wz.shape),
                  _const_spec(wxbc.shape), _const_spec(wdt.shape)],
        out_specs=[row(D_LRU), row(D_LRU), row(D_SSD), row(dx), row(LANES)],
        out_shape=[jax.ShapeDtypeStruct((n, D_LRU), BF16),
                   jax.ShapeDtypeStruct((n, D_LRU), BF16),
                   jax.ShapeDtypeStruct((n, D_SSD), BF16),
                   jax.ShapeDtypeStruct((n, dx), BF16),
                   jax.ShapeDtypeStruct((n, LANES), F32)],
        compiler_params=_cparams(("arbitrary",)),
        name="inproj",
    )(x, nw, wlx, wlg, wz, wxbc, wdt)


def _causal_conv(xbuf, cw_ref, cb_ref, t):
    cw = cw_ref[...]
    acc = cb_ref[...] + cw[0:1, :] * xbuf[pl.ds(SUBLANES - 3, t), :]
    for k in range(1, CONV_K):
        acc = acc + cw[k:k + 1, :] * xbuf[pl.ds(SUBLANES - 3 + k, t), :]
    return acc


def _lru_kernel(lx_ref, lg_ref, cw_ref, cb_ref, wg_ref, ba_ref, bx_ref, lam_ref, nw_ref,
                y_ref, xbuf, a_s, v_s, h_s, hcarry):
    t = T_LRU
    j = pl.program_id(1)

    @pl.when(j == 0)
    def _():
        xbuf[0:SUBLANES, :] = jnp.zeros((SUBLANES, D_LRU), F32)
        hcarry[...] = jnp.zeros_like(hcarry)

    xbuf[SUBLANES:SUBLANES + t, :] = lx_ref[...].astype(F32)
    u = _causal_conv(xbuf, cw_ref, cb_ref, t)
    xbuf[0:SUBLANES, :] = xbuf[t:t + SUBLANES, :]

    lam = lam_ref[...]
    log_sig = jnp.minimum(lam, 0.0) - jnp.log1p(jnp.exp(-jnp.abs(lam)))
    for g in range(D_LRU // GATE_W):
        sl = slice(g * GATE_W, (g + 1) * GATE_W)
        ug = u[:, sl]
        gates = jnp.dot(ug.astype(BF16), wg_ref[g], preferred_element_type=F32)
        r = _sigmoid(gates[:, :GATE_W] + ba_ref[:, sl])
        i = _sigmoid(gates[:, GATE_W:] + bx_ref[:, sl])
        log_a = RG_C * r * log_sig[:, sl]
        a = jnp.exp(log_a)
        mult = jnp.sqrt(1.0 - a * a)
        a_s[:, sl] = a
        v_s[:, sl] = mult * (i * ug)

    row = lax.broadcasted_iota(jnp.int32, (SUBLANES, D_LRU), 0)

    def scan_body(k, hprev):
        r0 = pl.multiple_of(k * SUBLANES, SUBLANES)
        a = a_s[pl.ds(r0, SUBLANES), :]
        v = v_s[pl.ds(r0, SUBLANES), :]
        for s in (1, 2, 4):
            keep = row >= s
            a_sh = jnp.where(keep, pltpu.roll(a, s, 0), 1.0)
            v_sh = jnp.where(keep, pltpu.roll(v, s, 0), 0.0)
            v = v + a * v_sh
            a = a * a_sh
        h = v + a * hprev
        h_s[pl.ds(r0, SUBLANES), :] = h
        return h[SUBLANES - 1:SUBLANES, :]

    hcarry[...] = lax.fori_loop(0, t // SUBLANES, scan_body, hcarry[...], unroll=4)

    g_in = lg_ref[...].astype(F32)
    gelu = 0.5 * g_in * (1.0 + jnp.tanh(0.7978845608028654 * (g_in + 0.044715 * g_in * g_in * g_in)))
    y_ref[...] = _rms(gelu * h_s[...], nw_ref[...]).astype(BF16)


def _lru(lx, lg, cw, cb, wg, ba, bx, lam, nw, bsz, seq):
    t = T_LRU
    nj = seq // t
    row = pl.BlockSpec((t, D_LRU), lambda b, j: (b * nj + j, 0))
    vec = _const_spec((1, D_LRU))
    return pl.pallas_call(
        _lru_kernel,
        grid=(bsz, nj),
        in_specs=[row, row, _const_spec(cw.shape), vec, _const_spec(wg.shape),
                  vec, vec, vec, vec],
        out_specs=row,
        out_shape=jax.ShapeDtypeStruct((bsz * seq, D_LRU), BF16),
        scratch_shapes=[pltpu.VMEM((t + SUBLANES, D_LRU), F32),
                        pltpu.VMEM((t, D_LRU), F32),
                        pltpu.VMEM((t, D_LRU), F32),
                        pltpu.VMEM((t, D_LRU), F32),
                        pltpu.VMEM((1, D_LRU), F32)],
        compiler_params=_cparams(("arbitrary", "arbitrary")),
        name="rglru",
    )(lx, lg, cw, cb, wg, ba, bx, lam, nw)


def _split3(x):
    hi = x.astype(BF16)
    r1 = x - hi.astype(F32)
    mid = r1.astype(BF16)
    lo = (r1 - mid.astype(F32)).astype(BF16)
    return hi, mid, lo


def _ssd_kernel(xbc_ref, z_ref, dt_ref, cw_ref, cb_ref, dtb_ref, alog_ref, dvec_ref, nw_ref,
                y_ref, xbuf, state, y_s):
    t = SSD_CHUNK
    j = pl.program_id(1)

    @pl.when(j == 0)
    def _():
        xbuf[0:SUBLANES, :] = jnp.zeros((SUBLANES, xbuf.shape[1]), F32)
        state[...] = jnp.zeros_like(state)

    xbuf[SUBLANES:SUBLANES + t, :] = xbc_ref[...].astype(F32)
    u = _causal_conv(xbuf, cw_ref, cb_ref, t)
    xbuf[0:SUBLANES, :] = xbuf[t:t + SUBLANES, :]
    xc = u * _sigmoid(u)
    gn = SSD_GROUPS * SSD_STATE
    xs = xc[:, :D_SSD]
    bm = xc[:, D_SSD:D_SSD + gn]
    cm = xc[:, D_SSD + gn:]

    dt = _softplus(dt_ref[...] + dtb_ref[...])
    d_a = dt * (-jnp.exp(alog_ref[...]))
    ri = lax.broadcasted_iota(jnp.int32, (t, t), 0)
    ci = lax.broadcasted_iota(jnp.int32, (t, t), 1)
    causal = ri >= ci
    tril = causal.astype(BF16)
    hi, mid, lo = _split3(d_a)
    a_cs = (jnp.dot(tril, hi, preferred_element_type=F32)
            + jnp.dot(tril, mid, preferred_element_type=F32)
            + jnp.dot(tril, lo, preferred_element_type=F32))
    a_cs_t = a_cs.T
    dt_t = dt.T
    a_last_t = a_cs_t[:, t - 1:t]
    w_state_t = jnp.exp(a_last_t - a_cs_t) * dt_t
    chunk_decay_t = jnp.exp(a_last_t)

    lane = lax.broadcasted_iota(jnp.int32, (1, LANES), 1)
    first = lane < SSD_HEAD_DIM

    heads_per_group = SSD_HEADS // SSD_GROUPS
    for g in range(SSD_GROUPS):
        gsl = slice(g * SSD_STATE, (g + 1) * SSD_STATE)
        c_g = cm[:, gsl].astype(BF16)
        b_g = bm[:, gsl]
        scores = lax.dot_general(c_g, b_g.astype(BF16), (((1,), (1,)), ((), ())),
                                 preferred_element_type=F32)
        b_t = b_g.T
        st_g = state[g]
        y_off = jnp.dot(c_g, st_g.astype(BF16), preferred_element_type=F32)
        for q in range(heads_per_group // 2):
            h0 = g * heads_per_group + 2 * q
            psl = slice(h0 * SSD_HEAD_DIM, (h0 + 2) * SSD_HEAD_DIM)
            lsl = slice(2 * q * SSD_HEAD_DIM, (2 * q + 2) * SSD_HEAD_DIM)
            x_pair = xs[:, psl].astype(BF16)
            yd, ns, ea, cd = [], [], [], []
            for h in (h0, h0 + 1):
                col = jnp.broadcast_to(a_cs[:, h:h + 1], (t, t))
                seg = col - a_cs_t[h:h + 1, :]
                lmat = jnp.exp(jnp.where(causal, seg, -1e30))
                m = (scores * lmat * dt_t[h:h + 1, :]).astype(BF16)
                yd.append(jnp.dot(m, x_pair, preferred_element_type=F32))
                bw = (b_t * w_state_t[h:h + 1, :]).astype(BF16)
                ns.append(jnp.dot(bw, x_pair, preferred_element_type=F32))
                ea.append(jnp.exp(col))
                cd.append(jnp.broadcast_to(chunk_decay_t[h:h + 1, :], (1, LANES)))
            y_pair = (jnp.where(first, yd[0], yd[1])
                      + jnp.where(first, ea[0], ea[1]) * y_off[:, lsl])
            y_s[:, psl] = y_pair
            state[g, :, lsl] = (st_g[:, lsl] * jnp.where(first, cd[0], cd[1])
                                + jnp.where(first, ns[0], ns[1]))

    zf = z_ref[...].astype(F32)
    y = (y_s[...] + xs * dvec_ref[...]) * (zf * _sigmoid(zf))
    y_ref[...] = _rms(y, nw_ref[...]).astype(BF16)


def _ssd(xbc, z, dt, cw, cb, dtb, alog, dvec, nw, bsz, seq):
    t = SSD_CHUNK
    nj = seq // t
    dx = xbc.shape[1]
    row = lambda w: pl.BlockSpec((t, w), lambda b, j: (b * nj + j, 0))
    return pl.pallas_call(
        _ssd_kernel,
        grid=(bsz, nj),
        in_specs=[row(dx), row(D_SSD), row(LANES), _const_spec(cw.shape), _const_spec((1, dx)),
                  _const_spec((1, LANES)), _const_spec((1, LANES)),
                  _const_spec((1, D_SSD)), _const_spec((1, D_SSD))],
        out_specs=row(D_SSD),
        out_shape=jax.ShapeDtypeStruct((bsz * seq, D_SSD), BF16),
        scratch_shapes=[pltpu.VMEM((t + SUBLANES, dx), F32),
                        pltpu.VMEM((SSD_GROUPS, SSD_STATE, D_SSD // SSD_GROUPS), F32),
                        pltpu.VMEM((t, D_SSD), F32)],
        compiler_params=_cparams(("arbitrary", "arbitrary")),
        name="ssd",
    )(xbc, z, dt, cw, cb, dtb, alog, dvec, nw)


def _outproj_kernel(yl_ref, ys_ref, x_ref, wo1_ref, wo2_ref, nw_ref, wrh_ref, wrl_ref,
                    x1_ref, h_ref, lg_ref):
    x1 = (x_ref[...]
          + jnp.dot(yl_ref[...], wo1_ref[...], preferred_element_type=F32)
          + jnp.dot(ys_ref[...], wo2_ref[...], preferred_element_type=F32))
    x1_ref[...] = x1
    h = _rms(x1, nw_ref[...])
    h_ref[...] = h
    h_hi = h.astype(BF16)
    h_lo = (h - h_hi.astype(F32)).astype(BF16)
    w_hi = wrh_ref[...]
    lg_ref[...] = (jnp.dot(h_hi, w_hi, preferred_element_type=F32)
                   + jnp.dot(h_lo, w_hi, preferred_element_type=F32)
                   + jnp.dot(h_hi, wrl_ref[...], preferred_element_type=F32))


def _outproj(yl, ys, x, wo1, wo2, nw, wrh, wrl):
    n = x.shape[0]
    tm = TM_PROJ
    row = lambda w: pl.BlockSpec((tm, w), lambda i: (i, 0))
    return pl.pallas_call(
        _outproj_kernel,
        grid=(n // tm,),
        in_specs=[row(D_LRU), row(D_SSD), row(D_MODEL), _const_spec(wo1.shape),
                  _const_spec(wo2.shape), _const_spec((1, D_MODEL)),
                  _const_spec(wrh.shape), _const_spec(wrl.shape)],
        out_specs=[row(D_MODEL), row(D_MODEL), row(LANES)],
        out_shape=[jax.ShapeDtypeStruct((n, D_MODEL), F32),
                   jax.ShapeDtypeStruct((n, D_MODEL), F32),
                   jax.ShapeDtypeStruct((n, LANES), F32)],
        compiler_params=_cparams(("arbitrary",)),
        name="outproj",
    )(yl, ys, x, wo1, wo2, nw, wrh, wrl)


def _dispatch_kernel(dest_ref, h_ref, xb_in_ref, xb_ref, sem):
    del xb_in_ref
    n_copy = T_TOK * TOP_K

    def copy(a):
        return pltpu.make_async_copy(h_ref.at[pl.ds(a // TOP_K, 1), :],
                                     xb_ref.at[pl.ds(dest_ref[0, a], 1), :], sem)

    def start(a, c):
        copy(a).start()
        return c

    def wait(a, c):
        copy(a).wait()
        return c

    lax.fori_loop(0, n_copy, start, 0, unroll=8)
    lax.fori_loop(0, n_copy, wait, 0, unroll=8)


def _dispatch(dest, h, n_rows):
    n = h.shape[0]
    t = T_TOK
    dest3 = dest.reshape(n // t, 1, t * TOP_K)
    xb0 = jnp.zeros((n_rows, D_MODEL), F32)
    return pl.pallas_call(
        _dispatch_kernel,
        grid=(n // t,),
        in_specs=[pl.BlockSpec((None, 1, t * TOP_K), lambda i: (i, 0, 0), memory_space=pltpu.SMEM),
                  pl.BlockSpec((t, D_MODEL), lambda i: (i, 0)),
                  pl.BlockSpec(memory_space=pl.ANY)],
        out_specs=pl.BlockSpec(memory_space=pl.ANY),
        out_shape=jax.ShapeDtypeStruct((n_rows, D_MODEL), F32),
        scratch_shapes=[pltpu.SemaphoreType.DMA(())],
        input_output_aliases={2: 0},
        compiler_params=_cparams(("arbitrary",)),
        name="dispatch",
    )(dest3, h, xb0)


def _expert_kernel(be_ref, nu_ref, xb_ref, wg_ref, wu_ref, wd_ref, yb_ref):
    i = pl.program_id(0)

    @pl.when(i < nu_ref[0])
    def _():
        x = xb_ref[...].astype(BF16)
        gate = jnp.dot(x, wg_ref[...], preferred_element_type=F32)
        up = jnp.dot(x, wu_ref[...], preferred_element_type=F32)
        hid = (gate * _sigmoid(gate) * up).astype(BF16)
        yb_ref[...] = jnp.dot(hid, wd_ref[...], preferred_element_type=F32)

    @pl.when(i >= nu_ref[0])
    def _():
        yb_ref[...] = jnp.zeros_like(yb_ref)


def _experts(blk_expert, n_used, xb, wg, wu, wd):
    n_rows = xb.shape[0]
    tm = TM_EXP
    grid_spec = pltpu.PrefetchScalarGridSpec(
        num_scalar_prefetch=2,
        grid=(n_rows // tm,),
        in_specs=[pl.BlockSpec((tm, D_MODEL), lambda i, be, nu: (jnp.minimum(i, nu[0] - 1), 0)),
                  pl.BlockSpec((None, D_MODEL, D_EXPERT), lambda i, be, nu: (be[i], 0, 0)),
                  pl.BlockSpec((None, D_MODEL, D_EXPERT), lambda i, be, nu: (be[i], 0, 0)),
                  pl.BlockSpec((None, D_EXPERT, D_MODEL), lambda i, be, nu: (be[i], 0, 0))],
        out_specs=pl.BlockSpec((tm, D_MODEL), lambda i, be, nu: (i, 0)),
    )
    return pl.pallas_call(
        _expert_kernel,
        grid_spec=grid_spec,
        out_shape=jax.ShapeDtypeStruct((n_rows, D_MODEL), F32),
        compiler_params=_cparams(("arbitrary",)),
        name="experts",
    )(blk_expert, n_used, xb, wg, wu, wd)


def _combine_kernel(dest_ref, x_ref, w_ref, nw_ref, yb_ref, o_ref, gbuf, sem, *, final_norm):
    t = T_TOK
    n_copy = t * TOP_K

    def copy(a):
        r = (a % TOP_K) * t + a // TOP_K
        return pltpu.make_async_copy(yb_ref.at[pl.ds(dest_ref[0, a], 1), :],
                                     gbuf.at[pl.ds(r, 1), :], sem)

    def start(a, c):
        copy(a).start()
        return c

    def wait(a, c):
        copy(a).wait()
        return c

    lax.fori_loop(0, n_copy, start, 0, unroll=8)
    lax.fori_loop(0, n_copy, wait, 0, unroll=8)

    w = w_ref[...]
    out = x_ref[...] + w[:, 0:1] * gbuf[0:t, :] + w[:, 1:2] * gbuf[t:2 * t, :]
    if final_norm:
        out = _rms(out, nw_ref[...])
    o_ref[...] = out


def _combine(dest, x, w, nw, yb, final_norm):
    n = x.shape[0]
    t = T_TOK
    dest3 = dest.reshape(n // t, 1, t * TOP_K)
    return pl.pallas_call(
        functools.partial(_combine_kernel, final_norm=final_norm),
        grid=(n // t,),
        in_specs=[pl.BlockSpec((None, 1, t * TOP_K), lambda i: (i, 0, 0), memory_space=pltpu.SMEM),
                  pl.BlockSpec((t, D_MODEL), lambda i: (i, 0)),
                  pl.BlockSpec((t, TOP_K), lambda i: (i, 0)),
                  _const_spec((1, D_MODEL)),
                  pl.BlockSpec(memory_space=pl.ANY)],
        out_specs=pl.BlockSpec((t, D_MODEL), lambda i: (i, 0)),
        out_shape=jax.ShapeDtypeStruct((n, D_MODEL), F32),
        scratch_shapes=[pltpu.VMEM((t * TOP_K, D_MODEL), F32),
                        pltpu.SemaphoreType.DMA(())],
        compiler_params=_cparams(("arbitrary",)),
        name="combine",
    )(dest3, x, w, nw, yb)


def _route(logits, b_coarse, b_fine, n_blocks):
    n = logits.shape[0]
    c_logits = logits[:, :MOE_GROUPS] + b_coarse
    c_prob = jax.nn.softmax(c_logits, axis=-1)
    g_idx = jnp.argmax(c_logits, axis=-1)
    g_w = jnp.take_along_axis(c_prob, g_idx[:, None], axis=1)[:, 0]
    f_all = logits[:, MOE_GROUPS:MOE_GROUPS + N_EXPERTS].reshape(n, MOE_GROUPS, EXPERTS_PER_GROUP) + b_fine
    f_logits = jnp.take_along_axis(f_all, g_idx[:, None, None], axis=1)[:, 0]
    f_val, f_idx = lax.top_k(f_logits, TOP_K)
    f_w = jax.nn.softmax(f_val, axis=-1)
    expert = (g_idx[:, None] * EXPERTS_PER_GROUP + f_idx).astype(jnp.int32)
    weight = g_w[:, None] * f_w
    flat = expert.reshape(-1)
    onehot = (flat[:, None] == jnp.arange(N_EXPERTS, dtype=jnp.int32)[None, :]).astype(jnp.int32)
    csum = jnp.cumsum(onehot, axis=0)
    rank = jnp.sum(csum * onehot, axis=1) - 1
    counts = csum[-1]
    padded = (counts + TM_EXP - 1) // TM_EXP * TM_EXP
    pad_end = jnp.cumsum(padded)
    pad_start = pad_end - padded
    dest = (pad_start[flat] + rank).astype(jnp.int32)
    n_used = (pad_end[-1:] // TM_EXP).astype(jnp.int32)
    blk_expert = jnp.minimum(
        jnp.searchsorted(pad_end, jnp.arange(n_blocks, dtype=jnp.int32) * TM_EXP, side='right'),
        N_EXPERTS - 1).astype(jnp.int32)
    return dest, weight, blk_expert, n_used


def _gate_blocks(wa, wx):
    per = GATE_W // LRU_BW
    eye = jnp.eye(per, dtype=F32)

    def bd(w):
        w = w.reshape(LRU_HEADS // per, per, LRU_BW, LRU_BW)
        full = jnp.einsum('gpij,pq->gpiqj', w, eye)
        return full.reshape(LRU_HEADS // per, GATE_W, GATE_W)

    return jnp.concatenate([bd(wa), bd(wx)], axis=-1).astype(BF16)


def _pad_rows(w, rows):
    return jnp.pad(w, ((0, rows - w.shape[0]), (0, 0)))


def _pad_lanes(v):
    return jnp.pad(v, (0, LANES - v.shape[0])).reshape(1, LANES)


def kernel(x, norm_mix, w_in, lru_conv_w, lru_conv_b, lru_wa, lru_ba, lru_wx, lru_bx, lru_lambda, lru_norm, ssd_conv_w, ssd_conv_b, ssd_dt_bias, ssd_a_log, ssd_d, ssd_norm, w_out, norm_ffn, w_coarse, b_coarse, w_fine, b_fine, w_gate, w_up, w_down, final_norm):
    bsz, seq, d = x.shape
    n = bsz * seq
    depth = w_in.shape[0]
    n_assign = n * TOP_K
    n_blocks = -(-(n_assign + N_EXPERTS * (TM_EXP - 1)) // TM_EXP)
    n_rows = n_blocks * TM_EXP
    o1, o2 = D_LRU, 2 * D_LRU
    o3 = o2 + D_SSD
    o4 = o3 + D_SSD + 2 * SSD_GROUPS * SSD_STATE

    xt = x.reshape(n, d)
    for i in range(depth):
        wi = w_in[i].astype(BF16)
        wdt = jnp.pad(wi[:, o4:], ((0, 0), (0, LANES - SSD_HEADS)))
        lx, lg, z, xbc, dt = _inproj(xt, norm_mix[i].reshape(1, d), wi[:, :o1], wi[:, o1:o2],
                                     wi[:, o2:o3], wi[:, o3:o4], wdt)
        y_lru = _lru(lx, lg, _pad_rows(lru_conv_w[i], SUBLANES), lru_conv_b[i].reshape(1, -1),
                     _gate_blocks(lru_wa[i], lru_wx[i]), lru_ba[i].reshape(1, -1),
                     lru_bx[i].reshape(1, -1), lru_lambda[i].reshape(1, -1),
                     lru_norm[i].reshape(1, -1), bsz, seq)
        y_ssd = _ssd(xbc, z, dt, _pad_rows(ssd_conv_w[i], SUBLANES), ssd_conv_b[i].reshape(1, -1),
                     _pad_lanes(ssd_dt_bias[i]), _pad_lanes(ssd_a_log[i]),
                     jnp.repeat(ssd_d[i], SSD_HEAD_DIM).reshape(1, -1),
                     ssd_norm[i].reshape(1, -1), bsz, seq)
        wo = w_out[i].astype(BF16)
        w_r = jnp.concatenate([w_coarse[i], w_fine[i].transpose(1, 0, 2).reshape(d, N_EXPERTS)], axis=1)
        w_r = jnp.pad(w_r, ((0, 0), (0, LANES - w_r.shape[1])))
        w_r_hi = w_r.astype(BF16)
        w_r_lo = (w_r - w_r_hi.astype(F32)).astype(BF16)
        x1, h2, logits = _outproj(y_lru, y_ssd, xt, wo[:D_LRU], wo[D_LRU:],
                                  norm_ffn[i].reshape(1, d), w_r_hi, w_r_lo)
        dest, weight, blk_expert, n_used = _route(logits, b_coarse[i], b_fine[i], n_blocks)
        xb = _dispatch(dest, h2, n_rows)
        yb = _experts(blk_expert, n_used, xb, w_gate[i].astype(BF16), w_up[i].astype(BF16),
                      w_down[i].astype(BF16))
        xt = _combine(dest, x1, weight, final_norm.reshape(1, d), yb, final_norm=(i == depth - 1))
    return xt.reshape(bsz, seq, d)
```

```python
import functools

import jax
import jax.numpy as jnp
from jax import lax
from jax.experimental import pallas as pl
from jax.experimental.pallas import tpu as pltpu

F32 = jnp.float32
BF16 = jnp.bfloat16

D_MODEL = 1024
D_LRU = 1024
LRU_HEADS = 16
LRU_BW = 64
RG_C = 8.0
CONV_K = 4
D_SSD = 1024
SSD_HEAD_DIM = 64
SSD_HEADS = 16
SSD_GROUPS = 4
SSD_STATE = 128
SSD_CHUNK = 128
MOE_GROUPS = 4
EXPERTS_PER_GROUP = 8
N_EXPERTS = 32
TOP_K = 2
D_EXPERT = 512
EPS = 1e-6

LANES = 128
SUBLANES = 8
VMEM_LIMIT = 56 * 1024 * 1024

TM_PROJ = 512
T_LRU = 256
TM_EXP = 256
T_TOK = 256
GATE_W = 256


def _cparams(sem):
    return pltpu.CompilerParams(dimension_semantics=sem, vmem_limit_bytes=VMEM_LIMIT)


def _const_spec(shape):
    n = len(shape)
    return pl.BlockSpec(shape, lambda *_: (0,) * n)


def _sigmoid(x):
    return 1.0 / (1.0 + jnp.exp(-x))


def _softplus(x):
    return jnp.maximum(x, 0.0) + jnp.log1p(jnp.exp(-jnp.abs(x)))


def _rms(x, w):
    ms = jnp.mean(x * x, axis=-1, keepdims=True)
    return x * lax.rsqrt(ms + EPS) * w


W_COL = 1024
SSD_XBC = D_SSD + 2 * SSD_GROUPS * SSD_STATE


def _inproj_kernel(x_ref, nw_ref, wlx_ref, wlg_ref, wz_ref, wx_ref, wbc_ref, wdt_ref,
                   lx_ref, lg_ref, z_ref, xbc_ref, dt_ref):
    h = _rms(x_ref[...], nw_ref[...]).astype(BF16)
    lx_ref[...] = jnp.dot(h, wlx_ref[...], preferred_element_type=F32).astype(BF16)
    lg_ref[...] = jnp.dot(h, wlg_ref[...], preferred_element_type=F32).astype(BF16)
    z_ref[...] = jnp.dot(h, wz_ref[...], preferred_element_type=F32).astype(BF16)
    xbc_ref[:, 0:W_COL] = jnp.dot(h, wx_ref[...], preferred_element_type=F32).astype(BF16)
    xbc_ref[:, W_COL:] = jnp.dot(h, wbc_ref[...], preferred_element_type=F32).astype(BF16)
    dt_ref[...] = jnp.dot(h, wdt_ref[...], preferred_element_type=F32)


def _inproj(layer, x, nw, w_in, wdt):
    n = x.shape[0]
    tm = TM_PROJ
    row = lambda w: pl.BlockSpec((tm, w), lambda i: (i, 0))
    wcol = lambda c: pl.BlockSpec((None, D_MODEL, W_COL), lambda i: (layer, 0, c))
    return pl.pallas_call(
        _inproj_kernel,
        grid=(n // tm,),
        in_specs=[row(D_MODEL), _const_spec((1, D_MODEL)),
                  wcol(0), wcol(1), wcol(2), wcol(3), wcol(4), _const_spec(wdt.shape)],
        out_specs=[row(D_LRU), row(D_LRU), row(D_SSD), row(SSD_XBC), row(LANES)],
        out_shape=[jax.ShapeDtypeStruct((n, D_LRU), BF16),
                   jax.ShapeDtypeStruct((n, D_LRU), BF16),
                   jax.ShapeDtypeStruct((n, D_SSD), BF16),
                   jax.ShapeDtypeStruct((n, SSD_XBC), BF16),
                   jax.ShapeDtypeStruct((n, LANES), F32)],
        compiler_params=_cparams(("arbitrary",)),
        name="inproj",
    )(x, nw, w_in, w_in, w_in, w_in, w_in, wdt)


def _causal_conv(xbuf, cw_ref, cb_ref, t):
    cw = cw_ref[...]
    acc = cb_ref[...] + cw[0:1, :] * xbuf[pl.ds(SUBLANES - 3, t), :]
    for k in range(1, CONV_K):
        acc = acc + cw[k:k + 1, :] * xbuf[pl.ds(SUBLANES - 3 + k, t), :]
    return acc


def _lru_kernel(lx_ref, lg_ref, cw_ref, cb_ref, wg_ref, ba_ref, bx_ref, lam_ref, nw_ref,
                y_ref, xbuf, a_s, v_s, h_s, hcarry):
    t = T_LRU
    j = pl.program_id(1)

    @pl.when(j == 0)
    def _():
        xbuf[0:SUBLANES, :] = jnp.zeros((SUBLANES, D_LRU), F32)
        hcarry[...] = jnp.zeros_like(hcarry)

    xbuf[SUBLANES:SUBLANES + t, :] = lx_ref[...].astype(F32)
    u = _causal_conv(xbuf, cw_ref, cb_ref, t)
    xbuf[0:SUBLANES, :] = xbuf[t:t + SUBLANES, :]

    lam = lam_ref[...]
    log_sig = jnp.minimum(lam, 0.0) - jnp.log1p(jnp.exp(-jnp.abs(lam)))
    for g in range(D_LRU // GATE_W):
        sl = slice(g * GATE_W, (g + 1) * GATE_W)
        ug = u[:, sl]
        gates = jnp.dot(ug.astype(BF16), wg_ref[g], preferred_element_type=F32)
        r = _sigmoid(gates[:, :GATE_W] + ba_ref[:, sl])
        i = _sigmoid(gates[:, GATE_W:] + bx_ref[:, sl])
        log_a = RG_C * r * log_sig[:, sl]
        a = jnp.exp(log_a)
        mult = jnp.sqrt(1.0 - a * a)
        a_s[:, sl] = a
        v_s[:, sl] = mult * (i * ug)

    row = lax.broadcasted_iota(jnp.int32, (SUBLANES, D_LRU), 0)

    def scan_body(k, hprev):
        r0 = pl.multiple_of(k * SUBLANES, SUBLANES)
        a = a_s[pl.ds(r0, SUBLANES), :]
        v = v_s[pl.ds(r0, SUBLANES), :]
        for s in (1, 2, 4):
            keep = row >= s
            a_sh = jnp.where(keep, pltpu.roll(a, s, 0), 1.0)
            v_sh = jnp.where(keep, pltpu.roll(v, s, 0), 0.0)
            v = v + a * v_sh
            a = a * a_sh
        h = v + a * hprev
        h_s[pl.ds(r0, SUBLANES), :] = h
        return h[SUBLANES - 1:SUBLANES, :]

    hcarry[...] = lax.fori_loop(0, t // SUBLANES, scan_body, hcarry[...], unroll=4)

    g_in = lg_ref[...].astype(F32)
    gelu = 0.5 * g_in * (1.0 + jnp.tanh(0.7978845608028654 * (g_in + 0.044715 * g_in * g_in * g_in)))
    y_ref[...] = _rms(gelu * h_s[...], nw_ref[...]).astype(BF16)


def _lru(lx, lg, cw, cb, wg, ba, bx, lam, nw, bsz, seq):
    t = T_LRU
    nj = seq // t
    row = pl.BlockSpec((t, D_LRU), lambda b, j: (b * nj + j, 0))
    vec = _const_spec((1, D_LRU))
    return pl.pallas_call(
        _lru_kernel,
        grid=(bsz, nj),
        in_specs=[row, row, _const_spec(cw.shape), vec, _const_spec(wg.shape),
                  vec, vec, vec, vec],
        out_specs=row,
        out_shape=jax.ShapeDtypeStruct((bsz * seq, D_LRU), BF16),
        scratch_shapes=[pltpu.VMEM((t + SUBLANES, D_LRU), F32),
                        pltpu.VMEM((t, D_LRU), F32),
                        pltpu.VMEM((t, D_LRU), F32),
                        pltpu.VMEM((t, D_LRU), F32),
                        pltpu.VMEM((1, D_LRU), F32)],
        compiler_params=_cparams(("arbitrary", "arbitrary")),
        name="rglru",
    )(lx, lg, cw, cb, wg, ba, bx, lam, nw)


def _split3(x):
    hi = x.astype(BF16)
    r1 = x - hi.astype(F32)
    mid = r1.astype(BF16)
    lo = (r1 - mid.astype(F32)).astype(BF16)
    return hi, mid, lo


def _ssd_kernel(xbc_ref, z_ref, dt_ref, cw_ref, cb_ref, dtb_ref, alog_ref, dvec_ref, nw_ref,
                y_ref, xbuf, state, y_s):
    t = SSD_CHUNK
    j = pl.program_id(1)

    @pl.when(j == 0)
    def _():
        xbuf[0:SUBLANES, :] = jnp.zeros((SUBLANES, xbuf.shape[1]), F32)
        state[...] = jnp.zeros_like(state)

    xbuf[SUBLANES:SUBLANES + t, :] = xbc_ref[...].astype(F32)
    u = _causal_conv(xbuf, cw_ref, cb_ref, t)
    xbuf[0:SUBLANES, :] = xbuf[t:t + SUBLANES, :]
    xc = u * _sigmoid(u)
    gn = SSD_GROUPS * SSD_STATE
    xs = xc[:, :D_SSD]
    bm = xc[:, D_SSD:D_SSD + gn]
    cm = xc[:, D_SSD + gn:]

    dt = _softplus(dt_ref[...] + dtb_ref[...])
    d_a = dt * (-jnp.exp(alog_ref[...]))
    ri = lax.broadcasted_iota(jnp.int32, (t, t), 0)
    ci = lax.broadcasted_iota(jnp.int32, (t, t), 1)
    causal = ri >= ci
    tril = causal.astype(BF16)
    hi, mid, lo = _split3(d_a)
    a_cs = (jnp.dot(tril, hi, preferred_element_type=F32)
            + jnp.dot(tril, mid, preferred_element_type=F32)
            + jnp.dot(tril, lo, preferred_element_type=F32))
    a_cs_t = a_cs.T
    dt_t = dt.T
    a_last_t = a_cs_t[:, t - 1:t]
    w_state_t = jnp.exp(a_last_t - a_cs_t) * dt_t
    chunk_decay_t = jnp.exp(a_last_t)

    lane = lax.broadcasted_iota(jnp.int32, (1, LANES), 1)
    first = lane < SSD_HEAD_DIM

    heads_per_group = SSD_HEADS // SSD_GROUPS
    for g in range(SSD_GROUPS):
        gsl = slice(g * SSD_STATE, (g + 1) * SSD_STATE)
        c_g = cm[:, gsl].astype(BF16)
        b_g = bm[:, gsl]
        scores = lax.dot_general(c_g, b_g.astype(BF16), (((1,), (1,)), ((), ())),
                                 preferred_element_type=F32)
        b_t = b_g.T
        st_g = state[g]
        y_off = jnp.dot(c_g, st_g.astype(BF16), preferred_element_type=F32)
        for q in range(heads_per_group // 2):
            h0 = g * heads_per_group + 2 * q
            psl = slice(h0 * SSD_HEAD_DIM, (h0 + 2) * SSD_HEAD_DIM)
            lsl = slice(2 * q * SSD_HEAD_DIM, (2 * q + 2) * SSD_HEAD_DIM)
            x_pair = xs[:, psl].astype(BF16)
            yd, ns, ea, cd = [], [], [], []
            for h in (h0, h0 + 1):
                col = jnp.broadcast_to(a_cs[:, h:h + 1], (t, t))
                seg = col - a_cs_t[h:h + 1, :]
                lmat = jnp.exp(jnp.where(causal, seg, -1e30))
                m = (scores * lmat * dt_t[h:h + 1, :]).astype(BF16)
                yd.append(jnp.dot(m, x_pair, preferred_element_type=F32))
                bw = (b_t * w_state_t[h:h + 1, :]).astype(BF16)
                ns.append(jnp.dot(bw, x_pair, preferred_element_type=F32))
                ea.append(jnp.exp(col))
                cd.append(jnp.broadcast_to(chunk_decay_t[h:h + 1, :], (1, LANES)))
            y_pair = (jnp.where(first, yd[0], yd[1])
                      + jnp.where(first, ea[0], ea[1]) * y_off[:, lsl])
            y_s[:, psl] = y_pair
            state[g, :, lsl] = (st_g[:, lsl] * jnp.where(first, cd[0], cd[1])
                                + jnp.where(first, ns[0], ns[1]))

    zf = z_ref[...].astype(F32)
    y = (y_s[...] + xs * dvec_ref[...]) * (zf * _sigmoid(zf))
    y_ref[...] = _rms(y, nw_ref[...]).astype(BF16)


def _ssd(xbc, z, dt, cw, cb, dtb, alog, dvec, nw, bsz, seq):
    t = SSD_CHUNK
    nj = seq // t
    dx = xbc.shape[1]
    row = lambda w: pl.BlockSpec((t, w), lambda b, j: (b * nj + j, 0))
    return pl.pallas_call(
        _ssd_kernel,
        grid=(bsz, nj),
        in_specs=[row(dx), row(D_SSD), row(LANES), _const_spec(cw.shape), _const_spec((1, dx)),
                  _const_spec((1, LANES)), _const_spec((1, LANES)),
                  _const_spec((1, D_SSD)), _const_spec((1, D_SSD))],
        out_specs=row(D_SSD),
        out_shape=jax.ShapeDtypeStruct((bsz * seq, D_SSD), BF16),
        scratch_shapes=[pltpu.VMEM((t + SUBLANES, dx), F32),
                        pltpu.VMEM((SSD_GROUPS, SSD_STATE, D_SSD // SSD_GROUPS), F32),
                        pltpu.VMEM((t, D_SSD), F32)],
        compiler_params=_cparams(("arbitrary", "arbitrary")),
        name="ssd",
    )(xbc, z, dt, cw, cb, dtb, alog, dvec, nw)


ROUTE_E, ROUTE_W, ROUTE_R = 0, 2, 4
NEG_BIG = -1e30


def _first_argmax(vals, lane_f):
    m = jnp.max(vals, axis=-1, keepdims=True)
    idx = jnp.min(jnp.where(vals == m, lane_f, float(LANES)), axis=-1, keepdims=True)
    return m, idx


def _outproj_kernel(yl_ref, ys_ref, x_ref, wo_ref, nw_ref, wrh_ref, wrl_ref, rb_ref,
                    x1_ref, h_ref, route_ref, cnt_ref, stril, running):
    tm = x_ref.shape[0]

    @pl.when(pl.program_id(0) == 0)
    def _():
        ri = lax.broadcasted_iota(jnp.int32, (tm, tm), 0)
        ci = lax.broadcasted_iota(jnp.int32, (tm, tm), 1)
        stril[...] = (ri > ci).astype(BF16)
        running[...] = jnp.zeros_like(running)

    x1 = (x_ref[...]
          + jnp.dot(yl_ref[...], wo_ref[0:D_LRU, :], preferred_element_type=F32)
          + jnp.dot(ys_ref[...], wo_ref[D_LRU:, :], preferred_element_type=F32))
    x1_ref[...] = x1
    h = _rms(x1, nw_ref[...])
    h_ref[...] = h
    h_hi = h.astype(BF16)
    h_lo = (h - h_hi.astype(F32)).astype(BF16)
    w_hi = wrh_ref[...]
    logits = (jnp.dot(h_hi, w_hi, preferred_element_type=F32)
              + jnp.dot(h_lo, w_hi, preferred_element_type=F32)
              + jnp.dot(h_hi, wrl_ref[...], preferred_element_type=F32)) + rb_ref[...]

    lane = lax.broadcasted_iota(jnp.int32, (tm, LANES), 1)
    lane_f = lane.astype(F32)
    is_c = lane < MOE_GROUPS
    lc = jnp.where(is_c, logits, NEG_BIG)
    m_c, g_idx = _first_argmax(lc, lane_f)
    g_w = 1.0 / jnp.sum(jnp.where(is_c, jnp.exp(lc - m_c), 0.0), axis=-1, keepdims=True)
    lo = float(MOE_GROUPS) + float(EXPERTS_PER_GROUP) * g_idx
    is_f = (lane_f >= lo) & (lane_f < lo + float(EXPERTS_PER_GROUP))
    lf = jnp.where(is_f, logits, NEG_BIG)
    v1, i1 = _first_argmax(lf, lane_f)
    v2, i2 = _first_argmax(jnp.where(lane_f == i1, NEG_BIG, lf), lane_f)
    ex = jnp.exp(v2 - v1)
    w1 = g_w / (1.0 + ex)
    w2 = g_w * ex / (1.0 + ex)

    oh1 = (lane_f == i1).astype(F32)
    oh2 = (lane_f == i2).astype(F32)
    both = oh1 + oh2
    before = jnp.dot(stril[...], both.astype(BF16), preferred_element_type=F32) + running[...]
    r1 = jnp.sum(oh1 * before, axis=-1, keepdims=True)
    r2 = jnp.sum(oh2 * before, axis=-1, keepdims=True)
    running[...] = running[...] + jnp.sum(both, axis=0, keepdims=True)
    cnt_ref[...] = running[...]

    e1 = i1 - float(MOE_GROUPS)
    e2 = i2 - float(MOE_GROUPS)
    route = jnp.zeros((tm, LANES), F32)
    for off, (a, b) in ((ROUTE_E, (e1, e2)), (ROUTE_W, (w1, w2)), (ROUTE_R, (r1, r2))):
        route = jnp.where(lane == off, a, jnp.where(lane == off + 1, b, route))
    route_ref[...] = route


def _outproj(layer, yl, ys, x, wo, nw, wrh, wrl, rb):
    n = x.shape[0]
    tm = TM_PROJ
    row = lambda w: pl.BlockSpec((tm, w), lambda i: (i, 0))
    return pl.pallas_call(
        _outproj_kernel,
        grid=(n // tm,),
        in_specs=[row(D_LRU), row(D_SSD), row(D_MODEL),
                  pl.BlockSpec((None,) + wo.shape[1:], lambda i: (layer, 0, 0)),
                  _const_spec((1, D_MODEL)),
                  _const_spec(wrh.shape), _const_spec(wrl.shape), _const_spec((1, LANES))],
        out_specs=[row(D_MODEL), row(D_MODEL), row(LANES), _const_spec((1, LANES))],
        out_shape=[jax.ShapeDtypeStruct((n, D_MODEL), F32),
                   jax.ShapeDtypeStruct((n, D_MODEL), F32),
                   jax.ShapeDtypeStruct((n, LANES), F32),
                   jax.ShapeDtypeStruct((1, LANES), F32)],
        scratch_shapes=[pltpu.VMEM((tm, tm), BF16), pltpu.VMEM((1, LANES), F32)],
        compiler_params=_cparams(("arbitrary",)),
        name="outproj",
    )(yl, ys, x, wo, nw, wrh, wrl, rb)


def _dispatch_kernel(dest_ref, h_ref, xb_in_ref, xb_ref, sem):
    del xb_in_ref

    def copy(k, tok):
        return pltpu.make_async_copy(h_ref.at[pl.ds(tok, 1), :],
                                     xb_ref.at[pl.ds(dest_ref[k, tok], 1), :], sem)

    def start(tok, c):
        for k in range(TOP_K):
            copy(k, tok).start()
        return c

    def wait(tok, c):
        for k in range(TOP_K):
            copy(k, tok).wait()
        return c

    lax.fori_loop(0, T_TOK, start, 0, unroll=8)
    lax.fori_loop(0, T_TOK, wait, 0, unroll=8)


def _tile_dest(dest, t):
    n = dest.shape[0]
    return dest.reshape(n // t, t, TOP_K).transpose(0, 2, 1)


def _dispatch(dest, h, n_rows):
    n = h.shape[0]
    t = T_TOK
    dest3 = _tile_dest(dest, t)
    xb0 = jnp.zeros((n_rows, D_MODEL), F32)
    return pl.pallas_call(
        _dispatch_kernel,
        grid=(n // t,),
        in_specs=[pl.BlockSpec((None, TOP_K, t), lambda i: (i, 0, 0), memory_space=pltpu.SMEM),
                  pl.BlockSpec((t, D_MODEL), lambda i: (i, 0)),
                  pl.BlockSpec(memory_space=pl.ANY)],
        out_specs=pl.BlockSpec(memory_space=pl.ANY),
        out_shape=jax.ShapeDtypeStruct((n_rows, D_MODEL), F32),
        scratch_shapes=[pltpu.SemaphoreType.DMA(())],
        input_output_aliases={2: 0},
        compiler_params=_cparams(("arbitrary",)),
        name="dispatch",
    )(dest3, h, xb0)


def _expert_kernel(be_ref, nu_ref, xb_ref, wg_ref, wu_ref, wd_ref, yb_ref, wg_s, wu_s, wd_s):
    i = pl.program_id(0)
    prev = be_ref[jnp.maximum(i - 1, 0)]

    @pl.when((i == 0) | (be_ref[i] != prev))
    def _():
        wg_s[...] = wg_ref[...].astype(BF16)
        wu_s[...] = wu_ref[...].astype(BF16)
        wd_s[...] = wd_ref[...].astype(BF16)

    @pl.when(i < nu_ref[0])
    def _():
        x = xb_ref[...].astype(BF16)
        gate = jnp.dot(x, wg_s[...], preferred_element_type=F32)
        up = jnp.dot(x, wu_s[...], preferred_element_type=F32)
        hid = (gate * _sigmoid(gate) * up).astype(BF16)
        yb_ref[...] = jnp.dot(hid, wd_s[...], preferred_element_type=F32)

    @pl.when(i >= nu_ref[0])
    def _():
        yb_ref[...] = jnp.zeros_like(yb_ref)


def _experts(layer, blk_expert, n_used, xb, wg, wu, wd):
    n_rows = xb.shape[0]
    tm = TM_EXP
    wspec = lambda a, b: pl.BlockSpec((None, None, a, b), lambda i, be, nu: (layer, be[i], 0, 0))
    grid_spec = pltpu.PrefetchScalarGridSpec(
        num_scalar_prefetch=2,
        grid=(n_rows // tm,),
        in_specs=[pl.BlockSpec((tm, D_MODEL), lambda i, be, nu: (jnp.minimum(i, nu[0] - 1), 0)),
                  wspec(D_MODEL, D_EXPERT), wspec(D_MODEL, D_EXPERT), wspec(D_EXPERT, D_MODEL)],
        out_specs=pl.BlockSpec((tm, D_MODEL), lambda i, be, nu: (i, 0)),
        scratch_shapes=[pltpu.VMEM((D_MODEL, D_EXPERT), BF16),
                        pltpu.VMEM((D_MODEL, D_EXPERT), BF16),
                        pltpu.VMEM((D_EXPERT, D_MODEL), BF16)],
    )
    return pl.pallas_call(
        _expert_kernel,
        grid_spec=grid_spec,
        out_shape=jax.ShapeDtypeStruct((n_rows, D_MODEL), F32),
        compiler_params=_cparams(("arbitrary",)),
        name="experts",
    )(blk_expert, n_used, xb, wg, wu, wd)


def _combine_kernel(dest_ref, x_ref, w_ref, nw_ref, yb_ref, o_ref, gbuf, sem, *, final_norm):
    t = T_TOK

    def copy(k, tok):
        return pltpu.make_async_copy(yb_ref.at[pl.ds(dest_ref[k, tok], 1), :],
                                     gbuf.at[k, pl.ds(tok, 1), :], sem)

    def start(tok, c):
        for k in range(TOP_K):
            copy(k, tok).start()
        return c

    def wait(tok, c):
        for k in range(TOP_K):
            copy(k, tok).wait()
        return c

    lax.fori_loop(0, t, start, 0, unroll=8)
    lax.fori_loop(0, t, wait, 0, unroll=8)

    w = w_ref[...]
    out = x_ref[...] + w[:, 0:1] * gbuf[0] + w[:, 1:2] * gbuf[1]
    if final_norm:
        out = _rms(out, nw_ref[...])
    o_ref[...] = out


def _combine(dest, x, w, nw, yb, final_norm):
    n = x.shape[0]
    t = T_TOK
    dest3 = _tile_dest(dest, t)
    return pl.pallas_call(
        functools.partial(_combine_kernel, final_norm=final_norm),
        grid=(n // t,),
        in_specs=[pl.BlockSpec((None, TOP_K, t), lambda i: (i, 0, 0), memory_space=pltpu.SMEM),
                  pl.BlockSpec((t, D_MODEL), lambda i: (i, 0)),
                  pl.BlockSpec((t, TOP_K), lambda i: (i, 0)),
                  _const_spec((1, D_MODEL)),
                  pl.BlockSpec(memory_space=pl.ANY)],
        out_specs=pl.BlockSpec((t, D_MODEL), lambda i: (i, 0)),
        out_shape=jax.ShapeDtypeStruct((n, D_MODEL), F32),
        scratch_shapes=[pltpu.VMEM((TOP_K, t, D_MODEL), F32),
                        pltpu.SemaphoreType.DMA(())],
        compiler_params=_cparams(("arbitrary",)),
        name="combine",
    )(dest3, x, w, nw, yb)


def _route(route, cnt, n_blocks):
    expert = route[:, ROUTE_E:ROUTE_E + TOP_K].astype(jnp.int32)
    weight = route[:, ROUTE_W:ROUTE_W + TOP_K]
    rank = route[:, ROUTE_R:ROUTE_R + TOP_K].astype(jnp.int32)
    counts = cnt[0, MOE_GROUPS:MOE_GROUPS + N_EXPERTS].astype(jnp.int32)
    padded = (counts + TM_EXP - 1) // TM_EXP * TM_EXP
    pad_end = jnp.cumsum(padded)
    pad_start = pad_end - padded
    onehot = expert[:, :, None] == jnp.arange(N_EXPERTS, dtype=jnp.int32)
    dest = jnp.sum(jnp.where(onehot, pad_start, 0), axis=-1) + rank
    n_used = (pad_end[-1:] // TM_EXP).astype(jnp.int32)
    blk_expert = jnp.minimum(
        jnp.searchsorted(pad_end, jnp.arange(n_blocks, dtype=jnp.int32) * TM_EXP, side='right'),
        N_EXPERTS - 1).astype(jnp.int32)
    return dest.astype(jnp.int32), weight, blk_expert, n_used


def _gate_blocks(wa, wx):
    per = GATE_W // LRU_BW
    eye = jnp.eye(per, dtype=F32)

    def bd(w):
        w = w.reshape(LRU_HEADS // per, per, LRU_BW, LRU_BW)
        full = jnp.einsum('gpij,pq->gpiqj', w, eye)
        return full.reshape(LRU_HEADS // per, GATE_W, GATE_W)

    return jnp.concatenate([bd(wa), bd(wx)], axis=-1).astype(BF16)


def _pad_rows(w, rows):
    return jnp.pad(w, ((0, rows - w.shape[0]), (0, 0)))


def _pad_lanes(v):
    return jnp.pad(v, (0, LANES - v.shape[0])).reshape(1, LANES)


def kernel(x, norm_mix, w_in, lru_conv_w, lru_conv_b, lru_wa, lru_ba, lru_wx, lru_bx, lru_lambda, lru_norm, ssd_conv_w, ssd_conv_b, ssd_dt_bias, ssd_a_log, ssd_d, ssd_norm, w_out, norm_ffn, w_coarse, b_coarse, w_fine, b_fine, w_gate, w_up, w_down, final_norm):
    bsz, seq, d = x.shape
    n = bsz * seq
    depth = w_in.shape[0]
    n_assign = n * TOP_K
    n_blocks = -(-(n_assign + N_EXPERTS * (TM_EXP - 1)) // TM_EXP)
    n_rows = n_blocks * TM_EXP
    o_dt = 2 * D_LRU + D_SSD + SSD_XBC
    w_in_bf = w_in.astype(BF16)
    w_out_bf = w_out.astype(BF16)

    xt = x.reshape(n, d)
    for i in range(depth):
        wdt = jnp.pad(w_in_bf[i, :, o_dt:], ((0, 0), (0, LANES - SSD_HEADS)))
        lx, lg, z, xbc, dt = _inproj(i, xt, norm_mix[i].reshape(1, d), w_in_bf, wdt)
        y_lru = _lru(lx, lg, _pad_rows(lru_conv_w[i], SUBLANES), lru_conv_b[i].reshape(1, -1),
                     _gate_blocks(lru_wa[i], lru_wx[i]), lru_ba[i].reshape(1, -1),
                     lru_bx[i].reshape(1, -1), lru_lambda[i].reshape(1, -1),
                     lru_norm[i].reshape(1, -1), bsz, seq)
        y_ssd = _ssd(xbc, z, dt, _pad_rows(ssd_conv_w[i], SUBLANES), ssd_conv_b[i].reshape(1, -1),
                     _pad_lanes(ssd_dt_bias[i]), _pad_lanes(ssd_a_log[i]),
                     jnp.repeat(ssd_d[i], SSD_HEAD_DIM).reshape(1, -1),
                     ssd_norm[i].reshape(1, -1), bsz, seq)
        w_r = jnp.concatenate([w_coarse[i], w_fine[i].transpose(1, 0, 2).reshape(d, N_EXPERTS)], axis=1)
        w_r = jnp.pad(w_r, ((0, 0), (0, LANES - w_r.shape[1])))
        w_r_hi = w_r.astype(BF16)
        w_r_lo = (w_r - w_r_hi.astype(F32)).astype(BF16)
        r_bias = _pad_lanes(jnp.concatenate([b_coarse[i], b_fine[i].reshape(-1)]))
        x1, h2, route, cnt = _outproj(i, y_lru, y_ssd, xt, w_out_bf, norm_ffn[i].reshape(1, d),
                                      w_r_hi, w_r_lo, r_bias)
        dest, weight, blk_expert, n_used = _route(route, cnt, n_blocks)
        xb = _dispatch(dest, h2, n_rows)
        yb = _experts(i, blk_expert, n_used, xb, w_gate, w_up, w_down)
        xt = _combine(dest, x1, weight, final_norm.reshape(1, d), yb, final_norm=(i == depth - 1))
    return xt.reshape(bsz, seq, d)
```

```python
import functools

import jax
import jax.numpy as jnp
from jax import lax
from jax.experimental import pallas as pl
from jax.experimental.pallas import tpu as pltpu

F32 = jnp.float32
BF16 = jnp.bfloat16
U32 = jnp.uint32

D_MODEL = 1024
D_LRU = 1024
LRU_HEADS = 16
LRU_BW = 64
RG_C = 8.0
CONV_K = 4
D_SSD = 1024
SSD_HEAD_DIM = 64
SSD_HEADS = 16
SSD_GROUPS = 4
SSD_STATE = 128
SSD_CHUNK = 128
MOE_GROUPS = 4
EXPERTS_PER_GROUP = 8
N_EXPERTS = 32
TOP_K = 2
D_EXPERT = 512
EPS = 1e-6

LANES = 128
SUBLANES = 8
VMEM_LIMIT = 60 * 1024 * 1024

TM_PROJ = 512
T_LRU = 256
TM_EXP = 256
T_TOK = 256
GATE_W = 256
D_HALF = D_MODEL // 2
ROW_TILES = D_MODEL // LANES


def _cparams(sem):
    return pltpu.CompilerParams(dimension_semantics=sem, vmem_limit_bytes=VMEM_LIMIT)


def _const_spec(shape):
    n = len(shape)
    return pl.BlockSpec(shape, lambda *_: (0,) * n)


def _sigmoid(x):
    return 1.0 / (1.0 + jnp.exp(-x))


def _softplus(x):
    return jnp.maximum(x, 0.0) + jnp.log1p(jnp.exp(-jnp.abs(x)))


def _rms(x, w):
    ms = jnp.mean(x * x, axis=-1, keepdims=True)
    return x * lax.rsqrt(ms + EPS) * w


W_COL = 1024
SSD_XBC = D_SSD + 2 * SSD_GROUPS * SSD_STATE


def _inproj_kernel(x_ref, nw_ref, wlx_ref, wlg_ref, wz_ref, wx_ref, wbc_ref, wdt_ref,
                   lx_ref, lg_ref, z_ref, xbc_ref, dt_ref):
    h = _rms(x_ref[...], nw_ref[...]).astype(BF16)
    lx_ref[...] = jnp.dot(h, wlx_ref[...], preferred_element_type=F32).astype(BF16)
    lg_ref[...] = jnp.dot(h, wlg_ref[...], preferred_element_type=F32).astype(BF16)
    z_ref[...] = jnp.dot(h, wz_ref[...], preferred_element_type=F32).astype(BF16)
    xbc_ref[:, 0:W_COL] = jnp.dot(h, wx_ref[...], preferred_element_type=F32).astype(BF16)
    xbc_ref[:, W_COL:] = jnp.dot(h, wbc_ref[...], preferred_element_type=F32).astype(BF16)
    dt_ref[...] = jnp.dot(h, wdt_ref[...], preferred_element_type=F32)


def _inproj(layer, x, nw, w_in, wdt):
    n = x.shape[0]
    tm = TM_PROJ
    row = lambda w: pl.BlockSpec((tm, w), lambda i: (i, 0))
    wcol = lambda c: pl.BlockSpec((None, D_MODEL, W_COL), lambda i: (layer, 0, c))
    return pl.pallas_call(
        _inproj_kernel,
        grid=(n // tm,),
        in_specs=[row(D_MODEL), _const_spec((1, D_MODEL)),
                  wcol(0), wcol(1), wcol(2), wcol(3), wcol(4), _const_spec(wdt.shape)],
        out_specs=[row(D_LRU), row(D_LRU), row(D_SSD), row(SSD_XBC), row(LANES)],
        out_shape=[jax.ShapeDtypeStruct((n, D_LRU), BF16),
                   jax.ShapeDtypeStruct((n, D_LRU), BF16),
                   jax.ShapeDtypeStruct((n, D_SSD), BF16),
                   jax.ShapeDtypeStruct((n, SSD_XBC), BF16),
                   jax.ShapeDtypeStruct((n, LANES), F32)],
        compiler_params=_cparams(("arbitrary",)),
        name="inproj",
    )(x, nw, w_in, w_in, w_in, w_in, w_in, wdt)


def _causal_conv(xbuf, cw_ref, cb_ref, t):
    cw = cw_ref[...]
    acc = cb_ref[...] + cw[0:1, :] * xbuf[pl.ds(SUBLANES - 3, t), :]
    for k in range(1, CONV_K):
        acc = acc + cw[k:k + 1, :] * xbuf[pl.ds(SUBLANES - 3 + k, t), :]
    return acc


def _lru_kernel(lx_ref, lg_ref, cw_ref, cb_ref, wg_ref, ba_ref, bx_ref, lam_ref, nw_ref,
                y_ref, xbuf, a_s, v_s, h_s, hcarry):
    t = T_LRU
    j = pl.program_id(1)

    @pl.when(j == 0)
    def _():
        xbuf[0:SUBLANES, :] = jnp.zeros((SUBLANES, D_LRU), F32)
        hcarry[...] = jnp.zeros_like(hcarry)

    xbuf[SUBLANES:SUBLANES + t, :] = lx_ref[...].astype(F32)
    u = _causal_conv(xbuf, cw_ref, cb_ref, t)
    xbuf[0:SUBLANES, :] = xbuf[t:t + SUBLANES, :]

    lam = lam_ref[...]
    log_sig = jnp.minimum(lam, 0.0) - jnp.log1p(jnp.exp(-jnp.abs(lam)))
    for g in range(D_LRU // GATE_W):
        sl = slice(g * GATE_W, (g + 1) * GATE_W)
        ug = u[:, sl]
        gates = jnp.dot(ug.astype(BF16), wg_ref[g], preferred_element_type=F32)
        r = _sigmoid(gates[:, :GATE_W] + ba_ref[:, sl])
        i = _sigmoid(gates[:, GATE_W:] + bx_ref[:, sl])
        log_a = RG_C * r * log_sig[:, sl]
        a = jnp.exp(log_a)
        mult = jnp.sqrt(1.0 - a * a)
        a_s[:, sl] = a
        v_s[:, sl] = mult * (i * ug)

    row = lax.broadcasted_iota(jnp.int32, (SUBLANES, D_LRU), 0)

    def scan_body(k, hprev):
        r0 = pl.multiple_of(k * SUBLANES, SUBLANES)
        a = a_s[pl.ds(r0, SUBLANES), :]
        v = v_s[pl.ds(r0, SUBLANES), :]
        for s in (1, 2, 4):
            keep = row >= s
            a_sh = jnp.where(keep, pltpu.roll(a, s, 0), 1.0)
            v_sh = jnp.where(keep, pltpu.roll(v, s, 0), 0.0)
            v = v + a * v_sh
            a = a * a_sh
        h = v + a * hprev
        h_s[pl.ds(r0, SUBLANES), :] = h
        return h[SUBLANES - 1:SUBLANES, :]

    hcarry[...] = lax.fori_loop(0, t // SUBLANES, scan_body, hcarry[...], unroll=4)

    g_in = lg_ref[...].astype(F32)
    gelu = 0.5 * g_in * (1.0 + jnp.tanh(0.7978845608028654 * (g_in + 0.044715 * g_in * g_in * g_in)))
    y_ref[...] = _rms(gelu * h_s[...], nw_ref[...]).astype(BF16)


def _lru(lx, lg, cw, cb, wg, ba, bx, lam, nw, bsz, seq):
    t = T_LRU
    nj = seq // t
    row = pl.BlockSpec((t, D_LRU), lambda b, j: (b * nj + j, 0))
    vec = _const_spec((1, D_LRU))
    return pl.pallas_call(
        _lru_kernel,
        grid=(bsz, nj),
        in_specs=[row, row, _const_spec(cw.shape), vec, _const_spec(wg.shape),
                  vec, vec, vec, vec],
        out_specs=row,
        out_shape=jax.ShapeDtypeStruct((bsz * seq, D_LRU), BF16),
        scratch_shapes=[pltpu.VMEM((t + SUBLANES, D_LRU), F32),
                        pltpu.VMEM((t, D_LRU), F32),
                        pltpu.VMEM((t, D_LRU), F32),
                        pltpu.VMEM((t, D_LRU), F32),
                        pltpu.VMEM((1, D_LRU), F32)],
        compiler_params=_cparams(("arbitrary", "arbitrary")),
        name="rglru",
    )(lx, lg, cw, cb, wg, ba, bx, lam, nw)


def _split3(x):
    hi = x.astype(BF16)
    r1 = x - hi.astype(F32)
    mid = r1.astype(BF16)
    lo = (r1 - mid.astype(F32)).astype(BF16)
    return hi, mid, lo


def _ssd_kernel(xbc_ref, z_ref, dt_ref, cw_ref, cb_ref, dtb_ref, alog_ref, dvec_ref, nw_ref,
                y_ref, xbuf, state, y_s):
    t = SSD_CHUNK
    j = pl.program_id(1)

    @pl.when(j == 0)
    def _():
        xbuf[0:SUBLANES, :] = jnp.zeros((SUBLANES, xbuf.shape[1]), F32)
        state[...] = jnp.zeros_like(state)

    xbuf[SUBLANES:SUBLANES + t, :] = xbc_ref[...].astype(F32)
    u = _causal_conv(xbuf, cw_ref, cb_ref, t)
    xbuf[0:SUBLANES, :] = xbuf[t:t + SUBLANES, :]
    xc = u * _sigmoid(u)
    gn = SSD_GROUPS * SSD_STATE
    xs = xc[:, :D_SSD]
    bm = xc[:, D_SSD:D_SSD + gn]
    cm = xc[:, D_SSD + gn:]

    dt = _softplus(dt_ref[...] + dtb_ref[...])
    d_a = dt * (-jnp.exp(alog_ref[...]))
    ri = lax.broadcasted_iota(jnp.int32, (t, t), 0)
    ci = lax.broadcasted_iota(jnp.int32, (t, t), 1)
    causal = ri >= ci
    tril = causal.astype(BF16)
    hi, mid, lo = _split3(d_a)
    a_cs = (jnp.dot(tril, hi, preferred_element_type=F32)
            + jnp.dot(tril, mid, preferred_element_type=F32)
            + jnp.dot(tril, lo, preferred_element_type=F32))
    a_cs_t = a_cs.T
    dt_t = dt.T
    a_last_t = a_cs_t[:, t - 1:t]
    w_state_t = jnp.exp(a_last_t - a_cs_t) * dt_t
    chunk_decay_t = jnp.exp(a_last_t)

    lane = lax.broadcasted_iota(jnp.int32, (1, LANES), 1)
    first = lane < SSD_HEAD_DIM

    heads_per_group = SSD_HEADS // SSD_GROUPS
    for g in range(SSD_GROUPS):
        gsl = slice(g * SSD_STATE, (g + 1) * SSD_STATE)
        c_g = cm[:, gsl].astype(BF16)
        b_g = bm[:, gsl]
        scores = lax.dot_general(c_g, b_g.astype(BF16), (((1,), (1,)), ((), ())),
                                 preferred_element_type=F32)
        b_t = b_g.T
        st_g = state[g]
        y_off = jnp.dot(c_g, st_g.astype(BF16), preferred_element_type=F32)
        for q in range(heads_per_group // 2):
            h0 = g * heads_per_group + 2 * q
            psl = slice(h0 * SSD_HEAD_DIM, (h0 + 2) * SSD_HEAD_DIM)
            lsl = slice(2 * q * SSD_HEAD_DIM, (2 * q + 2) * SSD_HEAD_DIM)
            x_pair = xs[:, psl].astype(BF16)
            yd, ns, ea, cd = [], [], [], []
            for h in (h0, h0 + 1):
                col = jnp.broadcast_to(a_cs[:, h:h + 1], (t, t))
                seg = col - a_cs_t[h:h + 1, :]
                lmat = jnp.exp(jnp.where(causal, seg, -1e30))
                m = (scores * lmat * dt_t[h:h + 1, :]).astype(BF16)
                yd.append(jnp.dot(m, x_pair, preferred_element_type=F32))
                bw = (b_t * w_state_t[h:h + 1, :]).astype(BF16)
                ns.append(jnp.dot(bw, x_pair, preferred_element_type=F32))
                ea.append(jnp.exp(col))
                cd.append(jnp.broadcast_to(chunk_decay_t[h:h + 1, :], (1, LANES)))
            y_pair = (jnp.where(first, yd[0], yd[1])
                      + jnp.where(first, ea[0], ea[1]) * y_off[:, lsl])
            y_s[:, psl] = y_pair
            state[g, :, lsl] = (st_g[:, lsl] * jnp.where(first, cd[0], cd[1])
                                + jnp.where(first, ns[0], ns[1]))

    zf = z_ref[...].astype(F32)
    y = (y_s[...] + xs * dvec_ref[...]) * (zf * _sigmoid(zf))
    y_ref[...] = _rms(y, nw_ref[...]).astype(BF16)


def _ssd(xbc, z, dt, cw, cb, dtb, alog, dvec, nw, bsz, seq):
    t = SSD_CHUNK
    nj = seq // t
    dx = xbc.shape[1]
    row = lambda w: pl.BlockSpec((t, w), lambda b, j: (b * nj + j, 0))
    return pl.pallas_call(
        _ssd_kernel,
        grid=(bsz, nj),
        in_specs=[row(dx), row(D_SSD), row(LANES), _const_spec(cw.shape), _const_spec((1, dx)),
                  _const_spec((1, LANES)), _const_spec((1, LANES)),
                  _const_spec((1, D_SSD)), _const_spec((1, D_SSD))],
        out_specs=row(D_SSD),
        out_shape=jax.ShapeDtypeStruct((bsz * seq, D_SSD), BF16),
        scratch_shapes=[pltpu.VMEM((t + SUBLANES, dx), F32),
                        pltpu.VMEM((SSD_GROUPS, SSD_STATE, D_SSD // SSD_GROUPS), F32),
                        pltpu.VMEM((t, D_SSD), F32)],
        compiler_params=_cparams(("arbitrary", "arbitrary")),
        name="ssd",
    )(xbc, z, dt, cw, cb, dtb, alog, dvec, nw)


ROUTE_E, ROUTE_W, ROUTE_R = 0, 2, 4
NEG_BIG = -1e30


def _first_argmax(vals, lane_f):
    m = jnp.max(vals, axis=-1, keepdims=True)
    idx = jnp.min(jnp.where(vals == m, lane_f, float(LANES)), axis=-1, keepdims=True)
    return m, idx


def _outproj_kernel(yl_ref, ys_ref, x_ref, wo_ref, nw_ref, wrh_ref, wrl_ref, rb_ref,
                    x1_ref, hp_ref, route_ref, cnt_ref, stril, running):
    tm = x_ref.shape[0]

    @pl.when(pl.program_id(0) == 0)
    def _():
        ri = lax.broadcasted_iota(jnp.int32, (tm, tm), 0)
        ci = lax.broadcasted_iota(jnp.int32, (tm, tm), 1)
        stril[...] = (ri > ci).astype(BF16)
        running[...] = jnp.zeros_like(running)

    x1 = (x_ref[...]
          + jnp.dot(yl_ref[...], wo_ref[0:D_LRU, :], preferred_element_type=F32)
          + jnp.dot(ys_ref[...], wo_ref[D_LRU:, :], preferred_element_type=F32))
    x1_ref[...] = x1
    h = _rms(x1, nw_ref[...])
    h_hi = h.astype(BF16)
    h_rt = h_hi.astype(F32)
    bits = lax.bitcast_convert_type(h_rt, U32)
    hp_ref[...] = bits[:, D_HALF:] | (bits[:, :D_HALF] >> 16)
    h_lo = (h - h_rt).astype(BF16)
    w_hi = wrh_ref[...]
    logits = (jnp.dot(h_hi, w_hi, preferred_element_type=F32)
              + jnp.dot(h_lo, w_hi, preferred_element_type=F32)
              + jnp.dot(h_hi, wrl_ref[...], preferred_element_type=F32)) + rb_ref[...]

    lane = lax.broadcasted_iota(jnp.int32, (tm, LANES), 1)
    lane_f = lane.astype(F32)
    is_c = lane < MOE_GROUPS
    lc = jnp.where(is_c, logits, NEG_BIG)
    m_c, g_idx = _first_argmax(lc, lane_f)
    g_w = 1.0 / jnp.sum(jnp.where(is_c, jnp.exp(lc - m_c), 0.0), axis=-1, keepdims=True)
    lo = float(MOE_GROUPS) + float(EXPERTS_PER_GROUP) * g_idx
    is_f = (lane_f >= lo) & (lane_f < lo + float(EXPERTS_PER_GROUP))
    lf = jnp.where(is_f, logits, NEG_BIG)
    v1, i1 = _first_argmax(lf, lane_f)
    v2, i2 = _first_argmax(jnp.where(lane_f == i1, NEG_BIG, lf), lane_f)
    ex = jnp.exp(v2 - v1)
    w1 = g_w / (1.0 + ex)
    w2 = g_w * ex / (1.0 + ex)

    oh1 = (lane_f == i1).astype(F32)
    oh2 = (lane_f == i2).astype(F32)
    both = oh1 + oh2
    before = jnp.dot(stril[...], both.astype(BF16), preferred_element_type=F32) + running[...]
    r1 = jnp.sum(oh1 * before, axis=-1, keepdims=True)
    r2 = jnp.sum(oh2 * before, axis=-1, keepdims=True)
    running[...] = running[...] + jnp.sum(both, axis=0, keepdims=True)
    cnt_ref[...] = running[...]

    e1 = i1 - float(MOE_GROUPS)
    e2 = i2 - float(MOE_GROUPS)
    route = jnp.zeros((tm, LANES), F32)
    for off, (a, b) in ((ROUTE_E, (e1, e2)), (ROUTE_W, (w1, w2)), (ROUTE_R, (r1, r2))):
        route = jnp.where(lane == off, a, jnp.where(lane == off + 1, b, route))
    route_ref[...] = route


def _outproj(layer, yl, ys, x, wo, nw, wrh, wrl, rb):
    n = x.shape[0]
    tm = TM_PROJ
    row = lambda w: pl.BlockSpec((tm, w), lambda i: (i, 0))
    return pl.pallas_call(
        _outproj_kernel,
        grid=(n // tm,),
        in_specs=[row(D_LRU), row(D_SSD), row(D_MODEL),
                  pl.BlockSpec((None,) + wo.shape[1:], lambda i: (layer, 0, 0)),
                  _const_spec((1, D_MODEL)),
                  _const_spec(wrh.shape), _const_spec(wrl.shape), _const_spec((1, LANES))],
        out_specs=[row(D_MODEL), row(D_HALF), row(LANES), _const_spec((1, LANES))],
        out_shape=[jax.ShapeDtypeStruct((n, D_MODEL), F32),
                   jax.ShapeDtypeStruct((n, D_HALF), U32),
                   jax.ShapeDtypeStruct((n, LANES), F32),
                   jax.ShapeDtypeStruct((1, LANES), F32)],
        scratch_shapes=[pltpu.VMEM((tm, tm), BF16), pltpu.VMEM((1, LANES), F32)],
        compiler_params=_cparams(("arbitrary",)),
        name="outproj",
    )(yl, ys, x, wo, nw, wrh, wrl, rb)


def _expert_kernel(be_ref, nu_ref, rt_ref, hp_hbm, wg_ref, wu_ref, wd_ref, yb_ref,
                   hp_v, xs, wg_s, wu_s, wd_s, sem):
    i = pl.program_id(0)

    @pl.when(i == 0)
    def _():
        cp = pltpu.make_async_copy(hp_hbm, hp_v, sem)
        cp.start()
        cp.wait()

    prev = be_ref[jnp.maximum(i - 1, 0)]

    @pl.when((i == 0) | (be_ref[i] != prev))
    def _():
        wg_s[...] = wg_ref[...].astype(BF16)
        wu_s[...] = wu_ref[...].astype(BF16)
        wd_s[...] = wd_ref[...].astype(BF16)

    @pl.when(i < nu_ref[0])
    def _():
        for r in range(TM_EXP):
            xs[r:r + 1, :] = hp_v[pl.ds(rt_ref[0, r], 1), :]
        packed = xs[...]
        x_lo = lax.bitcast_convert_type(packed << 16, F32).astype(BF16)
        x_hi = lax.bitcast_convert_type((packed >> 16) << 16, F32).astype(BF16)
        gate = (jnp.dot(x_lo, wg_s[0:D_HALF, :], preferred_element_type=F32)
                + jnp.dot(x_hi, wg_s[D_HALF:, :], preferred_element_type=F32))
        up = (jnp.dot(x_lo, wu_s[0:D_HALF, :], preferred_element_type=F32)
              + jnp.dot(x_hi, wu_s[D_HALF:, :], preferred_element_type=F32))
        hid = (gate * _sigmoid(gate) * up).astype(BF16)
        y = jnp.dot(hid, wd_s[...], preferred_element_type=F32)
        for c in range(ROW_TILES):
            yb_ref[pl.ds(c, TM_EXP, stride=ROW_TILES), :] = y[:, c * LANES:(c + 1) * LANES]

    @pl.when(i >= nu_ref[0])
    def _():
        yb_ref[...] = jnp.zeros_like(yb_ref)


def _experts(layer, blk_expert, n_used, row_tok, hp, wg, wu, wd):
    n_blocks = blk_expert.shape[0]
    tm = TM_EXP
    wspec = lambda a, b: pl.BlockSpec((None, None, a, b), lambda i, be, nu: (layer, be[i], 0, 0))
    grid_spec = pltpu.PrefetchScalarGridSpec(
        num_scalar_prefetch=2,
        grid=(n_blocks,),
        in_specs=[pl.BlockSpec((None, 1, tm), lambda i, be, nu: (i, 0, 0), memory_space=pltpu.SMEM),
                  pl.BlockSpec(memory_space=pl.ANY),
                  wspec(D_MODEL, D_EXPERT), wspec(D_MODEL, D_EXPERT), wspec(D_EXPERT, D_MODEL)],
        out_specs=pl.BlockSpec((tm * ROW_TILES, LANES), lambda i, be, nu: (i, 0)),
        scratch_shapes=[pltpu.VMEM(hp.shape, U32),
                        pltpu.VMEM((tm, D_HALF), U32),
                        pltpu.VMEM((D_MODEL, D_EXPERT), BF16),
                        pltpu.VMEM((D_MODEL, D_EXPERT), BF16),
                        pltpu.VMEM((D_EXPERT, D_MODEL), BF16),
                        pltpu.SemaphoreType.DMA(())],
    )
    return pl.pallas_call(
        _expert_kernel,
        grid_spec=grid_spec,
        out_shape=jax.ShapeDtypeStruct((n_blocks * tm * ROW_TILES, LANES), F32),
        compiler_params=_cparams(("arbitrary",)),
        name="experts",
    )(blk_expert, n_used, row_tok.reshape(n_blocks, 1, tm), hp, wg, wu, wd)


def _combine_kernel(dcur_ref, dnxt_ref, x_ref, w_ref, nw_ref, yb_ref, o_ref, gbuf, sems,
                    *, final_norm):
    t = T_TOK
    i = pl.program_id(0)
    n_steps = pl.num_programs(0)
    slot = i % 2

    def copy(dref, s, k, tok):
        src = pl.ds(pl.multiple_of(dref[k, tok] * ROW_TILES, ROW_TILES), ROW_TILES)
        dst = pl.ds(pl.multiple_of(tok * ROW_TILES, ROW_TILES), ROW_TILES)
        return pltpu.make_async_copy(yb_ref.at[src, :], gbuf.at[s, k, dst, :], sems.at[s])

    def issue(dref, s):
        def body(tok, c):
            for k in range(TOP_K):
                copy(dref, s, k, tok).start()
            return c
        lax.fori_loop(0, t, body, 0, unroll=8)

    @pl.when(i == 0)
    def _():
        issue(dcur_ref, 0)

    @pl.when(i + 1 < n_steps)
    def _():
        issue(dnxt_ref, 1 - slot)

    def wait(tok, c):
        for k in range(TOP_K):
            copy(dcur_ref, slot, k, tok).wait()
        return c

    lax.fori_loop(0, t, wait, 0, unroll=8)

    w = w_ref[...]
    w0 = w[:, 0:1]
    w1 = w[:, 1:2]
    parts = []
    for c in range(ROW_TILES):
        g0 = gbuf[slot, 0, pl.ds(c, t, stride=ROW_TILES), :]
        g1 = gbuf[slot, 1, pl.ds(c, t, stride=ROW_TILES), :]
        parts.append(x_ref[:, c * LANES:(c + 1) * LANES] + w0 * g0 + w1 * g1)
    out = jnp.concatenate(parts, axis=-1)
    if final_norm:
        out = _rms(out, nw_ref[...])
    o_ref[...] = out


def _tile_dest(dest, t):
    n = dest.shape[0]
    return dest.reshape(n // t, t, TOP_K).transpose(0, 2, 1)


def _combine(dest, x, w, nw, yb, final_norm):
    n = x.shape[0]
    t = T_TOK
    n_steps = n // t
    dest3 = _tile_dest(dest, t)
    dspec = lambda f: pl.BlockSpec((None, TOP_K, t), lambda i: (f(i), 0, 0), memory_space=pltpu.SMEM)
    return pl.pallas_call(
        functools.partial(_combine_kernel, final_norm=final_norm),
        grid=(n_steps,),
        in_specs=[dspec(lambda i: i), dspec(lambda i: jnp.minimum(i + 1, n_steps - 1)),
                  pl.BlockSpec((t, D_MODEL), lambda i: (i, 0)),
                  pl.BlockSpec((t, TOP_K), lambda i: (i, 0)),
                  _const_spec((1, D_MODEL)),
                  pl.BlockSpec(memory_space=pl.ANY)],
        out_specs=pl.BlockSpec((t, D_MODEL), lambda i: (i, 0)),
        out_shape=jax.ShapeDtypeStruct((n, D_MODEL), F32),
        scratch_shapes=[pltpu.VMEM((2, TOP_K, t * ROW_TILES, LANES), F32),
                        pltpu.SemaphoreType.DMA((2,))],
        compiler_params=_cparams(("arbitrary",)),
        name="combine",
    )(dest3, dest3, x, w, nw, yb)


def _route(route, cnt, n_blocks):
    n = route.shape[0]
    expert = route[:, ROUTE_E:ROUTE_E + TOP_K].astype(jnp.int32)
    weight = route[:, ROUTE_W:ROUTE_W + TOP_K]
    rank = route[:, ROUTE_R:ROUTE_R + TOP_K].astype(jnp.int32)
    counts = cnt[0, MOE_GROUPS:MOE_GROUPS + N_EXPERTS].astype(jnp.int32)
    padded = (counts + TM_EXP - 1) // TM_EXP * TM_EXP
    pad_end = jnp.cumsum(padded)
    pad_start = pad_end - padded
    onehot = expert[:, :, None] == jnp.arange(N_EXPERTS, dtype=jnp.int32)
    dest = (jnp.sum(jnp.where(onehot, pad_start, 0), axis=-1) + rank).astype(jnp.int32)
    n_used = (pad_end[-1:] // TM_EXP).astype(jnp.int32)
    blk_start = jnp.arange(n_blocks, dtype=jnp.int32) * TM_EXP
    blk_expert = jnp.minimum(jnp.sum(pad_end[None, :] <= blk_start[:, None], axis=1),
                             N_EXPERTS - 1).astype(jnp.int32)
    token = jnp.broadcast_to(jnp.arange(n, dtype=jnp.int32)[:, None], (n, TOP_K))
    row_tok = jnp.zeros((n_blocks * TM_EXP,), jnp.int32).at[dest.reshape(-1)].set(
        token.reshape(-1), unique_indices=True)
    return dest, weight, blk_expert, n_used, row_tok


def _gate_blocks(wa, wx):
    per = GATE_W // LRU_BW
    eye = jnp.eye(per, dtype=F32)

    def bd(w):
        w = w.reshape(LRU_HEADS // per, per, LRU_BW, LRU_BW)
        full = jnp.einsum('gpij,pq->gpiqj', w, eye)
        return full.reshape(LRU_HEADS // per, GATE_W, GATE_W)

    return jnp.concatenate([bd(wa), bd(wx)], axis=-1).astype(BF16)


def _pad_rows(w, rows):
    return jnp.pad(w, ((0, rows - w.shape[0]), (0, 0)))


def _pad_lanes(v):
    return jnp.pad(v, (0, LANES - v.shape[0])).reshape(1, LANES)


def kernel(x, norm_mix, w_in, lru_conv_w, lru_conv_b, lru_wa, lru_ba, lru_wx, lru_bx, lru_lambda, lru_norm, ssd_conv_w, ssd_conv_b, ssd_dt_bias, ssd_a_log, ssd_d, ssd_norm, w_out, norm_ffn, w_coarse, b_coarse, w_fine, b_fine, w_gate, w_up, w_down, final_norm):
    bsz, seq, d = x.shape
    n = bsz * seq
    depth = w_in.shape[0]
    n_assign = n * TOP_K
    n_blocks = -(-(n_assign + N_EXPERTS * (TM_EXP - 1)) // TM_EXP)
    o_dt = 2 * D_LRU + D_SSD + SSD_XBC
    w_in_bf = w_in.astype(BF16)
    w_out_bf = w_out.astype(BF16)

    xt = x.reshape(n, d)
    for i in range(depth):
        wdt = jnp.pad(w_in_bf[i, :, o_dt:], ((0, 0), (0, LANES - SSD_HEADS)))
        lx, lg, z, xbc, dt = _inproj(i, xt, norm_mix[i].reshape(1, d), w_in_bf, wdt)
        y_lru = _lru(lx, lg, _pad_rows(lru_conv_w[i], SUBLANES), lru_conv_b[i].reshape(1, -1),
                     _gate_blocks(lru_wa[i], lru_wx[i]), lru_ba[i].reshape(1, -1),
                     lru_bx[i].reshape(1, -1), lru_lambda[i].reshape(1, -1),
                     lru_norm[i].reshape(1, -1), bsz, seq)
        y_ssd = _ssd(xbc, z, dt, _pad_rows(ssd_conv_w[i], SUBLANES), ssd_conv_b[i].reshape(1, -1),
                     _pad_lanes(ssd_dt_bias[i]), _pad_lanes(ssd_a_log[i]),
                     jnp.repeat(ssd_d[i], SSD_HEAD_DIM).reshape(1, -1),
                     ssd_norm[i].reshape(1, -1), bsz, seq)
        w_r = jnp.concatenate([w_coarse[i], w_fine[i].transpose(1, 0, 2).reshape(d, N_EXPERTS)], axis=1)
        w_r = jnp.pad(w_r, ((0, 0), (0, LANES - w_r.shape[1])))
        w_r_hi = w_r.astype(BF16)
        w_r_lo = (w_r - w_r_hi.astype(F32)).astype(BF16)
        r_bias = _pad_lanes(jnp.concatenate([b_coarse[i], b_fine[i].reshape(-1)]))
        x1, hp, route, cnt = _outproj(i, y_lru, y_ssd, xt, w_out_bf, norm_ffn[i].reshape(1, d),
                                      w_r_hi, w_r_lo, r_bias)
        dest, weight, blk_expert, n_used, row_tok = _route(route, cnt, n_blocks)
        yb = _experts(i, blk_expert, n_used, row_tok, hp, w_gate, w_up, w_down)
        xt = _combine(dest, x1, weight, final_norm.reshape(1, d), yb, final_norm=(i == depth - 1))
    return xt.reshape(bsz, seq, d)
```

```python
import functools

import jax
import jax.numpy as jnp
from jax import lax
from jax.experimental import pallas as pl
from jax.experimental.pallas import tpu as pltpu

F32 = jnp.float32
BF16 = jnp.bfloat16
U32 = jnp.uint32

D_MODEL = 1024
D_LRU = 1024
LRU_HEADS = 16
LRU_BW = 64
RG_C = 8.0
CONV_K = 4
D_SSD = 1024
SSD_HEAD_DIM = 64
SSD_HEADS = 16
SSD_GROUPS = 4
SSD_STATE = 128
SSD_CHUNK = 128
MOE_GROUPS = 4
EXPERTS_PER_GROUP = 8
N_EXPERTS = 32
TOP_K = 2
D_EXPERT = 512
EPS = 1e-6

LANES = 128
SUBLANES = 8
VMEM_LIMIT = 60 * 1024 * 1024

TM_PROJ = 512
T_LRU = 256
TM_EXP = 256
T_TOK = 256
GATE_W = 256
D_HALF = D_MODEL // 2
ROW_TILES = D_MODEL // LANES


def _cparams(sem):
    return pltpu.CompilerParams(dimension_semantics=sem, vmem_limit_bytes=VMEM_LIMIT)


def _const_spec(shape):
    n = len(shape)
    return pl.BlockSpec(shape, lambda *_: (0,) * n)


def _sigmoid(x):
    return 1.0 / (1.0 + jnp.exp(-x))


def _softplus(x):
    return jnp.maximum(x, 0.0) + jnp.log1p(jnp.exp(-jnp.abs(x)))


def _rms(x, w):
    ms = jnp.mean(x * x, axis=-1, keepdims=True)
    return x * lax.rsqrt(ms + EPS) * w


W_COL = 1024
SSD_XBC = D_SSD + 2 * SSD_GROUPS * SSD_STATE


def _inproj_kernel(x_ref, nw_ref, wlx_ref, wlg_ref, wz_ref, wx_ref, wbc_ref, wdt_ref,
                   lx_ref, lg_ref, z_ref, xbc_ref, dt_ref):
    h = _rms(x_ref[...], nw_ref[...]).astype(BF16)
    lx_ref[...] = jnp.dot(h, wlx_ref[...], preferred_element_type=F32).astype(BF16)
    lg_ref[...] = jnp.dot(h, wlg_ref[...], preferred_element_type=F32).astype(BF16)
    z_ref[...] = jnp.dot(h, wz_ref[...], preferred_element_type=F32).astype(BF16)
    xbc_ref[:, 0:W_COL] = jnp.dot(h, wx_ref[...], preferred_element_type=F32).astype(BF16)
    xbc_ref[:, W_COL:] = jnp.dot(h, wbc_ref[...], preferred_element_type=F32).astype(BF16)
    dt_ref[...] = jnp.dot(h, wdt_ref[...], preferred_element_type=F32)


def _inproj(layer, x, nw, w_in, wdt):
    n = x.shape[0]
    tm = TM_PROJ
    row = lambda w: pl.BlockSpec((tm, w), lambda i: (i, 0))
    wcol = lambda c: pl.BlockSpec((None, D_MODEL, W_COL), lambda i: (layer, 0, c))
    return pl.pallas_call(
        _inproj_kernel,
        grid=(n // tm,),
        in_specs=[row(D_MODEL), _const_spec((1, D_MODEL)),
                  wcol(0), wcol(1), wcol(2), wcol(3), wcol(4), _const_spec(wdt.shape)],
        out_specs=[row(D_LRU), row(D_LRU), row(D_SSD), row(SSD_XBC), row(LANES)],
        out_shape=[jax.ShapeDtypeStruct((n, D_LRU), BF16),
                   jax.ShapeDtypeStruct((n, D_LRU), BF16),
                   jax.ShapeDtypeStruct((n, D_SSD), BF16),
                   jax.ShapeDtypeStruct((n, SSD_XBC), BF16),
                   jax.ShapeDtypeStruct((n, LANES), F32)],
        compiler_params=_cparams(("arbitrary",)),
        name="inproj",
    )(x, nw, w_in, w_in, w_in, w_in, w_in, wdt)


def _causal_conv(xbuf, cw_ref, cb_ref, t):
    cw = cw_ref[...]
    acc = cb_ref[...] + cw[0:1, :] * xbuf[pl.ds(SUBLANES - 3, t), :]
    for k in range(1, CONV_K):
        acc = acc + cw[k:k + 1, :] * xbuf[pl.ds(SUBLANES - 3 + k, t), :]
    return acc


def _lru_kernel(lx_ref, lg_ref, cw_ref, cb_ref, wg_ref, ba_ref, bx_ref, lam_ref, nw_ref,
                y_ref, xbuf, a_s, v_s, h_s, hcarry):
    t = T_LRU
    j = pl.program_id(1)

    @pl.when(j == 0)
    def _():
        xbuf[0:SUBLANES, :] = jnp.zeros((SUBLANES, D_LRU), F32)
        hcarry[...] = jnp.zeros_like(hcarry)

    xbuf[SUBLANES:SUBLANES + t, :] = lx_ref[...].astype(F32)
    u = _causal_conv(xbuf, cw_ref, cb_ref, t)
    xbuf[0:SUBLANES, :] = xbuf[t:t + SUBLANES, :]

    lam = lam_ref[...]
    log_sig = jnp.minimum(lam, 0.0) - jnp.log1p(jnp.exp(-jnp.abs(lam)))
    for g in range(D_LRU // GATE_W):
        sl = slice(g * GATE_W, (g + 1) * GATE_W)
        ug = u[:, sl]
        gates = jnp.dot(ug.astype(BF16), wg_ref[g], preferred_element_type=F32)
        r = _sigmoid(gates[:, :GATE_W] + ba_ref[:, sl])
        i = _sigmoid(gates[:, GATE_W:] + bx_ref[:, sl])
        log_a = RG_C * r * log_sig[:, sl]
        a = jnp.exp(log_a)
        mult = jnp.sqrt(1.0 - a * a)
        a_s[:, sl] = a
        v_s[:, sl] = mult * (i * ug)

    row = lax.broadcasted_iota(jnp.int32, (SUBLANES, D_LRU), 0)

    def scan_body(k, hprev):
        r0 = pl.multiple_of(k * SUBLANES, SUBLANES)
        a = a_s[pl.ds(r0, SUBLANES), :]
        v = v_s[pl.ds(r0, SUBLANES), :]
        for s in (1, 2, 4):
            keep = row >= s
            a_sh = jnp.where(keep, pltpu.roll(a, s, 0), 1.0)
            v_sh = jnp.where(keep, pltpu.roll(v, s, 0), 0.0)
            v = v + a * v_sh
            a = a * a_sh
        h = v + a * hprev
        h_s[pl.ds(r0, SUBLANES), :] = h
        return h[SUBLANES - 1:SUBLANES, :]

    hcarry[...] = lax.fori_loop(0, t // SUBLANES, scan_body, hcarry[...], unroll=4)

    g_in = lg_ref[...].astype(F32)
    gelu = 0.5 * g_in * (1.0 + jnp.tanh(0.7978845608028654 * (g_in + 0.044715 * g_in * g_in * g_in)))
    y_ref[...] = _rms(gelu * h_s[...], nw_ref[...]).astype(BF16)


def _lru(lx, lg, cw, cb, wg, ba, bx, lam, nw, bsz, seq):
    t = T_LRU
    nj = seq // t
    row = pl.BlockSpec((t, D_LRU), lambda b, j: (b * nj + j, 0))
    vec = _const_spec((1, D_LRU))
    return pl.pallas_call(
        _lru_kernel,
        grid=(bsz, nj),
        in_specs=[row, row, _const_spec(cw.shape), vec, _const_spec(wg.shape),
                  vec, vec, vec, vec],
        out_specs=row,
        out_shape=jax.ShapeDtypeStruct((bsz * seq, D_LRU), BF16),
        scratch_shapes=[pltpu.VMEM((t + SUBLANES, D_LRU), F32),
                        pltpu.VMEM((t, D_LRU), F32),
                        pltpu.VMEM((t, D_LRU), F32),
                        pltpu.VMEM((t, D_LRU), F32),
                        pltpu.VMEM((1, D_LRU), F32)],
        compiler_params=_cparams(("arbitrary", "arbitrary")),
        name="rglru",
    )(lx, lg, cw, cb, wg, ba, bx, lam, nw)


def _split3(x):
    hi = x.astype(BF16)
    r1 = x - hi.astype(F32)
    mid = r1.astype(BF16)
    lo = (r1 - mid.astype(F32)).astype(BF16)
    return hi, mid, lo


def _ssd_kernel(xbc_ref, z_ref, dt_ref, cw_ref, cb_ref, dtb_ref, alog_ref, dvec_ref, nw_ref,
                y_ref, xbuf, state, y_s):
    t = SSD_CHUNK
    j = pl.program_id(1)

    @pl.when(j == 0)
    def _():
        xbuf[0:SUBLANES, :] = jnp.zeros((SUBLANES, xbuf.shape[1]), F32)
        state[...] = jnp.zeros_like(state)

    xbuf[SUBLANES:SUBLANES + t, :] = xbc_ref[...].astype(F32)
    u = _causal_conv(xbuf, cw_ref, cb_ref, t)
    xbuf[0:SUBLANES, :] = xbuf[t:t + SUBLANES, :]
    xc = u * _sigmoid(u)
    gn = SSD_GROUPS * SSD_STATE
    xs = xc[:, :D_SSD]
    bm = xc[:, D_SSD:D_SSD + gn]
    cm = xc[:, D_SSD + gn:]

    dt = _softplus(dt_ref[...] + dtb_ref[...])
    d_a = dt * (-jnp.exp(alog_ref[...]))
    ri = lax.broadcasted_iota(jnp.int32, (t, t), 0)
    ci = lax.broadcasted_iota(jnp.int32, (t, t), 1)
    causal = ri >= ci
    tril = causal.astype(BF16)
    hi, mid, lo = _split3(d_a)
    a_cs = (jnp.dot(tril, hi, preferred_element_type=F32)
            + jnp.dot(tril, mid, preferred_element_type=F32)
            + jnp.dot(tril, lo, preferred_element_type=F32))
    a_cs_t = a_cs.T
    dt_t = dt.T
    a_last_t = a_cs_t[:, t - 1:t]
    w_state_t = jnp.exp(a_last_t - a_cs_t) * dt_t
    chunk_decay_t = jnp.exp(a_last_t)

    lane = lax.broadcasted_iota(jnp.int32, (1, LANES), 1)
    first = lane < SSD_HEAD_DIM

    heads_per_group = SSD_HEADS // SSD_GROUPS
    for g in range(SSD_GROUPS):
        gsl = slice(g * SSD_STATE, (g + 1) * SSD_STATE)
        c_g = cm[:, gsl].astype(BF16)
        b_g = bm[:, gsl]
        scores = lax.dot_general(c_g, b_g.astype(BF16), (((1,), (1,)), ((), ())),
                                 preferred_element_type=F32)
        b_t = b_g.T
        st_g = state[g]
        y_off = jnp.dot(c_g, st_g.astype(BF16), preferred_element_type=F32)
        for q in range(heads_per_group // 2):
            h0 = g * heads_per_group + 2 * q
            psl = slice(h0 * SSD_HEAD_DIM, (h0 + 2) * SSD_HEAD_DIM)
            lsl = slice(2 * q * SSD_HEAD_DIM, (2 * q + 2) * SSD_HEAD_DIM)
            x_pair = xs[:, psl].astype(BF16)
            yd, ns, ea, cd = [], [], [], []
            for h in (h0, h0 + 1):
                col = jnp.broadcast_to(a_cs[:, h:h + 1], (t, t))
                seg = col - a_cs_t[h:h + 1, :]
                lmat = jnp.exp(jnp.where(causal, seg, -1e30))
                m = (scores * lmat * dt_t[h:h + 1, :]).astype(BF16)
                yd.append(jnp.dot(m, x_pair, preferred_element_type=F32))
                bw = (b_t * w_state_t[h:h + 1, :]).astype(BF16)
                ns.append(jnp.dot(bw, x_pair, preferred_element_type=F32))
                ea.append(jnp.exp(col))
                cd.append(jnp.broadcast_to(chunk_decay_t[h:h + 1, :], (1, LANES)))
            y_pair = (jnp.where(first, yd[0], yd[1])
                      + jnp.where(first, ea[0], ea[1]) * y_off[:, lsl])
            y_s[:, psl] = y_pair
            state[g, :, lsl] = (st_g[:, lsl] * jnp.where(first, cd[0], cd[1])
                                + jnp.where(first, ns[0], ns[1]))

    zf = z_ref[...].astype(F32)
    y = (y_s[...] + xs * dvec_ref[...]) * (zf * _sigmoid(zf))
    y_ref[...] = _rms(y, nw_ref[...]).astype(BF16)


def _ssd(xbc, z, dt, cw, cb, dtb, alog, dvec, nw, bsz, seq):
    t = SSD_CHUNK
    nj = seq // t
    dx = xbc.shape[1]
    row = lambda w: pl.BlockSpec((t, w), lambda b, j: (b * nj + j, 0))
    return pl.pallas_call(
        _ssd_kernel,
        grid=(bsz, nj),
        in_specs=[row(dx), row(D_SSD), row(LANES), _const_spec(cw.shape), _const_spec((1, dx)),
                  _const_spec((1, LANES)), _const_spec((1, LANES)),
                  _const_spec((1, D_SSD)), _const_spec((1, D_SSD))],
        out_specs=row(D_SSD),
        out_shape=jax.ShapeDtypeStruct((bsz * seq, D_SSD), BF16),
        scratch_shapes=[pltpu.VMEM((t + SUBLANES, dx), F32),
                        pltpu.VMEM((SSD_GROUPS, SSD_STATE, D_SSD // SSD_GROUPS), F32),
                        pltpu.VMEM((t, D_SSD), F32)],
        compiler_params=_cparams(("arbitrary", "arbitrary")),
        name="ssd",
    )(xbc, z, dt, cw, cb, dtb, alog, dvec, nw)


ROUTE_E, ROUTE_W, ROUTE_R = 0, 2, 4
NEG_BIG = -1e30


def _first_argmax(vals, lane_f):
    m = jnp.max(vals, axis=-1, keepdims=True)
    idx = jnp.min(jnp.where(vals == m, lane_f, float(LANES)), axis=-1, keepdims=True)
    return m, idx


def _outproj_kernel(yl_ref, ys_ref, x_ref, wo_ref, nw_ref, wrh_ref, wrl_ref, rb_ref,
                    x1_ref, hp_ref, route_ref, cnt_ref, stril, running):
    tm = x_ref.shape[0]

    @pl.when(pl.program_id(0) == 0)
    def _():
        ri = lax.broadcasted_iota(jnp.int32, (tm, tm), 0)
        ci = lax.broadcasted_iota(jnp.int32, (tm, tm), 1)
        stril[...] = (ri > ci).astype(BF16)
        running[...] = jnp.zeros_like(running)

    x1 = (x_ref[...]
          + jnp.dot(yl_ref[...], wo_ref[0:D_LRU, :], preferred_element_type=F32)
          + jnp.dot(ys_ref[...], wo_ref[D_LRU:, :], preferred_element_type=F32))
    x1_ref[...] = x1
    h = _rms(x1, nw_ref[...])
    h_hi = h.astype(BF16)
    h_rt = h_hi.astype(F32)
    bits = lax.bitcast_convert_type(h_rt, U32)
    hp_ref[...] = bits[:, D_HALF:] | (bits[:, :D_HALF] >> 16)
    h_lo = (h - h_rt).astype(BF16)
    w_hi = wrh_ref[...]
    logits = (jnp.dot(h_hi, w_hi, preferred_element_type=F32)
              + jnp.dot(h_lo, w_hi, preferred_element_type=F32)
              + jnp.dot(h_hi, wrl_ref[...], preferred_element_type=F32)) + rb_ref[...]

    lane = lax.broadcasted_iota(jnp.int32, (tm, LANES), 1)
    lane_f = lane.astype(F32)
    is_c = lane < MOE_GROUPS
    lc = jnp.where(is_c, logits, NEG_BIG)
    m_c, g_idx = _first_argmax(lc, lane_f)
    g_w = 1.0 / jnp.sum(jnp.where(is_c, jnp.exp(lc - m_c), 0.0), axis=-1, keepdims=True)
    lo = float(MOE_GROUPS) + float(EXPERTS_PER_GROUP) * g_idx
    is_f = (lane_f >= lo) & (lane_f < lo + float(EXPERTS_PER_GROUP))
    lf = jnp.where(is_f, logits, NEG_BIG)
    v1, i1 = _first_argmax(lf, lane_f)
    v2, i2 = _first_argmax(jnp.where(lane_f == i1, NEG_BIG, lf), lane_f)
    ex = jnp.exp(v2 - v1)
    w1 = g_w / (1.0 + ex)
    w2 = g_w * ex / (1.0 + ex)

    oh1 = (lane_f == i1).astype(F32)
    oh2 = (lane_f == i2).astype(F32)
    both = oh1 + oh2
    before = jnp.dot(stril[...], both.astype(BF16), preferred_element_type=F32) + running[...]
    r1 = jnp.sum(oh1 * before, axis=-1, keepdims=True)
    r2 = jnp.sum(oh2 * before, axis=-1, keepdims=True)
    running[...] = running[...] + jnp.sum(both, axis=0, keepdims=True)
    cnt_ref[...] = running[...]

    e1 = i1 - float(MOE_GROUPS)
    e2 = i2 - float(MOE_GROUPS)
    route = jnp.zeros((tm, LANES), F32)
    for off, (a, b) in ((ROUTE_E, (e1, e2)), (ROUTE_W, (w1, w2)), (ROUTE_R, (r1, r2))):
        route = jnp.where(lane == off, a, jnp.where(lane == off + 1, b, route))
    route_ref[...] = route


def _outproj(layer, yl, ys, x, wo, nw, wrh, wrl, rb):
    n = x.shape[0]
    tm = TM_PROJ
    row = lambda w: pl.BlockSpec((tm, w), lambda i: (i, 0))
    return pl.pallas_call(
        _outproj_kernel,
        grid=(n // tm,),
        in_specs=[row(D_LRU), row(D_SSD), row(D_MODEL),
                  pl.BlockSpec((None,) + wo.shape[1:], lambda i: (layer, 0, 0)),
                  _const_spec((1, D_MODEL)),
                  _const_spec(wrh.shape), _const_spec(wrl.shape), _const_spec((1, LANES))],
        out_specs=[row(D_MODEL), row(D_HALF), row(LANES), _const_spec((1, LANES))],
        out_shape=[jax.ShapeDtypeStruct((n, D_MODEL), F32),
                   jax.ShapeDtypeStruct((n, D_HALF), U32),
                   jax.ShapeDtypeStruct((n, LANES), F32),
                   jax.ShapeDtypeStruct((1, LANES), F32)],
        scratch_shapes=[pltpu.VMEM((tm, tm), BF16), pltpu.VMEM((1, LANES), F32)],
        compiler_params=_cparams(("arbitrary",)),
        name="outproj",
    )(yl, ys, x, wo, nw, wrh, wrl, rb)


SCHED_EXPERT, SCHED_FIRST, SCHED_NEXT, SCHED_SLOT = 0, 1, 2, 3


def _expert_kernel(sch_ref, nu_ref, rt_ref, hp_hbm, wg_hbm, wu_hbm, wd_hbm, yb_ref,
                   hp_v, xs, wg_f, wu_f, wd_f, wg_s, wu_s, wd_s, sem, wsem, *, layer):
    i = pl.program_id(0)
    e = sch_ref[SCHED_EXPERT, i]
    slot = sch_ref[SCHED_SLOT, i]
    active = i < nu_ref[0]

    def weight_copies(expert, s):
        return (pltpu.make_async_copy(wg_hbm.at[layer, expert], wg_f.at[s], wsem.at[s, 0]),
                pltpu.make_async_copy(wu_hbm.at[layer, expert], wu_f.at[s], wsem.at[s, 1]),
                pltpu.make_async_copy(wd_hbm.at[layer, expert], wd_f.at[s], wsem.at[s, 2]))

    @pl.when(i == 0)
    def _():
        for cp in weight_copies(e, slot):
            cp.start()
        cp = pltpu.make_async_copy(hp_hbm, hp_v, sem)
        cp.start()
        cp.wait()

    @pl.when(active & (sch_ref[SCHED_FIRST, i] == 1))
    def _():
        for cp in weight_copies(e, slot):
            cp.wait()
        wg_s[...] = wg_f[slot].astype(BF16)
        wu_s[...] = wu_f[slot].astype(BF16)
        wd_s[...] = wd_f[slot].astype(BF16)
        nxt = sch_ref[SCHED_NEXT, i]

        @pl.when(nxt != e)
        def _():
            for cp in weight_copies(nxt, 1 - slot):
                cp.start()

    @pl.when(active)
    def _():
        for r in range(TM_EXP):
            xs[r:r + 1, :] = hp_v[pl.ds(rt_ref[0, r], 1), :]
        packed = xs[...]
        x_lo = lax.bitcast_convert_type(packed << 16, F32).astype(BF16)
        x_hi = lax.bitcast_convert_type((packed >> 16) << 16, F32).astype(BF16)
        gate = (jnp.dot(x_lo, wg_s[0:D_HALF, :], preferred_element_type=F32)
                + jnp.dot(x_hi, wg_s[D_HALF:, :], preferred_element_type=F32))
        up = (jnp.dot(x_lo, wu_s[0:D_HALF, :], preferred_element_type=F32)
              + jnp.dot(x_hi, wu_s[D_HALF:, :], preferred_element_type=F32))
        hid = (gate * _sigmoid(gate) * up).astype(BF16)
        y = jnp.dot(hid, wd_s[...], preferred_element_type=F32)
        for c in range(ROW_TILES):
            yb_ref[pl.ds(c, TM_EXP, stride=ROW_TILES), :] = y[:, c * LANES:(c + 1) * LANES]

    @pl.when(jnp.logical_not(active))
    def _():
        yb_ref[...] = jnp.zeros_like(yb_ref)


def _experts(layer, sched, n_used, row_tok, hp, wg, wu, wd):
    n_blocks = sched.shape[1]
    tm = TM_EXP
    hbm = pl.BlockSpec(memory_space=pl.ANY)
    grid_spec = pltpu.PrefetchScalarGridSpec(
        num_scalar_prefetch=2,
        grid=(n_blocks,),
        in_specs=[pl.BlockSpec((None, 1, tm), lambda i, sch, nu: (i, 0, 0), memory_space=pltpu.SMEM),
                  hbm, hbm, hbm, hbm],
        out_specs=pl.BlockSpec((tm * ROW_TILES, LANES), lambda i, sch, nu: (i, 0)),
        scratch_shapes=[pltpu.VMEM(hp.shape, U32),
                        pltpu.VMEM((tm, D_HALF), U32),
                        pltpu.VMEM((2, D_MODEL, D_EXPERT), F32),
                        pltpu.VMEM((2, D_MODEL, D_EXPERT), F32),
                        pltpu.VMEM((2, D_EXPERT, D_MODEL), F32),
                        pltpu.VMEM((D_MODEL, D_EXPERT), BF16),
                        pltpu.VMEM((D_MODEL, D_EXPERT), BF16),
                        pltpu.VMEM((D_EXPERT, D_MODEL), BF16),
                        pltpu.SemaphoreType.DMA(()),
                        pltpu.SemaphoreType.DMA((2, 3))],
    )
    return pl.pallas_call(
        functools.partial(_expert_kernel, layer=layer),
        grid_spec=grid_spec,
        out_shape=jax.ShapeDtypeStruct((n_blocks * tm * ROW_TILES, LANES), F32),
        compiler_params=_cparams(("arbitrary",)),
        name="experts",
    )(sched, n_used, row_tok.reshape(n_blocks, 1, tm), hp, wg, wu, wd)


def _combine_kernel(dcur_ref, dnxt_ref, x_ref, w_ref, nw_ref, yb_ref, o_ref, gbuf, sems,
                    *, final_norm):
    t = T_TOK
    i = pl.program_id(0)
    n_steps = pl.num_programs(0)
    slot = i % 2

    def copy(dref, s, k, tok):
        src = pl.ds(pl.multiple_of(dref[k, tok] * ROW_TILES, ROW_TILES), ROW_TILES)
        row0 = tok * ROW_TILES
        dst = pl.ds(row0 if isinstance(tok, int) else pl.multiple_of(row0, ROW_TILES), ROW_TILES)
        return pltpu.make_async_copy(yb_ref.at[src, :], gbuf.at[s, k, dst, :], sems.at[s])

    def issue(dref, s):
        def body(tok, c):
            for k in range(TOP_K):
                copy(dref, s, k, tok).start()
            return c
        lax.fori_loop(0, t, body, 0, unroll=8)

    @pl.when(i == 0)
    def _():
        issue(dcur_ref, 0)

    @pl.when(i + 1 < n_steps)
    def _():
        for tok in range(t):
            for k in range(TOP_K):
                copy(dnxt_ref, 1 - slot, k, tok).start()

    def wait(tok, c):
        for k in range(TOP_K):
            copy(dcur_ref, slot, k, tok).wait()
        return c

    lax.fori_loop(0, t, wait, 0, unroll=8)

    w = w_ref[...]
    w0 = w[:, 0:1]
    w1 = w[:, 1:2]
    parts = []
    for c in range(ROW_TILES):
        g0 = gbuf[slot, 0, pl.ds(c, t, stride=ROW_TILES), :]
        g1 = gbuf[slot, 1, pl.ds(c, t, stride=ROW_TILES), :]
        parts.append(x_ref[:, c * LANES:(c + 1) * LANES] + w0 * g0 + w1 * g1)
    out = jnp.concatenate(parts, axis=-1)
    if final_norm:
        out = _rms(out, nw_ref[...])
    o_ref[...] = out


def _tile_dest(dest, t):
    n = dest.shape[0]
    return dest.reshape(n // t, t, TOP_K).transpose(0, 2, 1)


def _combine(dest, x, w, nw, yb, final_norm):
    n = x.shape[0]
    t = T_TOK
    n_steps = n // t
    dest3 = _tile_dest(dest, t)
    dspec = lambda f: pl.BlockSpec((None, TOP_K, t), lambda i: (f(i), 0, 0), memory_space=pltpu.SMEM)
    return pl.pallas_call(
        functools.partial(_combine_kernel, final_norm=final_norm),
        grid=(n_steps,),
        in_specs=[dspec(lambda i: i), dspec(lambda i: jnp.minimum(i + 1, n_steps - 1)),
                  pl.BlockSpec((t, D_MODEL), lambda i: (i, 0)),
                  pl.BlockSpec((t, TOP_K), lambda i: (i, 0)),
                  _const_spec((1, D_MODEL)),
                  pl.BlockSpec(memory_space=pl.ANY)],
        out_specs=pl.BlockSpec((t, D_MODEL), lambda i: (i, 0)),
        out_shape=jax.ShapeDtypeStruct((n, D_MODEL), F32),
        scratch_shapes=[pltpu.VMEM((2, TOP_K, t * ROW_TILES, LANES), F32),
                        pltpu.SemaphoreType.DMA((2,))],
        compiler_params=_cparams(("arbitrary",)),
        name="combine",
    )(dest3, dest3, x, w, nw, yb)


def _route(route, cnt, n_blocks):
    n = route.shape[0]
    expert = route[:, ROUTE_E:ROUTE_E + TOP_K].astype(jnp.int32)
    weight = route[:, ROUTE_W:ROUTE_W + TOP_K]
    rank = route[:, ROUTE_R:ROUTE_R + TOP_K].astype(jnp.int32)
    counts = cnt[0, MOE_GROUPS:MOE_GROUPS + N_EXPERTS].astype(jnp.int32)
    padded = (counts + TM_EXP - 1) // TM_EXP * TM_EXP
    pad_end = jnp.cumsum(padded)
    pad_start = pad_end - padded
    onehot = expert[:, :, None] == jnp.arange(N_EXPERTS, dtype=jnp.int32)
    dest = (jnp.sum(jnp.where(onehot, pad_start, 0), axis=-1) + rank).astype(jnp.int32)
    n_used = (pad_end[-1:] // TM_EXP).astype(jnp.int32)
    blk = jnp.arange(n_blocks, dtype=jnp.int32)
    blk_expert = jnp.minimum(jnp.sum(pad_end[None, :] <= (blk * TM_EXP)[:, None], axis=1),
                             N_EXPERTS - 1).astype(jnp.int32)
    eid = jnp.arange(N_EXPERTS, dtype=jnp.int32)
    nonempty = counts > 0
    later = (eid[None, :] > eid[:, None]) & nonempty[None, :]
    next_used = jnp.min(jnp.where(later, eid[None, :], N_EXPERTS), axis=1)
    next_used = jnp.where(next_used == N_EXPERTS, eid, next_used)
    ordinal = jnp.cumsum(nonempty.astype(jnp.int32)) - 1
    first = ((blk * TM_EXP == pad_start[blk_expert]) & (blk < n_used[0])).astype(jnp.int32)
    sched = jnp.stack([blk_expert, first, next_used[blk_expert], ordinal[blk_expert] % 2])
    token = jnp.broadcast_to(jnp.arange(n, dtype=jnp.int32)[:, None], (n, TOP_K))
    row_tok = jnp.zeros((n_blocks * TM_EXP,), jnp.int32).at[dest.reshape(-1)].add(token.reshape(-1))
    return dest, weight, sched.astype(jnp.int32), n_used, row_tok


def _gate_blocks(wa, wx):
    per = GATE_W // LRU_BW
    eye = jnp.eye(per, dtype=F32)

    def bd(w):
        w = w.reshape(LRU_HEADS // per, per, LRU_BW, LRU_BW)
        full = jnp.einsum('gpij,pq->gpiqj', w, eye)
        return full.reshape(LRU_HEADS // per, GATE_W, GATE_W)

    return jnp.concatenate([bd(wa), bd(wx)], axis=-1).astype(BF16)


def _pad_rows(w, rows):
    return jnp.pad(w, ((0, rows - w.shape[0]), (0, 0)))


def _pad_lanes(v):
    return jnp.pad(v, (0, LANES - v.shape[0])).reshape(1, LANES)


def kernel(x, norm_mix, w_in, lru_conv_w, lru_conv_b, lru_wa, lru_ba, lru_wx, lru_bx, lru_lambda, lru_norm, ssd_conv_w, ssd_conv_b, ssd_dt_bias, ssd_a_log, ssd_d, ssd_norm, w_out, norm_ffn, w_coarse, b_coarse, w_fine, b_fine, w_gate, w_up, w_down, final_norm):
    bsz, seq, d = x.shape
    n = bsz * seq
    depth = w_in.shape[0]
    n_assign = n * TOP_K
    n_blocks = -(-(n_assign + N_EXPERTS * (TM_EXP - 1)) // TM_EXP)
    o_dt = 2 * D_LRU + D_SSD + SSD_XBC
    w_in_bf = w_in.astype(BF16)
    w_out_bf = w_out.astype(BF16)

    xt = x.reshape(n, d)
    for i in range(depth):
        wdt = jnp.pad(w_in_bf[i, :, o_dt:], ((0, 0), (0, LANES - SSD_HEADS)))
        lx, lg, z, xbc, dt = _inproj(i, xt, norm_mix[i].reshape(1, d), w_in_bf, wdt)
        y_lru = _lru(lx, lg, _pad_rows(lru_conv_w[i], SUBLANES), lru_conv_b[i].reshape(1, -1),
                     _gate_blocks(lru_wa[i], lru_wx[i]), lru_ba[i].reshape(1, -1),
                     lru_bx[i].reshape(1, -1), lru_lambda[i].reshape(1, -1),
                     lru_norm[i].reshape(1, -1), bsz, seq)
        y_ssd = _ssd(xbc, z, dt, _pad_rows(ssd_conv_w[i], SUBLANES), ssd_conv_b[i].reshape(1, -1),
                     _pad_lanes(ssd_dt_bias[i]), _pad_lanes(ssd_a_log[i]),
                     jnp.repeat(ssd_d[i], SSD_HEAD_DIM).reshape(1, -1),
                     ssd_norm[i].reshape(1, -1), bsz, seq)
        w_r = jnp.concatenate([w_coarse[i], w_fine[i].transpose(1, 0, 2).reshape(d, N_EXPERTS)], axis=1)
        w_r = jnp.pad(w_r, ((0, 0), (0, LANES - w_r.shape[1])))
        w_r_hi = w_r.astype(BF16)
        w_r_lo = (w_r - w_r_hi.astype(F32)).astype(BF16)
        r_bias = _pad_lanes(jnp.concatenate([b_coarse[i], b_fine[i].reshape(-1)]))
        x1, hp, route, cnt = _outproj(i, y_lru, y_ssd, xt, w_out_bf, norm_ffn[i].reshape(1, d),
                                      w_r_hi, w_r_lo, r_bias)
        dest, weight, sched, n_used, row_tok = _route(route, cnt, n_blocks)
        yb = _experts(i, sched, n_used, row_tok, hp, w_gate, w_up, w_down)
        xt = _combine(dest, x1, weight, final_norm.reshape(1, d), yb, final_norm=(i == depth - 1))
    return xt.reshape(bsz, seq, d)
```

```python
import functools

import jax
import jax.numpy as jnp
from jax import lax
from jax.experimental import pallas as pl
from jax.experimental.pallas import tpu as pltpu

F32 = jnp.float32
BF16 = jnp.bfloat16
U32 = jnp.uint32

D_MODEL = 1024
D_LRU = 1024
LRU_HEADS = 16
LRU_BW = 64
RG_C = 8.0
CONV_K = 4
D_SSD = 1024
SSD_HEAD_DIM = 64
SSD_HEADS = 16
SSD_GROUPS = 4
SSD_STATE = 128
SSD_CHUNK = 128
MOE_GROUPS = 4
EXPERTS_PER_GROUP = 8
N_EXPERTS = 32
TOP_K = 2
D_EXPERT = 512
EPS = 1e-6

LANES = 128
SUBLANES = 8
VMEM_LIMIT = 60 * 1024 * 1024

TM_PROJ = 512
T_LRU = 256
TM_EXP = 256
T_TOK = 256
GATE_W = 256
D_HALF = D_MODEL // 2
ROW_TILES = D_MODEL // LANES


def _cparams(sem):
    return pltpu.CompilerParams(dimension_semantics=sem, vmem_limit_bytes=VMEM_LIMIT)


def _const_spec(shape):
    n = len(shape)
    return pl.BlockSpec(shape, lambda *_: (0,) * n)


def _sigmoid(x):
    return 1.0 / (1.0 + jnp.exp(-x))


def _log1p(e):
    u = 1.0 + e
    d = u - 1.0
    return jnp.where(d == 0.0, e, jnp.log(u) * (e / jnp.where(d == 0.0, 1.0, d)))


def _softplus(x):
    return jnp.maximum(x, 0.0) + _log1p(jnp.exp(-jnp.abs(x)))


def _rms(x, w):
    ms = jnp.mean(x * x, axis=-1, keepdims=True)
    return x * lax.rsqrt(ms + EPS) * w


W_COL = 1024
SSD_XBC = D_SSD + 2 * SSD_GROUPS * SSD_STATE


def _inproj_kernel(x_ref, nw_ref, wlx_ref, wlg_ref, wz_ref, wx_ref, wbc_ref, wdt_ref,
                   lx_ref, lg_ref, z_ref, xbc_ref, dt_ref):
    h = _rms(x_ref[...], nw_ref[...]).astype(BF16)
    lx_ref[...] = jnp.dot(h, wlx_ref[...], preferred_element_type=F32).astype(BF16)
    lg_ref[...] = jnp.dot(h, wlg_ref[...], preferred_element_type=F32).astype(BF16)
    z_ref[...] = jnp.dot(h, wz_ref[...], preferred_element_type=F32).astype(BF16)
    xbc_ref[:, 0:W_COL] = jnp.dot(h, wx_ref[...], preferred_element_type=F32).astype(BF16)
    xbc_ref[:, W_COL:] = jnp.dot(h, wbc_ref[...], preferred_element_type=F32).astype(BF16)
    dt_ref[...] = jnp.dot(h, wdt_ref[...], preferred_element_type=F32)


def _inproj(layer, x, nw, w_in, wdt):
    n = x.shape[0]
    tm = TM_PROJ
    row = lambda w: pl.BlockSpec((tm, w), lambda i: (i, 0))
    wcol = lambda c: pl.BlockSpec((None, D_MODEL, W_COL), lambda i: (layer, 0, c))
    return pl.pallas_call(
        _inproj_kernel,
        grid=(n // tm,),
        in_specs=[row(D_MODEL), _const_spec((1, D_MODEL)),
                  wcol(0), wcol(1), wcol(2), wcol(3), wcol(4), _const_spec(wdt.shape)],
        out_specs=[row(D_LRU), row(D_LRU), row(D_SSD), row(SSD_XBC), row(LANES)],
        out_shape=[jax.ShapeDtypeStruct((n, D_LRU), BF16),
                   jax.ShapeDtypeStruct((n, D_LRU), BF16),
                   jax.ShapeDtypeStruct((n, D_SSD), BF16),
                   jax.ShapeDtypeStruct((n, SSD_XBC), BF16),
                   jax.ShapeDtypeStruct((n, LANES), F32)],
        compiler_params=_cparams(("arbitrary",)),
        name="inproj",
    )(x, nw, w_in, w_in, w_in, w_in, w_in, wdt)


def _causal_conv(xbuf, cw_ref, cb_ref, t):
    cw = cw_ref[...]
    n = t + SUBLANES
    full = xbuf[...]
    acc = cb_ref[...] + cw[CONV_K - 1:CONV_K, :] * full[SUBLANES:, :]
    for k in range(CONV_K - 1):
        shifted = pltpu.roll(full, n - (SUBLANES - 3 + k), 0)[0:t, :]
        acc = acc + cw[k:k + 1, :] * shifted
    return acc


def _lru_kernel(lx_ref, lg_ref, cw_ref, cb_ref, wg_ref, ba_ref, bx_ref, lam_ref, nw_ref,
                y_ref, xbuf, a_s, v_s, h_s, hcarry):
    t = T_LRU
    j = pl.program_id(1)

    @pl.when(j == 0)
    def _():
        xbuf[0:SUBLANES, :] = jnp.zeros((SUBLANES, D_LRU), F32)
        hcarry[...] = jnp.zeros_like(hcarry)

    xbuf[SUBLANES:SUBLANES + t, :] = lx_ref[...].astype(F32)
    u = _causal_conv(xbuf, cw_ref, cb_ref, t)
    xbuf[0:SUBLANES, :] = xbuf[t:t + SUBLANES, :]

    lam = lam_ref[...]
    log_sig = jnp.minimum(lam, 0.0) - _log1p(jnp.exp(-jnp.abs(lam)))
    for g in range(D_LRU // GATE_W):
        sl = slice(g * GATE_W, (g + 1) * GATE_W)
        ug = u[:, sl]
        gates = jnp.dot(ug.astype(BF16), wg_ref[g], preferred_element_type=F32)
        r = _sigmoid(gates[:, :GATE_W] + ba_ref[:, sl])
        i = _sigmoid(gates[:, GATE_W:] + bx_ref[:, sl])
        log_a = RG_C * r * log_sig[:, sl]
        a = jnp.exp(log_a)
        mult = jnp.sqrt(1.0 - a * a)
        a_s[:, sl] = a
        v_s[:, sl] = mult * (i * ug)

    row = lax.broadcasted_iota(jnp.int32, (SUBLANES, D_LRU), 0)

    def scan_body(k, hprev):
        r0 = pl.multiple_of(k * SUBLANES, SUBLANES)
        a = a_s[pl.ds(r0, SUBLANES), :]
        v = v_s[pl.ds(r0, SUBLANES), :]
        for s in (1, 2, 4):
            keep = row >= s
            a_sh = jnp.where(keep, pltpu.roll(a, s, 0), 1.0)
            v_sh = jnp.where(keep, pltpu.roll(v, s, 0), 0.0)
            v = v + a * v_sh
            a = a * a_sh
        h = v + a * hprev
        h_s[pl.ds(r0, SUBLANES), :] = h
        return h[SUBLANES - 1:SUBLANES, :]

    hcarry[...] = lax.fori_loop(0, t // SUBLANES, scan_body, hcarry[...], unroll=4)

    g_in = lg_ref[...].astype(F32)
    gelu = 0.5 * g_in * (1.0 + jnp.tanh(0.7978845608028654 * (g_in + 0.044715 * g_in * g_in * g_in)))
    y_ref[...] = _rms(gelu * h_s[...], nw_ref[...]).astype(BF16)


def _lru(lx, lg, cw, cb, wg, ba, bx, lam, nw, bsz, seq):
    t = T_LRU
    nj = seq // t
    row = pl.BlockSpec((t, D_LRU), lambda b, j: (b * nj + j, 0))
    vec = _const_spec((1, D_LRU))
    return pl.pallas_call(
        _lru_kernel,
        grid=(bsz, nj),
        in_specs=[row, row, _const_spec(cw.shape), vec, _const_spec(wg.shape),
                  vec, vec, vec, vec],
        out_specs=row,
        out_shape=jax.ShapeDtypeStruct((bsz * seq, D_LRU), BF16),
        scratch_shapes=[pltpu.VMEM((t + SUBLANES, D_LRU), F32),
                        pltpu.VMEM((t, D_LRU), F32),
                        pltpu.VMEM((t, D_LRU), F32),
                        pltpu.VMEM((t, D_LRU), F32),
                        pltpu.VMEM((1, D_LRU), F32)],
        compiler_params=_cparams(("arbitrary", "arbitrary")),
        name="rglru",
    )(lx, lg, cw, cb, wg, ba, bx, lam, nw)


def _split3(x):
    hi = x.astype(BF16)
    r1 = x - hi.astype(F32)
    mid = r1.astype(BF16)
    lo = (r1 - mid.astype(F32)).astype(BF16)
    return hi, mid, lo


def _ssd_kernel(xbc_ref, z_ref, dt_ref, cw_ref, cb_ref, dtb_ref, alog_ref, dvec_ref, nw_ref,
                tril_ref, fut_ref, y_ref, xbuf, state, y_s):
    t = SSD_CHUNK
    j = pl.program_id(1)

    @pl.when(j == 0)
    def _():
        xbuf[0:SUBLANES, :] = jnp.zeros((SUBLANES, xbuf.shape[1]), F32)
        state[...] = jnp.zeros_like(state)

    xbuf[SUBLANES:SUBLANES + t, :] = xbc_ref[...].astype(F32)
    u = _causal_conv(xbuf, cw_ref, cb_ref, t)
    xbuf[0:SUBLANES, :] = xbuf[t:t + SUBLANES, :]
    xc = u * _sigmoid(u)
    gn = SSD_GROUPS * SSD_STATE
    xs = xc[:, :D_SSD]
    bm = xc[:, D_SSD:D_SSD + gn]
    cm = xc[:, D_SSD + gn:]

    dt = _softplus(dt_ref[...] + dtb_ref[...])
    d_a = dt * (-jnp.exp(alog_ref[...]))
    tril = tril_ref[...]
    future = fut_ref[...]
    hi, mid, lo = _split3(d_a)
    a_cs = (jnp.dot(tril, hi, preferred_element_type=F32)
            + jnp.dot(tril, mid, preferred_element_type=F32)
            + jnp.dot(tril, lo, preferred_element_type=F32))
    a_cs_t = a_cs.T
    dt_t = dt.T
    a_last_t = a_cs_t[:, t - 1:t]
    w_state_t = jnp.exp(a_last_t - a_cs_t) * dt_t
    chunk_decay_t = jnp.exp(a_last_t)

    lane = lax.broadcasted_iota(jnp.int32, (1, LANES), 1)
    first = lane < SSD_HEAD_DIM

    heads_per_group = SSD_HEADS // SSD_GROUPS
    for g in range(SSD_GROUPS):
        gsl = slice(g * SSD_STATE, (g + 1) * SSD_STATE)
        c_g = cm[:, gsl].astype(BF16)
        b_g = bm[:, gsl]
        scores = lax.dot_general(c_g, b_g.astype(BF16), (((1,), (1,)), ((), ())),
                                 preferred_element_type=F32)
        b_t = b_g.T
        st_g = state[g]
        y_off = jnp.dot(c_g, st_g.astype(BF16), preferred_element_type=F32)
        for q in range(heads_per_group // 2):
            h0 = g * heads_per_group + 2 * q
            psl = slice(h0 * SSD_HEAD_DIM, (h0 + 2) * SSD_HEAD_DIM)
            lsl = slice(2 * q * SSD_HEAD_DIM, (2 * q + 2) * SSD_HEAD_DIM)
            x_pair = xs[:, psl].astype(BF16)
            yd, ns, ea, cd = [], [], [], []
            for h in (h0, h0 + 1):
                col = jnp.broadcast_to(a_cs[:, h:h + 1], (t, t))
                seg = col - a_cs_t[h:h + 1, :]
                lmat = jnp.exp(seg + future)
                m = (scores * lmat * dt_t[h:h + 1, :]).astype(BF16)
                yd.append(jnp.dot(m, x_pair, preferred_element_type=F32))
                bw = (b_t * w_state_t[h:h + 1, :]).astype(BF16)
                ns.append(jnp.dot(bw, x_pair, preferred_element_type=F32))
                ea.append(jnp.exp(col))
                cd.append(jnp.broadcast_to(chunk_decay_t[h:h + 1, :], (1, LANES)))
            y_pair = (jnp.where(first, yd[0], yd[1])
                      + jnp.where(first, ea[0], ea[1]) * y_off[:, lsl])
            y_s[:, psl] = y_pair
            state[g, :, lsl] = (st_g[:, lsl] * jnp.where(first, cd[0], cd[1])
                                + jnp.where(first, ns[0], ns[1]))

    zf = z_ref[...].astype(F32)
    y = (y_s[...] + xs * dvec_ref[...]) * (zf * _sigmoid(zf))
    y_ref[...] = _rms(y, nw_ref[...]).astype(BF16)


def _ssd(xbc, z, dt, cw, cb, dtb, alog, dvec, nw, bsz, seq):
    t = SSD_CHUNK
    nj = seq // t
    dx = xbc.shape[1]
    row = lambda w: pl.BlockSpec((t, w), lambda b, j: (b * nj + j, 0))
    causal = jnp.arange(t)[:, None] >= jnp.arange(t)[None, :]
    return pl.pallas_call(
        _ssd_kernel,
        grid=(bsz, nj),
        in_specs=[row(dx), row(D_SSD), row(LANES), _const_spec(cw.shape), _const_spec((1, dx)),
                  _const_spec((1, LANES)), _const_spec((1, LANES)),
                  _const_spec((1, D_SSD)), _const_spec((1, D_SSD)),
                  _const_spec((t, t)), _const_spec((t, t))],
        out_specs=row(D_SSD),
        out_shape=jax.ShapeDtypeStruct((bsz * seq, D_SSD), BF16),
        scratch_shapes=[pltpu.VMEM((t + SUBLANES, dx), F32),
                        pltpu.VMEM((SSD_GROUPS, SSD_STATE, D_SSD // SSD_GROUPS), F32),
                        pltpu.VMEM((t, D_SSD), F32)],
        compiler_params=_cparams(("arbitrary", "arbitrary")),
        name="ssd",
    )(xbc, z, dt, cw, cb, dtb, alog, dvec, nw, causal.astype(BF16),
      jnp.where(causal, 0.0, NEG_BIG).astype(F32))


ROUTE_E, ROUTE_W, ROUTE_R = 0, 2, 4
NEG_BIG = -1e30


def _first_argmax(vals, lane_f):
    m = jnp.max(vals, axis=-1, keepdims=True)
    idx = jnp.min(jnp.where(vals == m, lane_f, float(LANES)), axis=-1, keepdims=True)
    return m, idx


def _outproj_kernel(yl_ref, ys_ref, x_ref, wo_ref, nw_ref, wrh_ref, wrl_ref, rb_ref,
                    x1_ref, hp_ref, route_ref, cnt_ref, stril, running):
    tm = x_ref.shape[0]

    @pl.when(pl.program_id(0) == 0)
    def _():
        ri = lax.broadcasted_iota(jnp.int32, (tm, tm), 0)
        ci = lax.broadcasted_iota(jnp.int32, (tm, tm), 1)
        stril[...] = (ri > ci).astype(BF16)
        running[...] = jnp.zeros_like(running)

    x1 = (x_ref[...]
          + jnp.dot(yl_ref[...], wo_ref[0:D_LRU, :], preferred_element_type=F32)
          + jnp.dot(ys_ref[...], wo_ref[D_LRU:, :], preferred_element_type=F32))
    x1_ref[...] = x1
    h = _rms(x1, nw_ref[...])
    h_hi = h.astype(BF16)
    h_rt = h_hi.astype(F32)
    bits = lax.bitcast_convert_type(h_rt, U32)
    hp_ref[...] = bits[:, D_HALF:] | (bits[:, :D_HALF] >> 16)
    h_lo = (h - h_rt).astype(BF16)
    w_hi = wrh_ref[...]
    logits = (jnp.dot(h_hi, w_hi, preferred_element_type=F32)
              + jnp.dot(h_lo, w_hi, preferred_element_type=F32)
              + jnp.dot(h_hi, wrl_ref[...], preferred_element_type=F32)) + rb_ref[...]

    lane = lax.broadcasted_iota(jnp.int32, (tm, LANES), 1)
    lane_f = lane.astype(F32)
    is_c = lane < MOE_GROUPS
    lc = jnp.where(is_c, logits, NEG_BIG)
    m_c, g_idx = _first_argmax(lc, lane_f)
    g_w = 1.0 / jnp.sum(jnp.where(is_c, jnp.exp(lc - m_c), 0.0), axis=-1, keepdims=True)
    lo = float(MOE_GROUPS) + float(EXPERTS_PER_GROUP) * g_idx
    is_f = (lane_f >= lo) & (lane_f < lo + float(EXPERTS_PER_GROUP))
    lf = jnp.where(is_f, logits, NEG_BIG)
    v1, i1 = _first_argmax(lf, lane_f)
    v2, i2 = _first_argmax(jnp.where(lane_f == i1, NEG_BIG, lf), lane_f)
    ex = jnp.exp(v2 - v1)
    w1 = g_w / (1.0 + ex)
    w2 = g_w * ex / (1.0 + ex)

    oh1 = (lane_f == i1).astype(F32)
    oh2 = (lane_f == i2).astype(F32)
    both = oh1 + oh2
    before = jnp.dot(stril[...], both.astype(BF16), preferred_element_type=F32) + running[...]
    r1 = jnp.sum(oh1 * before, axis=-1, keepdims=True)
    r2 = jnp.sum(oh2 * before, axis=-1, keepdims=True)
    running[...] = running[...] + jnp.sum(both, axis=0, keepdims=True)
    cnt_ref[...] = running[...]

    e1 = i1 - float(MOE_GROUPS)
    e2 = i2 - float(MOE_GROUPS)
    route = jnp.zeros((tm, LANES), F32)
    for off, (a, b) in ((ROUTE_E, (e1, e2)), (ROUTE_W, (w1, w2)), (ROUTE_R, (r1, r2))):
        route = jnp.where(lane == off, a, jnp.where(lane == off + 1, b, route))
    route_ref[...] = route


def _outproj(layer, yl, ys, x, wo, nw, wrh, wrl, rb):
    n = x.shape[0]
    tm = TM_PROJ
    row = lambda w: pl.BlockSpec((tm, w), lambda i: (i, 0))
    return pl.pallas_call(
        _outproj_kernel,
        grid=(n // tm,),
        in_specs=[row(D_LRU), row(D_SSD), row(D_MODEL),
                  pl.BlockSpec((None,) + wo.shape[1:], lambda i: (layer, 0, 0)),
                  _const_spec((1, D_MODEL)),
                  _const_spec(wrh.shape), _const_spec(wrl.shape), _const_spec((1, LANES))],
        out_specs=[row(D_MODEL), row(D_HALF), row(LANES), _const_spec((1, LANES))],
        out_shape=[jax.ShapeDtypeStruct((n, D_MODEL), F32),
                   jax.ShapeDtypeStruct((n, D_HALF), U32),
                   jax.ShapeDtypeStruct((n, LANES), F32),
                   jax.ShapeDtypeStruct((1, LANES), F32)],
        scratch_shapes=[pltpu.VMEM((tm, tm), BF16), pltpu.VMEM((1, LANES), F32)],
        compiler_params=_cparams(("arbitrary",)),
        name="outproj",
    )(yl, ys, x, wo, nw, wrh, wrl, rb)


SCHED_EXPERT, SCHED_FIRST, SCHED_NEXT, SCHED_SLOT = 0, 1, 2, 3


def _expert_kernel(sch_ref, nu_ref, rt_ref, hp_hbm, wg_hbm, wu_hbm, wd_hbm, yb_ref,
                   hp_v, xs, wg_f, wu_f, wd_f, wg_s, wu_s, wd_s, sem, wsem, *, layer):
    i = pl.program_id(0)
    e = sch_ref[SCHED_EXPERT, i]
    slot = sch_ref[SCHED_SLOT, i]
    active = i < nu_ref[0]

    def weight_copies(expert, s):
        return (pltpu.make_async_copy(wg_hbm.at[layer, expert], wg_f.at[s], wsem.at[s, 0]),
                pltpu.make_async_copy(wu_hbm.at[layer, expert], wu_f.at[s], wsem.at[s, 1]),
                pltpu.make_async_copy(wd_hbm.at[layer, expert], wd_f.at[s], wsem.at[s, 2]))

    @pl.when(i == 0)
    def _():
        for cp in weight_copies(e, slot):
            cp.start()
        cp = pltpu.make_async_copy(hp_hbm, hp_v, sem)
        cp.start()
        cp.wait()

    @pl.when(active & (sch_ref[SCHED_FIRST, i] == 1))
    def _():
        for cp in weight_copies(e, slot):
            cp.wait()
        wg_s[...] = wg_f[slot].astype(BF16)
        wu_s[...] = wu_f[slot].astype(BF16)
        wd_s[...] = wd_f[slot].astype(BF16)
        nxt = sch_ref[SCHED_NEXT, i]

        @pl.when(nxt != e)
        def _():
            for cp in weight_copies(nxt, 1 - slot):
                cp.start()

    @pl.when(active)
    def _():
        for r in range(TM_EXP):
            xs[r:r + 1, :] = hp_v[pl.ds(rt_ref[0, r], 1), :]
        packed = xs[...]
        x_lo = lax.bitcast_convert_type(packed << 16, F32).astype(BF16)
        x_hi = lax.bitcast_convert_type((packed >> 16) << 16, F32).astype(BF16)
        gate = (jnp.dot(x_lo, wg_s[0:D_HALF, :], preferred_element_type=F32)
                + jnp.dot(x_hi, wg_s[D_HALF:, :], preferred_element_type=F32))
        up = (jnp.dot(x_lo, wu_s[0:D_HALF, :], preferred_element_type=F32)
              + jnp.dot(x_hi, wu_s[D_HALF:, :], preferred_element_type=F32))
        hid = (gate * _sigmoid(gate) * up).astype(BF16)
        y = jnp.dot(hid, wd_s[...], preferred_element_type=F32)
        for c in range(ROW_TILES):
            yb_ref[pl.ds(c, TM_EXP, stride=ROW_TILES), :] = y[:, c * LANES:(c + 1) * LANES]

    @pl.when(jnp.logical_not(active))
    def _():
        yb_ref[...] = jnp.zeros_like(yb_ref)


def _experts(layer, sched, n_used, row_tok, hp, wg, wu, wd):
    n_blocks = sched.shape[1]
    tm = TM_EXP
    hbm = pl.BlockSpec(memory_space=pl.ANY)
    grid_spec = pltpu.PrefetchScalarGridSpec(
        num_scalar_prefetch=2,
        grid=(n_blocks,),
        in_specs=[pl.BlockSpec((None, 1, tm), lambda i, sch, nu: (i, 0, 0), memory_space=pltpu.SMEM),
                  hbm, hbm, hbm, hbm],
        out_specs=pl.BlockSpec((tm * ROW_TILES, LANES), lambda i, sch, nu: (i, 0)),
        scratch_shapes=[pltpu.VMEM(hp.shape, U32),
                        pltpu.VMEM((tm, D_HALF), U32),
                        pltpu.VMEM((2, D_MODEL, D_EXPERT), F32),
                        pltpu.VMEM((2, D_MODEL, D_EXPERT), F32),
                        pltpu.VMEM((2, D_EXPERT, D_MODEL), F32),
                        pltpu.VMEM((D_MODEL, D_EXPERT), BF16),
                        pltpu.VMEM((D_MODEL, D_EXPERT), BF16),
                        pltpu.VMEM((D_EXPERT, D_MODEL), BF16),
                        pltpu.SemaphoreType.DMA(()),
                        pltpu.SemaphoreType.DMA((2, 3))],
    )
    return pl.pallas_call(
        functools.partial(_expert_kernel, layer=layer),
        grid_spec=grid_spec,
        out_shape=jax.ShapeDtypeStruct((n_blocks * tm * ROW_TILES, LANES), F32),
        compiler_params=_cparams(("arbitrary",)),
        name="experts",
    )(sched, n_used, row_tok.reshape(n_blocks, 1, tm), hp, wg, wu, wd)


def _combine_kernel(dcur_ref, dnxt_ref, x_ref, w_ref, nw_ref, yb_ref, o_ref, gbuf, sems,
                    *, final_norm):
    t = T_TOK
    i = pl.program_id(0)
    n_steps = pl.num_programs(0)
    slot = i % 2

    def copy(dref, s, k, tok):
        src = pl.ds(pl.multiple_of(dref[k, tok] * ROW_TILES, ROW_TILES), ROW_TILES)
        row0 = tok * ROW_TILES
        dst = pl.ds(row0 if isinstance(tok, int) else pl.multiple_of(row0, ROW_TILES), ROW_TILES)
        return pltpu.make_async_copy(yb_ref.at[src, :], gbuf.at[s, k, dst, :], sems.at[s])

    def issue(dref, s):
        def body(tok, c):
            for k in range(TOP_K):
                copy(dref, s, k, tok).start()
            return c
        lax.fori_loop(0, t, body, 0, unroll=8)

    @pl.when(i == 0)
    def _():
        issue(dcur_ref, 0)

    @pl.when(i + 1 < n_steps)
    def _():
        for tok in range(t):
            for k in range(TOP_K):
                copy(dnxt_ref, 1 - slot, k, tok).start(priority=k)

    def wait(tok, c):
        for k in range(TOP_K):
            copy(dcur_ref, slot, k, tok).wait()
        return c

    lax.fori_loop(0, t, wait, 0, unroll=8)

    w = w_ref[...]
    w0 = w[:, 0:1]
    w1 = w[:, 1:2]
    parts = []
    for c in range(ROW_TILES):
        g0 = gbuf[slot, 0, pl.ds(c, t, stride=ROW_TILES), :]
        g1 = gbuf[slot, 1, pl.ds(c, t, stride=ROW_TILES), :]
        parts.append(x_ref[:, c * LANES:(c + 1) * LANES] + w0 * g0 + w1 * g1)
    out = jnp.concatenate(parts, axis=-1)
    if final_norm:
        out = _rms(out, nw_ref[...])
    o_ref[...] = out


def _tile_dest(dest, t):
    n = dest.shape[0]
    return dest.reshape(n // t, t, TOP_K).transpose(0, 2, 1)


def _combine(dest, x, w, nw, yb, final_norm):
    n = x.shape[0]
    t = T_TOK
    n_steps = n // t
    dest3 = _tile_dest(dest, t)
    dspec = lambda f: pl.BlockSpec((None, TOP_K, t), lambda i: (f(i), 0, 0), memory_space=pltpu.SMEM)
    return pl.pallas_call(
        functools.partial(_combine_kernel, final_norm=final_norm),
        grid=(n_steps,),
        in_specs=[dspec(lambda i: i), dspec(lambda i: jnp.minimum(i + 1, n_steps - 1)),
                  pl.BlockSpec((t, D_MODEL), lambda i: (i, 0)),
                  pl.BlockSpec((t, TOP_K), lambda i: (i, 0)),
                  _const_spec((1, D_MODEL)),
                  pl.BlockSpec(memory_space=pl.ANY)],
        out_specs=pl.BlockSpec((t, D_MODEL), lambda i: (i, 0)),
        out_shape=jax.ShapeDtypeStruct((n, D_MODEL), F32),
        scratch_shapes=[pltpu.VMEM((2, TOP_K, t * ROW_TILES, LANES), F32),
                        pltpu.SemaphoreType.DMA((2,))],
        compiler_params=_cparams(("arbitrary",)),
        name="combine",
    )(dest3, dest3, x, w, nw, yb)


def _route(route, cnt, n_blocks):
    n = route.shape[0]
    expert = route[:, ROUTE_E:ROUTE_E + TOP_K].astype(jnp.int32)
    weight = route[:, ROUTE_W:ROUTE_W + TOP_K]
    rank = route[:, ROUTE_R:ROUTE_R + TOP_K].astype(jnp.int32)
    counts = cnt[0, MOE_GROUPS:MOE_GROUPS + N_EXPERTS].astype(jnp.int32)
    padded = (counts + TM_EXP - 1) // TM_EXP * TM_EXP
    pad_end = jnp.cumsum(padded)
    pad_start = pad_end - padded
    onehot = expert[:, :, None] == jnp.arange(N_EXPERTS, dtype=jnp.int32)
    dest = (jnp.sum(jnp.where(onehot, pad_start, 0), axis=-1) + rank).astype(jnp.int32)
    n_used = (pad_end[-1:] // TM_EXP).astype(jnp.int32)
    blk = jnp.arange(n_blocks, dtype=jnp.int32)
    blk_expert = jnp.minimum(jnp.sum(pad_end[None, :] <= (blk * TM_EXP)[:, None], axis=1),
                             N_EXPERTS - 1).astype(jnp.int32)
    eid = jnp.arange(N_EXPERTS, dtype=jnp.int32)
    nonempty = counts > 0
    later = (eid[None, :] > eid[:, None]) & nonempty[None, :]
    next_used = jnp.min(jnp.where(later, eid[None, :], N_EXPERTS), axis=1)
    next_used = jnp.where(next_used == N_EXPERTS, eid, next_used)
    ordinal = jnp.cumsum(nonempty.astype(jnp.int32)) - 1
    first = ((blk * TM_EXP == pad_start[blk_expert]) & (blk < n_used[0])).astype(jnp.int32)
    sched = jnp.stack([blk_expert, first, next_used[blk_expert], ordinal[blk_expert] % 2])
    token = jnp.broadcast_to(jnp.arange(n, dtype=jnp.int32)[:, None], (n, TOP_K))
    row_tok = jnp.zeros((n_blocks * TM_EXP,), jnp.int32).at[dest.reshape(-1)].add(token.reshape(-1))
    return dest, weight, sched.astype(jnp.int32), n_used, row_tok


def _gate_blocks(wa, wx):
    per = GATE_W // LRU_BW
    eye = jnp.eye(per, dtype=F32)

    def bd(w):
        w = w.reshape(LRU_HEADS // per, per, LRU_BW, LRU_BW)
        full = jnp.einsum('gpij,pq->gpiqj', w, eye)
        return full.reshape(LRU_HEADS // per, GATE_W, GATE_W)

    return jnp.concatenate([bd(wa), bd(wx)], axis=-1).astype(BF16)


def _pad_rows(w, rows):
    return jnp.pad(w, ((0, rows - w.shape[0]), (0, 0)))


def _pad_lanes(v):
    return jnp.pad(v, (0, LANES - v.shape[0])).reshape(1, LANES)


def kernel(x, norm_mix, w_in, lru_conv_w, lru_conv_b, lru_wa, lru_ba, lru_wx, lru_bx, lru_lambda, lru_norm, ssd_conv_w, ssd_conv_b, ssd_dt_bias, ssd_a_log, ssd_d, ssd_norm, w_out, norm_ffn, w_coarse, b_coarse, w_fine, b_fine, w_gate, w_up, w_down, final_norm):
    bsz, seq, d = x.shape
    n = bsz * seq
    depth = w_in.shape[0]
    n_assign = n * TOP_K
    n_blocks = -(-(n_assign + N_EXPERTS * (TM_EXP - 1)) // TM_EXP)
    o_dt = 2 * D_LRU + D_SSD + SSD_XBC
    w_in_bf = w_in.astype(BF16)
    w_out_bf = w_out.astype(BF16)

    xt = x.reshape(n, d)
    for i in range(depth):
        wdt = jnp.pad(w_in_bf[i, :, o_dt:], ((0, 0), (0, LANES - SSD_HEADS)))
        lx, lg, z, xbc, dt = _inproj(i, xt, norm_mix[i].reshape(1, d), w_in_bf, wdt)
        y_lru = _lru(lx, lg, _pad_rows(lru_conv_w[i], SUBLANES), lru_conv_b[i].reshape(1, -1),
                     _gate_blocks(lru_wa[i], lru_wx[i]), lru_ba[i].reshape(1, -1),
                     lru_bx[i].reshape(1, -1), lru_lambda[i].reshape(1, -1),
                     lru_norm[i].reshape(1, -1), bsz, seq)
        y_ssd = _ssd(xbc, z, dt, _pad_rows(ssd_conv_w[i], SUBLANES), ssd_conv_b[i].reshape(1, -1),
                     _pad_lanes(ssd_dt_bias[i]), _pad_lanes(ssd_a_log[i]),
                     jnp.repeat(ssd_d[i], SSD_HEAD_DIM).reshape(1, -1),
                     ssd_norm[i].reshape(1, -1), bsz, seq)
        w_r = jnp.concatenate([w_coarse[i], w_fine[i].transpose(1, 0, 2).reshape(d, N_EXPERTS)], axis=1)
        w_r = jnp.pad(w_r, ((0, 0), (0, LANES - w_r.shape[1])))
        w_r_hi = w_r.astype(BF16)
        w_r_lo = (w_r - w_r_hi.astype(F32)).astype(BF16)
        r_bias = _pad_lanes(jnp.concatenate([b_coarse[i], b_fine[i].reshape(-1)]))
        x1, hp, route, cnt = _outproj(i, y_lru, y_ssd, xt, w_out_bf, norm_ffn[i].reshape(1, d),
                                      w_r_hi, w_r_lo, r_bias)
        dest, weight, sched, n_used, row_tok = _route(route, cnt, n_blocks)
        yb = _experts(i, sched, n_used, row_tok, hp, w_gate, w_up, w_down)
        xt = _combine(dest, x1, weight, final_norm.reshape(1, d), yb, final_norm=(i == depth - 1))
    return xt.reshape(bsz, seq, d)
```

```python
import functools

import jax
import jax.numpy as jnp
from jax import lax
from jax.experimental import pallas as pl
from jax.experimental.pallas import tpu as pltpu

F32 = jnp.float32
BF16 = jnp.bfloat16
U32 = jnp.uint32

D_MODEL = 1024
D_LRU = 1024
LRU_HEADS = 16
LRU_BW = 64
RG_C = 8.0
CONV_K = 4
D_SSD = 1024
SSD_HEAD_DIM = 64
SSD_HEADS = 16
SSD_GROUPS = 4
SSD_STATE = 128
SSD_CHUNK = 128
MOE_GROUPS = 4
EXPERTS_PER_GROUP = 8
N_EXPERTS = 32
TOP_K = 2
D_EXPERT = 512
EPS = 1e-6

LANES = 128
SUBLANES = 8
VMEM_LIMIT = 60 * 1024 * 1024

TM_PROJ = 512
T_LRU = 256
T_SSD = 512
TM_EXP = 256
T_TOK = 256
GATE_W = 256
D_HALF = D_MODEL // 2
ROW_TILES = D_MODEL // LANES


def _cparams(sem):
    return pltpu.CompilerParams(dimension_semantics=sem, vmem_limit_bytes=VMEM_LIMIT)


def _const_spec(shape):
    n = len(shape)
    return pl.BlockSpec(shape, lambda *_: (0,) * n)


def _sigmoid(x):
    return 1.0 / (1.0 + jnp.exp(-x))


def _log1p(e):
    u = 1.0 + e
    d = u - 1.0
    return jnp.where(d == 0.0, e, jnp.log(u) * (e / jnp.where(d == 0.0, 1.0, d)))


def _softplus(x):
    return jnp.maximum(x, 0.0) + _log1p(jnp.exp(-jnp.abs(x)))


def _rms(x, w):
    ms = jnp.mean(x * x, axis=-1, keepdims=True)
    return x * lax.rsqrt(ms + EPS) * w


W_COL = 1024
SSD_XBC = D_SSD + 2 * SSD_GROUPS * SSD_STATE


def _inproj_kernel(x_ref, nw_ref, wlx_ref, wlg_ref, wz_ref, wx_ref, wbc_ref, wdt_ref,
                   lx_ref, lg_ref, z_ref, xbc_ref, dt_ref):
    h = _rms(x_ref[...], nw_ref[...]).astype(BF16)
    lx_ref[...] = jnp.dot(h, wlx_ref[...], preferred_element_type=F32).astype(BF16)
    lg_ref[...] = jnp.dot(h, wlg_ref[...], preferred_element_type=F32).astype(BF16)
    z_ref[...] = jnp.dot(h, wz_ref[...], preferred_element_type=F32).astype(BF16)
    xbc_ref[:, 0:W_COL] = jnp.dot(h, wx_ref[...], preferred_element_type=F32).astype(BF16)
    xbc_ref[:, W_COL:] = jnp.dot(h, wbc_ref[...], preferred_element_type=F32).astype(BF16)
    dt_ref[...] = jnp.dot(h, wdt_ref[...], preferred_element_type=F32)


def _inproj(layer, x, nw, w_in, wdt):
    n = x.shape[0]
    tm = TM_PROJ
    row = lambda w: pl.BlockSpec((tm, w), lambda i: (i, 0))
    wcol = lambda c: pl.BlockSpec((None, D_MODEL, W_COL), lambda i: (layer, 0, c))
    return pl.pallas_call(
        _inproj_kernel,
        grid=(n // tm,),
        in_specs=[row(D_MODEL), _const_spec((1, D_MODEL)),
                  wcol(0), wcol(1), wcol(2), wcol(3), wcol(4), _const_spec(wdt.shape)],
        out_specs=[row(D_LRU), row(D_LRU), row(D_SSD), row(SSD_XBC), row(LANES)],
        out_shape=[jax.ShapeDtypeStruct((n, D_LRU), BF16),
                   jax.ShapeDtypeStruct((n, D_LRU), BF16),
                   jax.ShapeDtypeStruct((n, D_SSD), BF16),
                   jax.ShapeDtypeStruct((n, SSD_XBC), BF16),
                   jax.ShapeDtypeStruct((n, LANES), F32)],
        compiler_params=_cparams(("arbitrary",)),
        name="inproj",
    )(x, nw, w_in, w_in, w_in, w_in, w_in, wdt)


def _causal_conv(xbuf, cw_ref, cb_ref, t):
    cw = cw_ref[...]
    n = t + SUBLANES
    full = xbuf[...]
    acc = cb_ref[...] + cw[CONV_K - 1:CONV_K, :] * full[SUBLANES:, :]
    for k in range(CONV_K - 1):
        shifted = pltpu.roll(full, n - (SUBLANES - 3 + k), 0)[0:t, :]
        acc = acc + cw[k:k + 1, :] * shifted
    return acc


def _lru_kernel(lx_ref, lg_ref, cw_ref, cb_ref, wg_ref, ba_ref, bx_ref, lam_ref, nw_ref,
                y_ref, xbuf, a_s, v_s, h_s, hcarry):
    t = T_LRU
    j = pl.program_id(1)

    @pl.when(j == 0)
    def _():
        xbuf[0:SUBLANES, :] = jnp.zeros((SUBLANES, D_LRU), F32)
        hcarry[...] = jnp.zeros_like(hcarry)

    xbuf[SUBLANES:SUBLANES + t, :] = lx_ref[...].astype(F32)
    u = _causal_conv(xbuf, cw_ref, cb_ref, t)
    xbuf[0:SUBLANES, :] = xbuf[t:t + SUBLANES, :]

    lam = lam_ref[...]
    log_sig = jnp.minimum(lam, 0.0) - _log1p(jnp.exp(-jnp.abs(lam)))
    for g in range(D_LRU // GATE_W):
        sl = slice(g * GATE_W, (g + 1) * GATE_W)
        ug = u[:, sl]
        gates = jnp.dot(ug.astype(BF16), wg_ref[g], preferred_element_type=F32)
        r = _sigmoid(gates[:, :GATE_W] + ba_ref[:, sl])
        i = _sigmoid(gates[:, GATE_W:] + bx_ref[:, sl])
        log_a = RG_C * r * log_sig[:, sl]
        a = jnp.exp(log_a)
        mult = jnp.sqrt(1.0 - a * a)
        a_s[:, sl] = a
        v_s[:, sl] = mult * (i * ug)

    row = lax.broadcasted_iota(jnp.int32, (SUBLANES, D_LRU), 0)

    def scan_body(k, hprev):
        r0 = pl.multiple_of(k * SUBLANES, SUBLANES)
        a = a_s[pl.ds(r0, SUBLANES), :]
        v = v_s[pl.ds(r0, SUBLANES), :]
        for s in (1, 2, 4):
            keep = row >= s
            a_sh = jnp.where(keep, pltpu.roll(a, s, 0), 1.0)
            v_sh = jnp.where(keep, pltpu.roll(v, s, 0), 0.0)
            v = v + a * v_sh
            a = a * a_sh
        h = v + a * hprev
        h_s[pl.ds(r0, SUBLANES), :] = h
        return h[SUBLANES - 1:SUBLANES, :]

    hcarry[...] = lax.fori_loop(0, t // SUBLANES, scan_body, hcarry[...], unroll=4)

    g_in = lg_ref[...].astype(F32)
    gelu = 0.5 * g_in * (1.0 + jnp.tanh(0.7978845608028654 * (g_in + 0.044715 * g_in * g_in * g_in)))
    y_ref[...] = _rms(gelu * h_s[...], nw_ref[...]).astype(BF16)


def _lru(lx, lg, cw, cb, wg, ba, bx, lam, nw, bsz, seq):
    t = T_LRU
    nj = seq // t
    row = pl.BlockSpec((t, D_LRU), lambda b, j: (b * nj + j, 0))
    vec = _const_spec((1, D_LRU))
    return pl.pallas_call(
        _lru_kernel,
        grid=(bsz, nj),
        in_specs=[row, row, _const_spec(cw.shape), vec, _const_spec(wg.shape),
                  vec, vec, vec, vec],
        out_specs=row,
        out_shape=jax.ShapeDtypeStruct((bsz * seq, D_LRU), BF16),
        scratch_shapes=[pltpu.VMEM((t + SUBLANES, D_LRU), F32),
                        pltpu.VMEM((t, D_LRU), F32),
                        pltpu.VMEM((t, D_LRU), F32),
                        pltpu.VMEM((t, D_LRU), F32),
                        pltpu.VMEM((1, D_LRU), F32)],
        compiler_params=_cparams(("arbitrary", "arbitrary")),
        name="rglru",
    )(lx, lg, cw, cb, wg, ba, bx, lam, nw)


def _split3(x):
    hi = x.astype(BF16)
    r1 = x - hi.astype(F32)
    mid = r1.astype(BF16)
    lo = (r1 - mid.astype(F32)).astype(BF16)
    return hi, mid, lo


def _ssd_kernel(xbc_ref, z_ref, dt_ref, cw_ref, cb_ref, dtb_ref, alog_ref, dvec_ref, nw_ref,
                tril_ref, fut_ref, y_ref, xbuf, xc_s, state, y_s):
    tt = xbc_ref.shape[0]
    j = pl.program_id(1)

    @pl.when(j == 0)
    def _():
        xbuf[0:SUBLANES, :] = jnp.zeros((SUBLANES, xbuf.shape[1]), F32)
        state[...] = jnp.zeros_like(state)

    xbuf[SUBLANES:SUBLANES + tt, :] = xbc_ref[...].astype(F32)
    u = _causal_conv(xbuf, cw_ref, cb_ref, tt)
    xbuf[0:SUBLANES, :] = xbuf[tt:tt + SUBLANES, :]
    xc_s[...] = u * _sigmoid(u)
    for c in range(tt // SSD_CHUNK):
        _ssd_chunk(c, xc_s, dt_ref, dtb_ref, alog_ref, tril_ref, fut_ref, state, y_s)

    zf = z_ref[...].astype(F32)
    y = (y_s[...] + xc_s[:, 0:D_SSD] * dvec_ref[...]) * (zf * _sigmoid(zf))
    y_ref[...] = _rms(y, nw_ref[...]).astype(BF16)


def _ssd_chunk(c, xc_s, dt_ref, dtb_ref, alog_ref, tril_ref, fut_ref, state, y_s):
    t = SSD_CHUNK
    rows = slice(c * t, (c + 1) * t)
    gn = SSD_GROUPS * SSD_STATE
    xs = xc_s[rows, 0:D_SSD]
    bm = xc_s[rows, D_SSD:D_SSD + gn]
    cm = xc_s[rows, D_SSD + gn:]

    dt = _softplus(dt_ref[rows, :] + dtb_ref[...])
    d_a = dt * (-jnp.exp(alog_ref[...]))
    tril = tril_ref[...]
    future = fut_ref[...]
    hi, mid, lo = _split3(d_a)
    a_cs = (jnp.dot(tril, hi, preferred_element_type=F32)
            + jnp.dot(tril, mid, preferred_element_type=F32)
            + jnp.dot(tril, lo, preferred_element_type=F32))
    a_cs_t = a_cs.T
    dt_t = dt.T
    a_last_t = a_cs_t[:, t - 1:t]
    w_state_t = jnp.exp(a_last_t - a_cs_t) * dt_t
    chunk_decay_t = jnp.exp(a_last_t)
    src_t = a_cs_t - jnp.log(dt_t)

    lane = lax.broadcasted_iota(jnp.int32, (1, LANES), 1)
    first = lane < SSD_HEAD_DIM

    heads_per_group = SSD_HEADS // SSD_GROUPS
    for g in range(SSD_GROUPS):
        gsl = slice(g * SSD_STATE, (g + 1) * SSD_STATE)
        c_g = cm[:, gsl].astype(BF16)
        b_g = bm[:, gsl]
        scores = lax.dot_general(c_g, b_g.astype(BF16), (((1,), (1,)), ((), ())),
                                 preferred_element_type=F32)
        b_t = b_g.T
        st_g = state[g]
        y_off = jnp.dot(c_g, st_g.astype(BF16), preferred_element_type=F32)
        for q in range(heads_per_group // 2):
            h0 = g * heads_per_group + 2 * q
            psl = slice(h0 * SSD_HEAD_DIM, (h0 + 2) * SSD_HEAD_DIM)
            lsl = slice(2 * q * SSD_HEAD_DIM, (2 * q + 2) * SSD_HEAD_DIM)
            x_pair = xs[:, psl].astype(BF16)
            yd, ns, cols, cd = [], [], [], []
            for h in (h0, h0 + 1):
                col = jnp.broadcast_to(a_cs[:, h:h + 1], (t, t))
                lmat_dt = jnp.exp(col - src_t[h:h + 1, :] + future)
                m = (scores * lmat_dt).astype(BF16)
                yd.append(jnp.dot(m, x_pair, preferred_element_type=F32))
                bw = (b_t * w_state_t[h:h + 1, :]).astype(BF16)
                ns.append(jnp.dot(bw, x_pair, preferred_element_type=F32))
                cols.append(col)
                cd.append(jnp.broadcast_to(chunk_decay_t[h:h + 1, :], (1, LANES)))
            y_pair = (jnp.where(first, yd[0], yd[1])
                      + jnp.exp(jnp.where(first, cols[0], cols[1])) * y_off[:, lsl])
            y_s[rows, psl] = y_pair
            state[g, :, lsl] = (st_g[:, lsl] * jnp.where(first, cd[0], cd[1])
                                + jnp.where(first, ns[0], ns[1]))


def _ssd(xbc, z, dt, cw, cb, dtb, alog, dvec, nw, bsz, seq):
    t = T_SSD
    nj = seq // t
    dx = xbc.shape[1]
    row = lambda w: pl.BlockSpec((t, w), lambda b, j: (b * nj + j, 0))
    ch = SSD_CHUNK
    causal = jnp.arange(ch)[:, None] >= jnp.arange(ch)[None, :]
    return pl.pallas_call(
        _ssd_kernel,
        grid=(bsz, nj),
        in_specs=[row(dx), row(D_SSD), row(LANES), _const_spec(cw.shape), _const_spec((1, dx)),
                  _const_spec((1, LANES)), _const_spec((1, LANES)),
                  _const_spec((1, D_SSD)), _const_spec((1, D_SSD)),
                  _const_spec((ch, ch)), _const_spec((ch, ch))],
        out_specs=row(D_SSD),
        out_shape=jax.ShapeDtypeStruct((bsz * seq, D_SSD), BF16),
        scratch_shapes=[pltpu.VMEM((t + SUBLANES, dx), F32),
                        pltpu.VMEM((t, dx), F32),
                        pltpu.VMEM((SSD_GROUPS, SSD_STATE, D_SSD // SSD_GROUPS), F32),
                        pltpu.VMEM((t, D_SSD), F32)],
        compiler_params=_cparams(("arbitrary", "arbitrary")),
        name="ssd",
    )(xbc, z, dt, cw, cb, dtb, alog, dvec, nw, causal.astype(BF16),
      jnp.where(causal, 0.0, NEG_BIG).astype(F32))


ROUTE_E, ROUTE_W, ROUTE_R = 0, 2, 4
NEG_BIG = -1e30


def _first_argmax(vals, lane_f):
    m = jnp.max(vals, axis=-1, keepdims=True)
    idx = jnp.min(jnp.where(vals == m, lane_f, float(LANES)), axis=-1, keepdims=True)
    return m, idx


def _outproj_kernel(yl_ref, ys_ref, x_ref, wo_ref, nw_ref, wrh_ref, wrl_ref, rb_ref,
                    x1_ref, hp_ref, route_ref, cnt_ref, stril, running):
    tm = x_ref.shape[0]

    @pl.when(pl.program_id(0) == 0)
    def _():
        ri = lax.broadcasted_iota(jnp.int32, (tm, tm), 0)
        ci = lax.broadcasted_iota(jnp.int32, (tm, tm), 1)
        stril[...] = (ri > ci).astype(BF16)
        running[...] = jnp.zeros_like(running)

    x1 = (x_ref[...]
          + jnp.dot(yl_ref[...], wo_ref[0:D_LRU, :], preferred_element_type=F32)
          + jnp.dot(ys_ref[...], wo_ref[D_LRU:, :], preferred_element_type=F32))
    x1_ref[...] = x1
    h = _rms(x1, nw_ref[...])
    h_hi = h.astype(BF16)
    h_rt = h_hi.astype(F32)
    bits = lax.bitcast_convert_type(h_rt, U32)
    hp_ref[...] = bits[:, D_HALF:] | (bits[:, :D_HALF] >> 16)
    h_lo = (h - h_rt).astype(BF16)
    w_hi = wrh_ref[...]
    logits = (jnp.dot(h_hi, w_hi, preferred_element_type=F32)
              + jnp.dot(h_lo, w_hi, preferred_element_type=F32)
              + jnp.dot(h_hi, wrl_ref[...], preferred_element_type=F32)) + rb_ref[...]

    lane = lax.broadcasted_iota(jnp.int32, (tm, LANES), 1)
    lane_f = lane.astype(F32)
    is_c = lane < MOE_GROUPS
    lc = jnp.where(is_c, logits, NEG_BIG)
    m_c, g_idx = _first_argmax(lc, lane_f)
    g_w = 1.0 / jnp.sum(jnp.where(is_c, jnp.exp(lc - m_c), 0.0), axis=-1, keepdims=True)
    lo = float(MOE_GROUPS) + float(EXPERTS_PER_GROUP) * g_idx
    is_f = (lane_f >= lo) & (lane_f < lo + float(EXPERTS_PER_GROUP))
    lf = jnp.where(is_f, logits, NEG_BIG)
    v1, i1 = _first_argmax(lf, lane_f)
    v2, i2 = _first_argmax(jnp.where(lane_f == i1, NEG_BIG, lf), lane_f)
    ex = jnp.exp(v2 - v1)
    w1 = g_w / (1.0 + ex)
    w2 = g_w * ex / (1.0 + ex)

    oh1 = (lane_f == i1).astype(F32)
    oh2 = (lane_f == i2).astype(F32)
    both = oh1 + oh2
    before = jnp.dot(stril[...], both.astype(BF16), preferred_element_type=F32) + running[...]
    r1 = jnp.sum(oh1 * before, axis=-1, keepdims=True)
    r2 = jnp.sum(oh2 * before, axis=-1, keepdims=True)
    running[...] = running[...] + jnp.sum(both, axis=0, keepdims=True)
    cnt_ref[...] = running[...]

    e1 = i1 - float(MOE_GROUPS)
    e2 = i2 - float(MOE_GROUPS)
    route = jnp.zeros((tm, LANES), F32)
    for off, (a, b) in ((ROUTE_E, (e1, e2)), (ROUTE_W, (w1, w2)), (ROUTE_R, (r1, r2))):
        route = jnp.where(lane == off, a, jnp.where(lane == off + 1, b, route))
    route_ref[...] = route


def _outproj(layer, yl, ys, x, wo, nw, wrh, wrl, rb):
    n = x.shape[0]
    tm = TM_PROJ
    row = lambda w: pl.BlockSpec((tm, w), lambda i: (i, 0))
    return pl.pallas_call(
        _outproj_kernel,
        grid=(n // tm,),
        in_specs=[row(D_LRU), row(D_SSD), row(D_MODEL),
                  pl.BlockSpec((None,) + wo.shape[1:], lambda i: (layer, 0, 0)),
                  _const_spec((1, D_MODEL)),
                  _const_spec(wrh.shape), _const_spec(wrl.shape), _const_spec((1, LANES))],
        out_specs=[row(D_MODEL), row(D_HALF), row(LANES), _const_spec((1, LANES))],
        out_shape=[jax.ShapeDtypeStruct((n, D_MODEL), F32),
                   jax.ShapeDtypeStruct((n, D_HALF), U32),
                   jax.ShapeDtypeStruct((n, LANES), F32),
                   jax.ShapeDtypeStruct((1, LANES), F32)],
        scratch_shapes=[pltpu.VMEM((tm, tm), BF16), pltpu.VMEM((1, LANES), F32)],
        compiler_params=_cparams(("arbitrary",)),
        name="outproj",
    )(yl, ys, x, wo, nw, wrh, wrl, rb)


SCHED_EXPERT, SCHED_FIRST, SCHED_NEXT, SCHED_SLOT = 0, 1, 2, 3


def _expert_kernel(sch_ref, nu_ref, rt_ref, rtn_ref, hp_hbm, wg_hbm, wu_hbm, wd_hbm, yb_ref,
                   hp_v, xs0, xs1, wg_f, wu_f, wd_f, wg_s, wu_s, wd_s, sem, wsem, *, layer):
    j = pl.program_id(0)
    out_rows = TM_EXP * ROW_TILES

    def weight_copies(expert, s):
        return (pltpu.make_async_copy(wg_hbm.at[layer, expert], wg_f.at[s], wsem.at[s, 0]),
                pltpu.make_async_copy(wu_hbm.at[layer, expert], wu_f.at[s], wsem.at[s, 1]),
                pltpu.make_async_copy(wd_hbm.at[layer, expert], wd_f.at[s], wsem.at[s, 2]))

    def gather(idx_ref, row, dst):
        for r in range(TM_EXP):
            dst[r:r + 1, :] = hp_v[pl.ds(idx_ref[row, r], 1), :]

    @pl.when(j == 0)
    def _():
        for cp in weight_copies(sch_ref[SCHED_EXPERT, 0], sch_ref[SCHED_SLOT, 0]):
            cp.start()
        cp = pltpu.make_async_copy(hp_hbm, hp_v, sem)
        cp.start()
        cp.wait()
        gather(rt_ref, 0, xs0)

    def block(half, src, prefetch):
        i = 2 * j + half
        e = sch_ref[SCHED_EXPERT, i]
        slot = sch_ref[SCHED_SLOT, i]
        active = i < nu_ref[0]
        out = pl.ds(half * out_rows, out_rows)

        @pl.when(active & (sch_ref[SCHED_FIRST, i] == 1))
        def _():
            for cp in weight_copies(e, slot):
                cp.wait()
            wg_s[...] = wg_f[slot].astype(BF16)
            wu_s[...] = wu_f[slot].astype(BF16)
            wd_s[...] = wd_f[slot].astype(BF16)
            nxt = sch_ref[SCHED_NEXT, i]

            @pl.when(nxt != e)
            def _():
                for cp in weight_copies(nxt, 1 - slot):
                    cp.start()

        @pl.when(active)
        def _():
            prefetch()
            packed = src[...]
            x_lo = lax.bitcast_convert_type(packed << 16, F32).astype(BF16)
            x_hi = lax.bitcast_convert_type((packed >> 16) << 16, F32).astype(BF16)
            gate = (jnp.dot(x_lo, wg_s[0:D_HALF, :], preferred_element_type=F32)
                    + jnp.dot(x_hi, wg_s[D_HALF:, :], preferred_element_type=F32))
            up = (jnp.dot(x_lo, wu_s[0:D_HALF, :], preferred_element_type=F32)
                  + jnp.dot(x_hi, wu_s[D_HALF:, :], preferred_element_type=F32))
            hid = (gate * _sigmoid(gate) * up).astype(BF16)
            y = jnp.dot(hid, wd_s[...], preferred_element_type=F32)
            for c in range(ROW_TILES):
                yb_ref[pl.ds(half * out_rows + c, TM_EXP, stride=ROW_TILES), :] = (
                    y[:, c * LANES:(c + 1) * LANES])

        @pl.when(jnp.logical_not(active))
        def _():
            yb_ref[out, :] = jnp.zeros((out_rows, LANES), F32)

    block(0, xs0, lambda: gather(rt_ref, 1, xs1))
    block(1, xs1, lambda: gather(rtn_ref, 0, xs0))


def _experts(layer, sched, n_used, row_tok, hp, wg, wu, wd):
    n_blocks = sched.shape[1]
    tm = TM_EXP
    n_steps = n_blocks // 2
    hbm = pl.BlockSpec(memory_space=pl.ANY)
    grid_spec = pltpu.PrefetchScalarGridSpec(
        num_scalar_prefetch=2,
        grid=(n_steps,),
        in_specs=[pl.BlockSpec((None, 2, tm), lambda j, sch, nu: (j, 0, 0), memory_space=pltpu.SMEM),
                  pl.BlockSpec((None, 2, tm), lambda j, sch, nu: (jnp.minimum(j + 1, n_steps - 1), 0, 0),
                               memory_space=pltpu.SMEM),
                  hbm, hbm, hbm, hbm],
        out_specs=pl.BlockSpec((2 * tm * ROW_TILES, LANES), lambda j, sch, nu: (j, 0)),
        scratch_shapes=[pltpu.VMEM(hp.shape, U32),
                        pltpu.VMEM((tm, D_HALF), U32),
                        pltpu.VMEM((tm, D_HALF), U32),
                        pltpu.VMEM((2, D_MODEL, D_EXPERT), F32),
                        pltpu.VMEM((2, D_MODEL, D_EXPERT), F32),
                        pltpu.VMEM((2, D_EXPERT, D_MODEL), F32),
                        pltpu.VMEM((D_MODEL, D_EXPERT), BF16),
                        pltpu.VMEM((D_MODEL, D_EXPERT), BF16),
                        pltpu.VMEM((D_EXPERT, D_MODEL), BF16),
                        pltpu.SemaphoreType.DMA(()),
                        pltpu.SemaphoreType.DMA((2, 3))],
    )
    return pl.pallas_call(
        functools.partial(_expert_kernel, layer=layer),
        grid_spec=grid_spec,
        out_shape=jax.ShapeDtypeStruct((n_blocks * tm * ROW_TILES, LANES), F32),
        compiler_params=_cparams(("arbitrary",)),
        name="experts",
    )(sched, n_used, row_tok.reshape(n_steps, 2, tm), row_tok.reshape(n_steps, 2, tm), hp, wg, wu, wd)


def _combine_kernel(dcur_ref, dnxt_ref, x_ref, w_ref, nw_ref, yb_ref, o_ref, gbuf, sems,
                    *, final_norm):
    t = T_TOK
    i = pl.program_id(0)
    n_steps = pl.num_programs(0)
    slot = i % 2

    def copy(dref, s, k, tok):
        src = pl.ds(pl.multiple_of(dref[k, tok] * ROW_TILES, ROW_TILES), ROW_TILES)
        row0 = tok * ROW_TILES
        dst = pl.ds(row0 if isinstance(tok, int) else pl.multiple_of(row0, ROW_TILES), ROW_TILES)
        return pltpu.make_async_copy(yb_ref.at[src, :], gbuf.at[s, k, dst, :], sems.at[s])

    def issue(dref, s):
        def body(tok, c):
            for k in range(TOP_K):
                copy(dref, s, k, tok).start()
            return c
        lax.fori_loop(0, t, body, 0, unroll=8)

    @pl.when(i == 0)
    def _():
        issue(dcur_ref, 0)

    @pl.when(i + 1 < n_steps)
    def _():
        for tok in range(t):
            for k in range(TOP_K):
                copy(dnxt_ref, 1 - slot, k, tok).start(priority=k)

    def wait(tok, c):
        for k in range(TOP_K):
            copy(dcur_ref, slot, k, tok).wait()
        return c

    lax.fori_loop(0, t, wait, 0, unroll=8)

    w = w_ref[...]
    w0 = w[:, 0:1]
    w1 = w[:, 1:2]
    parts = []
    for c in range(ROW_TILES):
        g0 = gbuf[slot, 0, pl.ds(c, t, stride=ROW_TILES), :]
        g1 = gbuf[slot, 1, pl.ds(c, t, stride=ROW_TILES), :]
        parts.append(x_ref[:, c * LANES:(c + 1) * LANES] + w0 * g0 + w1 * g1)
    out = jnp.concatenate(parts, axis=-1)
    if final_norm:
        out = _rms(out, nw_ref[...])
    o_ref[...] = out


def _tile_dest(dest, t):
    n = dest.shape[0]
    return dest.reshape(n // t, t, TOP_K).transpose(0, 2, 1)


def _combine(dest, x, w, nw, yb, final_norm):
    n = x.shape[0]
    t = T_TOK
    n_steps = n // t
    dest3 = _tile_dest(dest, t)
    dspec = lambda f: pl.BlockSpec((None, TOP_K, t), lambda i: (f(i), 0, 0), memory_space=pltpu.SMEM)
    return pl.pallas_call(
        functools.partial(_combine_kernel, final_norm=final_norm),
        grid=(n_steps,),
        in_specs=[dspec(lambda i: i), dspec(lambda i: jnp.minimum(i + 1, n_steps - 1)),
                  pl.BlockSpec((t, D_MODEL), lambda i: (i, 0)),
                  pl.BlockSpec((t, TOP_K), lambda i: (i, 0)),
                  _const_spec((1, D_MODEL)),
                  pl.BlockSpec(memory_space=pl.ANY)],
        out_specs=pl.BlockSpec((t, D_MODEL), lambda i: (i, 0)),
        out_shape=jax.ShapeDtypeStruct((n, D_MODEL), F32),
        scratch_shapes=[pltpu.VMEM((2, TOP_K, t * ROW_TILES, LANES), F32),
                        pltpu.SemaphoreType.DMA((2,))],
        compiler_params=_cparams(("arbitrary",)),
        name="combine",
    )(dest3, dest3, x, w, nw, yb)


def _route(route, cnt, n_blocks):
    n = route.shape[0]
    expert = route[:, ROUTE_E:ROUTE_E + TOP_K].astype(jnp.int32)
    weight = route[:, ROUTE_W:ROUTE_W + TOP_K]
    rank = route[:, ROUTE_R:ROUTE_R + TOP_K].astype(jnp.int32)
    counts = cnt[0, MOE_GROUPS:MOE_GROUPS + N_EXPERTS].astype(jnp.int32)
    padded = (counts + TM_EXP - 1) // TM_EXP * TM_EXP
    pad_end = jnp.cumsum(padded)
    pad_start = pad_end - padded
    onehot = expert[:, :, None] == jnp.arange(N_EXPERTS, dtype=jnp.int32)
    dest = (jnp.sum(jnp.where(onehot, pad_start, 0), axis=-1) + rank).astype(jnp.int32)
    n_used = (pad_end[-1:] // TM_EXP).astype(jnp.int32)
    blk = jnp.arange(n_blocks, dtype=jnp.int32)
    blk_expert = jnp.minimum(jnp.sum(pad_end[None, :] <= (blk * TM_EXP)[:, None], axis=1),
                             N_EXPERTS - 1).astype(jnp.int32)
    eid = jnp.arange(N_EXPERTS, dtype=jnp.int32)
    nonempty = counts > 0
    later = (eid[None, :] > eid[:, None]) & nonempty[None, :]
    next_used = jnp.min(jnp.where(later, eid[None, :], N_EXPERTS), axis=1)
    next_used = jnp.where(next_used == N_EXPERTS, eid, next_used)
    ordinal = jnp.cumsum(nonempty.astype(jnp.int32)) - 1
    first = ((blk * TM_EXP == pad_start[blk_expert]) & (blk < n_used[0])).astype(jnp.int32)
    sched = jnp.stack([blk_expert, first, next_used[blk_expert], ordinal[blk_expert] % 2])
    token = jnp.broadcast_to(jnp.arange(n, dtype=jnp.int32)[:, None], (n, TOP_K))
    row_tok = jnp.zeros((n_blocks * TM_EXP,), jnp.int32).at[dest.reshape(-1)].add(token.reshape(-1))
    return dest, weight, sched.astype(jnp.int32), n_used, row_tok


def _gate_blocks(wa, wx):
    per = GATE_W // LRU_BW
    eye = jnp.eye(per, dtype=F32)

    def bd(w):
        w = w.reshape(LRU_HEADS // per, per, LRU_BW, LRU_BW)
        full = jnp.einsum('gpij,pq->gpiqj', w, eye)
        return full.reshape(LRU_HEADS // per, GATE_W, GATE_W)

    return jnp.concatenate([bd(wa), bd(wx)], axis=-1).astype(BF16)


def _pad_rows(w, rows):
    return jnp.pad(w, ((0, rows - w.shape[0]), (0, 0)))


def _pad_lanes(v):
    return jnp.pad(v, (0, LANES - v.shape[0])).reshape(1, LANES)


def kernel(x, norm_mix, w_in, lru_conv_w, lru_conv_b, lru_wa, lru_ba, lru_wx, lru_bx, lru_lambda, lru_norm, ssd_conv_w, ssd_conv_b, ssd_dt_bias, ssd_a_log, ssd_d, ssd_norm, w_out, norm_ffn, w_coarse, b_coarse, w_fine, b_fine, w_gate, w_up, w_down, final_norm):
    bsz, seq, d = x.shape
    n = bsz * seq
    depth = w_in.shape[0]
    n_assign = n * TOP_K
    n_blocks = -(-(n_assign + N_EXPERTS * (TM_EXP - 1)) // TM_EXP)
    n_blocks += n_blocks % 2
    o_dt = 2 * D_LRU + D_SSD + SSD_XBC
    w_in_bf = w_in.astype(BF16)
    w_out_bf = w_out.astype(BF16)

    xt = x.reshape(n, d)
    for i in range(depth):
        wdt = jnp.pad(w_in_bf[i, :, o_dt:], ((0, 0), (0, LANES - SSD_HEADS)))
        lx, lg, z, xbc, dt = _inproj(i, xt, norm_mix[i].reshape(1, d), w_in_bf, wdt)
        y_lru = _lru(lx, lg, _pad_rows(lru_conv_w[i], SUBLANES), lru_conv_b[i].reshape(1, -1),
                     _gate_blocks(lru_wa[i], lru_wx[i]), lru_ba[i].reshape(1, -1),
                     lru_bx[i].reshape(1, -1), lru_lambda[i].reshape(1, -1),
                     lru_norm[i].reshape(1, -1), bsz, seq)
        y_ssd = _ssd(xbc, z, dt, _pad_rows(ssd_conv_w[i], SUBLANES), ssd_conv_b[i].reshape(1, -1),
                     _pad_lanes(ssd_dt_bias[i]), _pad_lanes(ssd_a_log[i]),
                     jnp.repeat(ssd_d[i], SSD_HEAD_DIM).reshape(1, -1),
                     ssd_norm[i].reshape(1, -1), bsz, seq)
        w_r = jnp.concatenate([w_coarse[i], w_fine[i].transpose(1, 0, 2).reshape(d, N_EXPERTS)], axis=1)
        w_r = jnp.pad(w_r, ((0, 0), (0, LANES - w_r.shape[1])))
        w_r_hi = w_r.astype(BF16)
        w_r_lo = (w_r - w_r_hi.astype(F32)).astype(BF16)
        r_bias = _pad_lanes(jnp.concatenate([b_coarse[i], b_fine[i].reshape(-1)]))
        x1, hp, route, cnt = _outproj(i, y_lru, y_ssd, xt, w_out_bf, norm_ffn[i].reshape(1, d),
                                      w_r_hi, w_r_lo, r_bias)
        dest, weight, sched, n_used, row_tok = _route(route, cnt, n_blocks)
        yb = _experts(i, sched, n_used, row_tok, hp, w_gate, w_up, w_down)
        xt = _combine(dest, x1, weight, final_norm.reshape(1, d), yb, final_norm=(i == depth - 1))
    return xt.reshape(bsz, seq, d)
```

```python
import functools

import jax
import jax.numpy as jnp
from jax import lax
from jax.experimental import pallas as pl
from jax.experimental.pallas import tpu as pltpu

F32 = jnp.float32
BF16 = jnp.bfloat16
U32 = jnp.uint32

D_MODEL = 1024
D_LRU = 1024
LRU_HEADS = 16
LRU_BW = 64
RG_C = 8.0
CONV_K = 4
D_SSD = 1024
SSD_HEAD_DIM = 64
SSD_HEADS = 16
SSD_GROUPS = 4
SSD_STATE = 128
SSD_CHUNK = 128
MOE_GROUPS = 4
EXPERTS_PER_GROUP = 8
N_EXPERTS = 32
TOP_K = 2
D_EXPERT = 512
EPS = 1e-6

LANES = 128
SUBLANES = 8
VMEM_LIMIT = 60 * 1024 * 1024

TM_PROJ = 512
TM_SUB = 256
T_LRU = 256
T_SSD = 512
TM_EXP = 256
T_TOK = 256
GATE_W = 256
D_HALF = D_MODEL // 2
ROW_TILES = D_MODEL // LANES


def _cparams(sem):
    return pltpu.CompilerParams(dimension_semantics=sem, vmem_limit_bytes=VMEM_LIMIT)


def _const_spec(shape):
    n = len(shape)
    return pl.BlockSpec(shape, lambda *_: (0,) * n)


def _sigmoid(x):
    return 1.0 / (1.0 + jnp.exp(-x))


def _log1p(e):
    u = 1.0 + e
    d = u - 1.0
    return jnp.where(d == 0.0, e, jnp.log(u) * (e / jnp.where(d == 0.0, 1.0, d)))


def _softplus(x):
    return jnp.maximum(x, 0.0) + _log1p(jnp.exp(-jnp.abs(x)))


def _rms(x, w):
    ms = jnp.mean(x * x, axis=-1, keepdims=True)
    return x * lax.rsqrt(ms + EPS) * w


W_COL = 1024
SSD_XBC = D_SSD + 2 * SSD_GROUPS * SSD_STATE


def _inproj_kernel(x_ref, nw_ref, wlx_ref, wlg_ref, wz_ref, wx_ref, wbc_ref, wdt_ref,
                   lx_ref, lg_ref, z_ref, xbc_ref, dt_ref):
    h = _rms(x_ref[...], nw_ref[...]).astype(BF16)
    lx_ref[...] = jnp.dot(h, wlx_ref[...], preferred_element_type=F32).astype(BF16)
    lg_ref[...] = jnp.dot(h, wlg_ref[...], preferred_element_type=F32).astype(BF16)
    z_ref[...] = jnp.dot(h, wz_ref[...], preferred_element_type=F32).astype(BF16)
    xbc_ref[:, 0:W_COL] = jnp.dot(h, wx_ref[...], preferred_element_type=F32).astype(BF16)
    xbc_ref[:, W_COL:] = jnp.dot(h, wbc_ref[...], preferred_element_type=F32).astype(BF16)
    dt_ref[...] = jnp.dot(h, wdt_ref[...], preferred_element_type=F32)


def _inproj(layer, x, nw, w_in, wdt):
    n = x.shape[0]
    tm = TM_PROJ
    row = lambda w: pl.BlockSpec((tm, w), lambda i: (i, 0))
    wcol = lambda c: pl.BlockSpec((None, D_MODEL, W_COL), lambda i: (layer, 0, c))
    return pl.pallas_call(
        _inproj_kernel,
        grid=(n // tm,),
        in_specs=[row(D_MODEL), _const_spec((1, D_MODEL)),
                  wcol(0), wcol(1), wcol(2), wcol(3), wcol(4), _const_spec(wdt.shape)],
        out_specs=[row(D_LRU), row(D_LRU), row(D_SSD), row(SSD_XBC), row(LANES)],
        out_shape=[jax.ShapeDtypeStruct((n, D_LRU), BF16),
                   jax.ShapeDtypeStruct((n, D_LRU), BF16),
                   jax.ShapeDtypeStruct((n, D_SSD), BF16),
                   jax.ShapeDtypeStruct((n, SSD_XBC), BF16),
                   jax.ShapeDtypeStruct((n, LANES), F32)],
        compiler_params=_cparams(("arbitrary",)),
        name="inproj",
    )(x, nw, w_in, w_in, w_in, w_in, w_in, wdt)


def _causal_conv(xbuf, cw_ref, cb_ref, t):
    cw = cw_ref[...]
    n = t + SUBLANES
    full = xbuf[...]
    acc = cb_ref[...] + cw[CONV_K - 1:CONV_K, :] * full[SUBLANES:, :]
    for k in range(CONV_K - 1):
        shifted = pltpu.roll(full, n - (SUBLANES - 3 + k), 0)[0:t, :]
        acc = acc + cw[k:k + 1, :] * shifted
    return acc


def _lru_kernel(lx_ref, lg_ref, cw_ref, cb_ref, wg_ref, ba_ref, bx_ref, lam_ref, nw_ref,
                y_ref, xbuf, a_s, v_s, h_s, hcarry):
    t = T_LRU
    j = pl.program_id(1)

    @pl.when(j == 0)
    def _():
        xbuf[0:SUBLANES, :] = jnp.zeros((SUBLANES, D_LRU), F32)
        hcarry[...] = jnp.zeros_like(hcarry)

    xbuf[SUBLANES:SUBLANES + t, :] = lx_ref[...].astype(F32)
    u = _causal_conv(xbuf, cw_ref, cb_ref, t)
    xbuf[0:SUBLANES, :] = xbuf[t:t + SUBLANES, :]

    lam = lam_ref[...]
    log_sig = jnp.minimum(lam, 0.0) - _log1p(jnp.exp(-jnp.abs(lam)))
    for g in range(D_LRU // GATE_W):
        sl = slice(g * GATE_W, (g + 1) * GATE_W)
        ug = u[:, sl]
        gates = jnp.dot(ug.astype(BF16), wg_ref[g], preferred_element_type=F32)
        r = _sigmoid(gates[:, :GATE_W] + ba_ref[:, sl])
        i = _sigmoid(gates[:, GATE_W:] + bx_ref[:, sl])
        log_a = RG_C * r * log_sig[:, sl]
        a = jnp.exp(log_a)
        mult = jnp.sqrt(1.0 - a * a)
        a_s[:, sl] = a
        v_s[:, sl] = mult * (i * ug)

    row = lax.broadcasted_iota(jnp.int32, (SUBLANES, D_LRU), 0)

    def scan_body(k, hprev):
        r0 = pl.multiple_of(k * SUBLANES, SUBLANES)
        a = a_s[pl.ds(r0, SUBLANES), :]
        v = v_s[pl.ds(r0, SUBLANES), :]
        for s in (1, 2, 4):
            keep = row >= s
            a_sh = jnp.where(keep, pltpu.roll(a, s, 0), 1.0)
            v_sh = jnp.where(keep, pltpu.roll(v, s, 0), 0.0)
            v = v + a * v_sh
            a = a * a_sh
        h = v + a * hprev
        h_s[pl.ds(r0, SUBLANES), :] = h
        return h[SUBLANES - 1:SUBLANES, :]

    hcarry[...] = lax.fori_loop(0, t // SUBLANES, scan_body, hcarry[...], unroll=4)

    g_in = lg_ref[...].astype(F32)
    gelu = 0.5 * g_in * (1.0 + jnp.tanh(0.7978845608028654 * (g_in + 0.044715 * g_in * g_in * g_in)))
    y_ref[...] = _rms(gelu * h_s[...], nw_ref[...]).astype(BF16)


def _lru(lx, lg, cw, cb, wg, ba, bx, lam, nw, bsz, seq):
    t = T_LRU
    nj = seq // t
    row = pl.BlockSpec((t, D_LRU), lambda b, j: (b * nj + j, 0))
    vec = _const_spec((1, D_LRU))
    return pl.pallas_call(
        _lru_kernel,
        grid=(bsz, nj),
        in_specs=[row, row, _const_spec(cw.shape), vec, _const_spec(wg.shape),
                  vec, vec, vec, vec],
        out_specs=row,
        out_shape=jax.ShapeDtypeStruct((bsz * seq, D_LRU), BF16),
        scratch_shapes=[pltpu.VMEM((t + SUBLANES, D_LRU), F32),
                        pltpu.VMEM((t, D_LRU), F32),
                        pltpu.VMEM((t, D_LRU), F32),
                        pltpu.VMEM((t, D_LRU), F32),
                        pltpu.VMEM((1, D_LRU), F32)],
        compiler_params=_cparams(("arbitrary", "arbitrary")),
        name="rglru",
    )(lx, lg, cw, cb, wg, ba, bx, lam, nw)


def _split3(x):
    hi = x.astype(BF16)
    r1 = x - hi.astype(F32)
    mid = r1.astype(BF16)
    lo = (r1 - mid.astype(F32)).astype(BF16)
    return hi, mid, lo


def _ssd_kernel(xbc_ref, z_ref, dt_ref, cw_ref, cb_ref, dtb_ref, alog_ref, dvec_ref, nw_ref,
                tril_ref, fut_ref, y_ref, xbuf, xc_s, state, y_s):
    tt = xbc_ref.shape[0]
    j = pl.program_id(1)

    @pl.when(j == 0)
    def _():
        xbuf[0:SUBLANES, :] = jnp.zeros((SUBLANES, xbuf.shape[1]), F32)
        state[...] = jnp.zeros_like(state)

    xbuf[SUBLANES:SUBLANES + tt, :] = xbc_ref[...].astype(F32)
    u = _causal_conv(xbuf, cw_ref, cb_ref, tt)
    xbuf[0:SUBLANES, :] = xbuf[tt:tt + SUBLANES, :]
    xc_s[...] = u * _sigmoid(u)
    for c in range(tt // SSD_CHUNK):
        _ssd_chunk(c, xc_s, dt_ref, dtb_ref, alog_ref, tril_ref, fut_ref, state, y_s)

    zf = z_ref[...].astype(F32)
    y = (y_s[...] + xc_s[:, 0:D_SSD] * dvec_ref[...]) * (zf * _sigmoid(zf))
    y_ref[...] = _rms(y, nw_ref[...]).astype(BF16)


def _ssd_chunk(c, xc_s, dt_ref, dtb_ref, alog_ref, tril_ref, fut_ref, state, y_s):
    t = SSD_CHUNK
    rows = slice(c * t, (c + 1) * t)
    gn = SSD_GROUPS * SSD_STATE
    xs = xc_s[rows, 0:D_SSD]
    bm = xc_s[rows, D_SSD:D_SSD + gn]
    cm = xc_s[rows, D_SSD + gn:]

    dt = _softplus(dt_ref[rows, :] + dtb_ref[...])
    d_a = dt * (-jnp.exp(alog_ref[...]))
    tril = tril_ref[...]
    future = fut_ref[...]
    hi, mid, lo = _split3(d_a)
    a_cs = (jnp.dot(tril, hi, preferred_element_type=F32)
            + jnp.dot(tril, mid, preferred_element_type=F32)
            + jnp.dot(tril, lo, preferred_element_type=F32))
    a_cs_t = a_cs.T
    dt_t = dt.T
    a_last_t = a_cs_t[:, t - 1:t]
    w_state_t = jnp.exp(a_last_t - a_cs_t) * dt_t
    chunk_decay_t = jnp.exp(a_last_t)
    src_t = a_cs_t - jnp.log(dt_t)

    lane = lax.broadcasted_iota(jnp.int32, (1, LANES), 1)
    first = lane < SSD_HEAD_DIM

    heads_per_group = SSD_HEADS // SSD_GROUPS
    for g in range(SSD_GROUPS):
        gsl = slice(g * SSD_STATE, (g + 1) * SSD_STATE)
        c_g = cm[:, gsl].astype(BF16)
        b_g = bm[:, gsl]
        scores = lax.dot_general(c_g, b_g.astype(BF16), (((1,), (1,)), ((), ())),
                                 preferred_element_type=F32)
        b_t = b_g.T
        st_g = state[g]
        y_off = jnp.dot(c_g, st_g.astype(BF16), preferred_element_type=F32)
        for q in range(heads_per_group // 2):
            h0 = g * heads_per_group + 2 * q
            psl = slice(h0 * SSD_HEAD_DIM, (h0 + 2) * SSD_HEAD_DIM)
            lsl = slice(2 * q * SSD_HEAD_DIM, (2 * q + 2) * SSD_HEAD_DIM)
            x_pair = xs[:, psl].astype(BF16)
            yd, ns, cols, cd = [], [], [], []
            for h in (h0, h0 + 1):
                col = jnp.broadcast_to(a_cs[:, h:h + 1], (t, t))
                lmat_dt = jnp.exp(col - src_t[h:h + 1, :] + future)
                m = (scores * lmat_dt).astype(BF16)
                yd.append(jnp.dot(m, x_pair, preferred_element_type=F32))
                bw = (b_t * w_state_t[h:h + 1, :]).astype(BF16)
                ns.append(jnp.dot(bw, x_pair, preferred_element_type=F32))
                cols.append(col)
                cd.append(jnp.broadcast_to(chunk_decay_t[h:h + 1, :], (1, LANES)))
            y_pair = (jnp.where(first, yd[0], yd[1])
                      + jnp.exp(jnp.where(first, cols[0], cols[1])) * y_off[:, lsl])
            y_s[rows, psl] = y_pair
            state[g, :, lsl] = (st_g[:, lsl] * jnp.where(first, cd[0], cd[1])
                                + jnp.where(first, ns[0], ns[1]))


def _ssd(xbc, z, dt, cw, cb, dtb, alog, dvec, nw, bsz, seq):
    t = T_SSD
    nj = seq // t
    dx = xbc.shape[1]
    row = lambda w: pl.BlockSpec((t, w), lambda b, j: (b * nj + j, 0))
    ch = SSD_CHUNK
    causal = jnp.arange(ch)[:, None] >= jnp.arange(ch)[None, :]
    return pl.pallas_call(
        _ssd_kernel,
        grid=(bsz, nj),
        in_specs=[row(dx), row(D_SSD), row(LANES), _const_spec(cw.shape), _const_spec((1, dx)),
                  _const_spec((1, LANES)), _const_spec((1, LANES)),
                  _const_spec((1, D_SSD)), _const_spec((1, D_SSD)),
                  _const_spec((ch, ch)), _const_spec((ch, ch))],
        out_specs=row(D_SSD),
        out_shape=jax.ShapeDtypeStruct((bsz * seq, D_SSD), BF16),
        scratch_shapes=[pltpu.VMEM((t + SUBLANES, dx), F32),
                        pltpu.VMEM((t, dx), F32),
                        pltpu.VMEM((SSD_GROUPS, SSD_STATE, D_SSD // SSD_GROUPS), F32),
                        pltpu.VMEM((t, D_SSD), F32)],
        compiler_params=_cparams(("arbitrary", "arbitrary")),
        name="ssd",
    )(xbc, z, dt, cw, cb, dtb, alog, dvec, nw, causal.astype(BF16),
      jnp.where(causal, 0.0, NEG_BIG).astype(F32))


ROUTE_E, ROUTE_W, ROUTE_R = 0, 2, 4
NEG_BIG = -1e30


def _first_argmax(vals, lane_f):
    m = jnp.max(vals, axis=-1, keepdims=True)
    idx = jnp.min(jnp.where(vals == m, lane_f, float(LANES)), axis=-1, keepdims=True)
    return m, idx


def _outproj_kernel(yl_ref, ys_ref, x_ref, wo_ref, nw_ref, wr_ref, rb_ref,
                    x1_ref, hp_ref, route_ref, cnt_ref, stril, running):
    @pl.when(pl.program_id(0) == 0)
    def _():
        ri = lax.broadcasted_iota(jnp.int32, (TM_SUB, TM_SUB), 0)
        ci = lax.broadcasted_iota(jnp.int32, (TM_SUB, TM_SUB), 1)
        stril[...] = (ri > ci).astype(BF16)
        running[...] = jnp.zeros_like(running)

    for s in range(x_ref.shape[0] // TM_SUB):
        _outproj_rows(pl.ds(s * TM_SUB, TM_SUB), yl_ref, ys_ref, x_ref, wo_ref, nw_ref, wr_ref, rb_ref,
                      x1_ref, hp_ref, route_ref, stril, running)
    cnt_ref[...] = running[...]


def _outproj_rows(rows, yl_ref, ys_ref, x_ref, wo_ref, nw_ref, wr_ref, rb_ref,
                  x1_ref, hp_ref, route_ref, stril, running):
    tm = TM_SUB
    y_cat = jnp.concatenate([yl_ref[rows, :], ys_ref[rows, :]], axis=1)
    x1 = x_ref[rows, :] + jnp.dot(y_cat, wo_ref[...], preferred_element_type=F32)
    x1_ref[rows, :] = x1
    h = _rms(x1, nw_ref[...])
    h_hi = h.astype(BF16)
    h_rt = h_hi.astype(F32)
    bits = lax.bitcast_convert_type(h_rt, U32)
    hp_ref[rows, :] = bits[:, D_HALF:] | (bits[:, :D_HALF] >> 16)
    logits = jnp.dot(h_hi, wr_ref[...], preferred_element_type=F32) + rb_ref[...]

    lane = lax.broadcasted_iota(jnp.int32, (tm, LANES), 1)
    lane_f = lane.astype(F32)
    is_c = lane < MOE_GROUPS
    lc = jnp.where(is_c, logits, NEG_BIG)
    m_c, g_idx = _first_argmax(lc, lane_f)
    g_w = 1.0 / jnp.sum(jnp.where(is_c, jnp.exp(lc - m_c), 0.0), axis=-1, keepdims=True)
    lo = float(MOE_GROUPS) + float(EXPERTS_PER_GROUP) * g_idx
    is_f = (lane_f >= lo) & (lane_f < lo + float(EXPERTS_PER_GROUP))
    lf = jnp.where(is_f, logits, NEG_BIG)
    v1, i1 = _first_argmax(lf, lane_f)
    v2, i2 = _first_argmax(jnp.where(lane_f == i1, NEG_BIG, lf), lane_f)
    ex = jnp.exp(v2 - v1)
    w1 = g_w / (1.0 + ex)
    w2 = g_w * ex / (1.0 + ex)

    oh1 = (lane_f == i1).astype(F32)
    oh2 = (lane_f == i2).astype(F32)
    both = oh1 + oh2
    before = jnp.dot(stril[...], both.astype(BF16), preferred_element_type=F32) + running[...]
    r1 = jnp.sum(oh1 * before, axis=-1, keepdims=True)
    r2 = jnp.sum(oh2 * before, axis=-1, keepdims=True)
    running[...] = running[...] + jnp.sum(both, axis=0, keepdims=True)

    e1 = i1 - float(MOE_GROUPS)
    e2 = i2 - float(MOE_GROUPS)
    route = jnp.zeros((tm, LANES), F32)
    for off, (a, b) in ((ROUTE_E, (e1, e2)), (ROUTE_W, (w1, w2)), (ROUTE_R, (r1, r2))):
        route = jnp.where(lane == off, a, jnp.where(lane == off + 1, b, route))
    route_ref[rows, :] = route


def _outproj(layer, yl, ys, x, wo, nw, wr, rb):
    n = x.shape[0]
    tm = TM_PROJ
    row = lambda w: pl.BlockSpec((tm, w), lambda i: (i, 0))
    return pl.pallas_call(
        _outproj_kernel,
        grid=(n // tm,),
        in_specs=[row(D_LRU), row(D_SSD), row(D_MODEL),
                  pl.BlockSpec((None,) + wo.shape[1:], lambda i: (layer, 0, 0)),
                  _const_spec((1, D_MODEL)),
                  _const_spec(wr.shape), _const_spec((1, LANES))],
        out_specs=[row(D_MODEL), row(D_HALF), row(LANES), _const_spec((1, LANES))],
        out_shape=[jax.ShapeDtypeStruct((n, D_MODEL), F32),
                   jax.ShapeDtypeStruct((n, D_HALF), U32),
                   jax.ShapeDtypeStruct((n, LANES), F32),
                   jax.ShapeDtypeStruct((1, LANES), F32)],
        scratch_shapes=[pltpu.VMEM((TM_SUB, TM_SUB), BF16), pltpu.VMEM((1, LANES), F32)],
        compiler_params=_cparams(("arbitrary",)),
        name="outproj",
    )(yl, ys, x, wo, nw, wr, rb)


SCHED_EXPERT, SCHED_FIRST, SCHED_NEXT, SCHED_SLOT = 0, 1, 2, 3


def _expert_kernel(sch_ref, nu_ref, rt_ref, rtn_ref, hp_hbm, wg_hbm, wu_hbm, wd_hbm, yb_ref,
                   hp_v, xs0, xs1, wg_f, wu_f, wd_f, wg_s, wu_s, wd_s, sem, wsem, *, layer):
    j = pl.program_id(0)
    out_rows = TM_EXP * ROW_TILES

    def weight_copies(expert, s):
        return (pltpu.make_async_copy(wg_hbm.at[layer, expert], wg_f.at[s], wsem.at[s, 0]),
                pltpu.make_async_copy(wu_hbm.at[layer, expert], wu_f.at[s], wsem.at[s, 1]),
                pltpu.make_async_copy(wd_hbm.at[layer, expert], wd_f.at[s], wsem.at[s, 2]))

    def gather(idx_ref, row, dst):
        for r in range(TM_EXP):
            dst[r:r + 1, :] = hp_v[pl.ds(idx_ref[row, r], 1), :]

    @pl.when(j == 0)
    def _():
        for cp in weight_copies(sch_ref[SCHED_EXPERT, 0], sch_ref[SCHED_SLOT, 0]):
            cp.start()
        cp = pltpu.make_async_copy(hp_hbm, hp_v, sem)
        cp.start()
        cp.wait()
        gather(rt_ref, 0, xs0)

    def block(half, src, prefetch):
        i = 2 * j + half
        e = sch_ref[SCHED_EXPERT, i]
        slot = sch_ref[SCHED_SLOT, i]
        active = i < nu_ref[0]
        out = pl.ds(half * out_rows, out_rows)

        @pl.when(active & (sch_ref[SCHED_FIRST, i] == 1))
        def _():
            for cp in weight_copies(e, slot):
                cp.wait()
            wg_s[...] = wg_f[slot].astype(BF16)
            wu_s[...] = wu_f[slot].astype(BF16)
            wd_s[...] = wd_f[slot].astype(BF16)
            nxt = sch_ref[SCHED_NEXT, i]

            @pl.when(nxt != e)
            def _():
                for cp in weight_copies(nxt, 1 - slot):
                    cp.start()

        @pl.when(active)
        def _():
            prefetch()
            packed = src[...]
            x_lo = lax.bitcast_convert_type(packed << 16, F32).astype(BF16)
            x_hi = lax.bitcast_convert_type((packed >> 16) << 16, F32).astype(BF16)
            gate = (jnp.dot(x_lo, wg_s[0:D_HALF, :], preferred_element_type=F32)
                    + jnp.dot(x_hi, wg_s[D_HALF:, :], preferred_element_type=F32))
            up = (jnp.dot(x_lo, wu_s[0:D_HALF, :], preferred_element_type=F32)
                  + jnp.dot(x_hi, wu_s[D_HALF:, :], preferred_element_type=F32))
            hid = (gate * _sigmoid(gate) * up).astype(BF16)
            y = jnp.dot(hid, wd_s[...], preferred_element_type=F32)
            for c in range(ROW_TILES):
                yb_ref[pl.ds(half * out_rows + c, TM_EXP, stride=ROW_TILES), :] = (
                    y[:, c * LANES:(c + 1) * LANES])

        @pl.when(jnp.logical_not(active))
        def _():
            yb_ref[out, :] = jnp.zeros((out_rows, LANES), F32)

    block(0, xs0, lambda: gather(rt_ref, 1, xs1))
    block(1, xs1, lambda: gather(rtn_ref, 0, xs0))


def _experts(layer, sched, n_used, row_tok, hp, wg, wu, wd):
    n_blocks = sched.shape[1]
    tm = TM_EXP
    n_steps = n_blocks // 2
    hbm = pl.BlockSpec(memory_space=pl.ANY)
    grid_spec = pltpu.PrefetchScalarGridSpec(
        num_scalar_prefetch=2,
        grid=(n_steps,),
        in_specs=[pl.BlockSpec((None, 2, tm), lambda j, sch, nu: (j, 0, 0), memory_space=pltpu.SMEM),
                  pl.BlockSpec((None, 2, tm), lambda j, sch, nu: (jnp.minimum(j + 1, n_steps - 1), 0, 0),
                               memory_space=pltpu.SMEM),
                  hbm, hbm, hbm, hbm],
        out_specs=pl.BlockSpec((2 * tm * ROW_TILES, LANES), lambda j, sch, nu: (j, 0)),
        scratch_shapes=[pltpu.VMEM(hp.shape, U32),
                        pltpu.VMEM((tm, D_HALF), U32),
                        pltpu.VMEM((tm, D_HALF), U32),
                        pltpu.VMEM((2, D_MODEL, D_EXPERT), F32),
                        pltpu.VMEM((2, D_MODEL, D_EXPERT), F32),
                        pltpu.VMEM((2, D_EXPERT, D_MODEL), F32),
                        pltpu.VMEM((D_MODEL, D_EXPERT), BF16),
                        pltpu.VMEM((D_MODEL, D_EXPERT), BF16),
                        pltpu.VMEM((D_EXPERT, D_MODEL), BF16),
                        pltpu.SemaphoreType.DMA(()),
                        pltpu.SemaphoreType.DMA((2, 3))],
    )
    return pl.pallas_call(
        functools.partial(_expert_kernel, layer=layer),
        grid_spec=grid_spec,
        out_shape=jax.ShapeDtypeStruct((n_blocks * tm * ROW_TILES, LANES), F32),
        compiler_params=_cparams(("arbitrary",)),
        name="experts",
    )(sched, n_used, row_tok.reshape(n_steps, 2, tm), row_tok.reshape(n_steps, 2, tm), hp, wg, wu, wd)


def _combine_kernel(dcur_ref, dnxt_ref, x_ref, w_ref, nw_ref, yb_ref, o_ref, gbuf, sems,
                    *, final_norm):
    t = T_TOK
    i = pl.program_id(0)
    n_steps = pl.num_programs(0)
    slot = i % 2

    def copy(dref, s, k, tok):
        src = pl.ds(pl.multiple_of(dref[k, tok] * ROW_TILES, ROW_TILES), ROW_TILES)
        row0 = tok * ROW_TILES
        dst = pl.ds(row0 if isinstance(tok, int) else pl.multiple_of(row0, ROW_TILES), ROW_TILES)
        return pltpu.make_async_copy(yb_ref.at[src, :], gbuf.at[s, k, dst, :], sems.at[s])

    def issue(dref, s):
        def body(tok, c):
            for k in range(TOP_K):
                copy(dref, s, k, tok).start()
            return c
        lax.fori_loop(0, t, body, 0, unroll=8)

    @pl.when(i == 0)
    def _():
        issue(dcur_ref, 0)

    @pl.when(i + 1 < n_steps)
    def _():
        for tok in range(t):
            for k in range(TOP_K):
                copy(dnxt_ref, 1 - slot, k, tok).start(priority=k)

    def wait(tok, c):
        for k in range(TOP_K):
            copy(dcur_ref, slot, k, tok).wait()
        return c

    lax.fori_loop(0, t, wait, 0, unroll=8)

    w0 = w_ref[:, ROUTE_W:ROUTE_W + 1]
    w1 = w_ref[:, ROUTE_W + 1:ROUTE_W + 2]
    parts = []
    for c in range(ROW_TILES):
        g0 = gbuf[slot, 0, pl.ds(c, t, stride=ROW_TILES), :]
        g1 = gbuf[slot, 1, pl.ds(c, t, stride=ROW_TILES), :]
        parts.append(x_ref[:, c * LANES:(c + 1) * LANES] + w0 * g0 + w1 * g1)
    out = jnp.concatenate(parts, axis=-1)
    if final_norm:
        out = _rms(out, nw_ref[...])
    o_ref[...] = out


def _combine(dest3, x, route, nw, yb, final_norm):
    n = x.shape[0]
    t = T_TOK
    n_steps = n // t
    dspec = lambda f: pl.BlockSpec((None, TOP_K, t), lambda i: (f(i), 0, 0), memory_space=pltpu.SMEM)
    return pl.pallas_call(
        functools.partial(_combine_kernel, final_norm=final_norm),
        grid=(n_steps,),
        in_specs=[dspec(lambda i: i), dspec(lambda i: jnp.minimum(i + 1, n_steps - 1)),
                  pl.BlockSpec((t, D_MODEL), lambda i: (i, 0)),
                  pl.BlockSpec((t, LANES), lambda i: (i, 0)),
                  _const_spec((1, D_MODEL)),
                  pl.BlockSpec(memory_space=pl.ANY)],
        out_specs=pl.BlockSpec((t, D_MODEL), lambda i: (i, 0)),
        out_shape=jax.ShapeDtypeStruct((n, D_MODEL), F32),
        scratch_shapes=[pltpu.VMEM((2, TOP_K, t * ROW_TILES, LANES), F32),
                        pltpu.SemaphoreType.DMA((2,))],
        compiler_params=_cparams(("arbitrary",)),
        name="combine",
    )(dest3, dest3, x, route, nw, yb)


T_SLOT = 1024


def _slots_kernel(route_ref, ps_ref, dest_ref):
    route = route_ref[...]
    lane_f = lax.broadcasted_iota(jnp.int32, route.shape, 1).astype(F32)
    ps = ps_ref[...]
    slots = []
    for k in range(TOP_K):
        e = route[:, ROUTE_E + k:ROUTE_E + k + 1] + float(MOE_GROUPS)
        base = jnp.sum(jnp.where(lane_f == e, ps, 0.0), axis=-1, keepdims=True)
        slots.append(base + route[:, ROUTE_R + k:ROUTE_R + k + 1])
    slab = jnp.where(lane_f == 0.0, slots[0], jnp.where(lane_f == 1.0, slots[1], 0.0))
    for i in range(T_SLOT // T_TOK):
        part = slab[i * T_TOK:(i + 1) * T_TOK, :].T
        dest_ref[i] = part[0:TOP_K, :].astype(jnp.int32)


def _slots(route, pad_start):
    n = route.shape[0]
    ps = jnp.pad(pad_start.astype(F32), (MOE_GROUPS, LANES - MOE_GROUPS - N_EXPERTS)).reshape(1, LANES)
    per = T_SLOT // T_TOK
    return pl.pallas_call(
        _slots_kernel,
        grid=(n // T_SLOT,),
        in_specs=[pl.BlockSpec((T_SLOT, LANES), lambda i: (i, 0)), _const_spec((1, LANES))],
        out_specs=pl.BlockSpec((per, TOP_K, T_TOK), lambda i: (i, 0, 0)),
        out_shape=jax.ShapeDtypeStruct((n // T_TOK, TOP_K, T_TOK), jnp.int32),
        compiler_params=_cparams(("arbitrary",)),
        name="slots",
    )(route, ps)


def _route(route, cnt, n_blocks):
    n = route.shape[0]
    counts = cnt[0, MOE_GROUPS:MOE_GROUPS + N_EXPERTS].astype(jnp.int32)
    padded = (counts + TM_EXP - 1) // TM_EXP * TM_EXP
    pad_end = jnp.cumsum(padded)
    pad_start = pad_end - padded
    dest3 = _slots(route, pad_start)
    n_used = (pad_end[-1:] // TM_EXP).astype(jnp.int32)
    blk = jnp.arange(n_blocks, dtype=jnp.int32)
    blk_expert = jnp.minimum(jnp.sum(pad_end[None, :] <= (blk * TM_EXP)[:, None], axis=1),
                             N_EXPERTS - 1).astype(jnp.int32)
    eid = jnp.arange(N_EXPERTS, dtype=jnp.int32)
    nonempty = counts > 0
    later = (eid[None, :] > eid[:, None]) & nonempty[None, :]
    next_used = jnp.min(jnp.where(later, eid[None, :], N_EXPERTS), axis=1)
    next_used = jnp.where(next_used == N_EXPERTS, eid, next_used)
    ordinal = jnp.cumsum(nonempty.astype(jnp.int32)) - 1
    first = ((blk * TM_EXP == pad_start[blk_expert]) & (blk < n_used[0])).astype(jnp.int32)
    sched = jnp.stack([blk_expert, first, next_used[blk_expert], ordinal[blk_expert] % 2])
    token = jnp.arange(n, dtype=jnp.int32).reshape(n // T_TOK, 1, T_TOK)
    token = jnp.broadcast_to(token, dest3.shape)
    row_tok = jnp.zeros((n_blocks * TM_EXP,), jnp.int32).at[dest3.reshape(-1)].add(token.reshape(-1))
    return dest3, sched.astype(jnp.int32), n_used, row_tok


def _gate_blocks(wa, wx):
    per = GATE_W // LRU_BW
    eye = jnp.eye(per, dtype=F32)

    def bd(w):
        w = w.reshape(LRU_HEADS // per, per, LRU_BW, LRU_BW)
        full = jnp.einsum('gpij,pq->gpiqj', w, eye)
        return full.reshape(LRU_HEADS // per, GATE_W, GATE_W)

    return jnp.concatenate([bd(wa), bd(wx)], axis=-1).astype(BF16)


def _pad_rows(w, rows):
    return jnp.pad(w, ((0, rows - w.shape[0]), (0, 0)))


def _pad_lanes(v):
    return jnp.pad(v, (0, LANES - v.shape[0])).reshape(1, LANES)


def kernel(x, norm_mix, w_in, lru_conv_w, lru_conv_b, lru_wa, lru_ba, lru_wx, lru_bx, lru_lambda, lru_norm, ssd_conv_w, ssd_conv_b, ssd_dt_bias, ssd_a_log, ssd_d, ssd_norm, w_out, norm_ffn, w_coarse, b_coarse, w_fine, b_fine, w_gate, w_up, w_down, final_norm):
    bsz, seq, d = x.shape
    n = bsz * seq
    depth = w_in.shape[0]
    n_assign = n * TOP_K
    n_blocks = -(-(n_assign + N_EXPERTS * (TM_EXP - 1)) // TM_EXP)
    n_blocks += n_blocks % 2
    o_dt = 2 * D_LRU + D_SSD + SSD_XBC
    w_in_bf = w_in.astype(BF16)
    w_out_bf = w_out.astype(BF16)

    xt = x.reshape(n, d)
    for i in range(depth):
        wdt = jnp.pad(w_in_bf[i, :, o_dt:], ((0, 0), (0, LANES - SSD_HEADS)))
        lx, lg, z, xbc, dt = _inproj(i, xt, norm_mix[i].reshape(1, d), w_in_bf, wdt)
        y_lru = _lru(lx, lg, _pad_rows(lru_conv_w[i], SUBLANES), lru_conv_b[i].reshape(1, -1),
                     _gate_blocks(lru_wa[i], lru_wx[i]), lru_ba[i].reshape(1, -1),
                     lru_bx[i].reshape(1, -1), lru_lambda[i].reshape(1, -1),
                     lru_norm[i].reshape(1, -1), bsz, seq)
        y_ssd = _ssd(xbc, z, dt, _pad_rows(ssd_conv_w[i], SUBLANES), ssd_conv_b[i].reshape(1, -1),
                     _pad_lanes(ssd_dt_bias[i]), _pad_lanes(ssd_a_log[i]),
                     jnp.repeat(ssd_d[i], SSD_HEAD_DIM).reshape(1, -1),
                     ssd_norm[i].reshape(1, -1), bsz, seq)
        w_r = jnp.concatenate([w_coarse[i], w_fine[i].transpose(1, 0, 2).reshape(d, N_EXPERTS)], axis=1)
        w_r = jnp.pad(w_r, ((0, 0), (0, LANES - w_r.shape[1])))
        r_bias = _pad_lanes(jnp.concatenate([b_coarse[i], b_fine[i].reshape(-1)]))
        x1, hp, route, cnt = _outproj(i, y_lru, y_ssd, xt, w_out_bf, norm_ffn[i].reshape(1, d),
                                      w_r.astype(BF16), r_bias)
        dest3, sched, n_used, row_tok = _route(route, cnt, n_blocks)
        yb = _experts(i, sched, n_used, row_tok, hp, w_gate, w_up, w_down)
        xt = _combine(dest3, x1, route, final_norm.reshape(1, d), yb, final_norm=(i == depth - 1))
    return xt.reshape(bsz, seq, d)
```

```python
import functools

import jax
import jax.numpy as jnp
from jax import lax
from jax.experimental import pallas as pl
from jax.experimental.pallas import tpu as pltpu

F32 = jnp.float32
BF16 = jnp.bfloat16
U32 = jnp.uint32

D_MODEL = 1024
D_LRU = 1024
LRU_HEADS = 16
LRU_BW = 64
RG_C = 8.0
CONV_K = 4
D_SSD = 1024
SSD_HEAD_DIM = 64
SSD_HEADS = 16
SSD_GROUPS = 4
SSD_STATE = 128
SSD_CHUNK = 128
MOE_GROUPS = 4
EXPERTS_PER_GROUP = 8
N_EXPERTS = 32
TOP_K = 2
D_EXPERT = 512
EPS = 1e-6
LOG2E = 1.4426950408889634

LANES = 128
SUBLANES = 8
VMEM_LIMIT = 60 * 1024 * 1024

TM_PROJ = 512
TM_SUB = 256
T_LRU = 512
T_SSD = 512
TM_EXP = 256
T_TOK = 256
GATE_W = 256
D_HALF = D_MODEL // 2
ROW_TILES = D_MODEL // LANES


def _cparams(sem):
    return pltpu.CompilerParams(dimension_semantics=sem, vmem_limit_bytes=VMEM_LIMIT)


def _const_spec(shape):
    n = len(shape)
    return pl.BlockSpec(shape, lambda *_: (0,) * n)


def _sigmoid(x):
    return 1.0 / (1.0 + jnp.exp(-x))


def _log1p(e):
    u = 1.0 + e
    d = u - 1.0
    return jnp.where(d == 0.0, e, jnp.log(u) * (e / jnp.where(d == 0.0, 1.0, d)))


def _softplus(x):
    return jnp.maximum(x, 0.0) + _log1p(jnp.exp(-jnp.abs(x)))


def _rms(x, w):
    ms = jnp.mean(x * x, axis=-1, keepdims=True)
    return x * lax.rsqrt(ms + EPS) * w


W_COL = 1024
SSD_XBC = D_SSD + 2 * SSD_GROUPS * SSD_STATE


def _inproj_kernel(x_ref, nw_ref, wlx_ref, wlg_ref, wz_ref, wx_ref, wbc_ref, wdt_ref,
                   lx_ref, lg_ref, z_ref, xbc_ref, dt_ref):
    h = _rms(x_ref[...], nw_ref[...]).astype(BF16)
    lx_ref[...] = jnp.dot(h, wlx_ref[...], preferred_element_type=F32).astype(BF16)
    lg_ref[...] = jnp.dot(h, wlg_ref[...], preferred_element_type=F32).astype(BF16)
    z_ref[...] = jnp.dot(h, wz_ref[...], preferred_element_type=F32).astype(BF16)
    xbc_ref[:, 0:W_COL] = jnp.dot(h, wx_ref[...], preferred_element_type=F32).astype(BF16)
    xbc_ref[:, W_COL:] = jnp.dot(h, wbc_ref[...], preferred_element_type=F32).astype(BF16)
    dt_ref[...] = jnp.dot(h, wdt_ref[...], preferred_element_type=F32)


def _inproj(layer, x, nw, w_in, wdt):
    n = x.shape[0]
    tm = TM_PROJ
    row = lambda w: pl.BlockSpec((tm, w), lambda i: (i, 0))
    wcol = lambda c: pl.BlockSpec((None, D_MODEL, W_COL), lambda i: (layer, 0, c))
    return pl.pallas_call(
        _inproj_kernel,
        grid=(n // tm,),
        in_specs=[row(D_MODEL), _const_spec((1, D_MODEL)),
                  wcol(0), wcol(1), wcol(2), wcol(3), wcol(4), _const_spec(wdt.shape)],
        out_specs=[row(D_LRU), row(D_LRU), row(D_SSD), row(SSD_XBC), row(LANES)],
        out_shape=[jax.ShapeDtypeStruct((n, D_LRU), BF16),
                   jax.ShapeDtypeStruct((n, D_LRU), BF16),
                   jax.ShapeDtypeStruct((n, D_SSD), BF16),
                   jax.ShapeDtypeStruct((n, SSD_XBC), BF16),
                   jax.ShapeDtypeStruct((n, LANES), F32)],
        compiler_params=_cparams(("arbitrary",)),
        name="inproj",
    )(x, nw, w_in, w_in, w_in, w_in, w_in, wdt)


def _causal_conv(xbuf, cw_ref, cb_ref, t):
    cw = cw_ref[...]
    n = t + SUBLANES
    full = xbuf[...]
    acc = cb_ref[...] + cw[CONV_K - 1:CONV_K, :] * full[SUBLANES:, :]
    for k in range(CONV_K - 1):
        shifted = pltpu.roll(full, n - (SUBLANES - 3 + k), 0)[0:t, :]
        acc = acc + cw[k:k + 1, :] * shifted
    return acc


def _lru_kernel(lx_ref, lg_ref, cw_ref, cb_ref, wg_ref, ba_ref, bx_ref, lam_ref, nw_ref,
                y_ref, xbuf, a_s, v_s, h_s, hcarry):
    t = T_LRU
    j = pl.program_id(1)

    @pl.when(j == 0)
    def _():
        xbuf[0:SUBLANES, :] = jnp.zeros((SUBLANES, D_LRU), F32)
        hcarry[...] = jnp.zeros_like(hcarry)

    xbuf[SUBLANES:SUBLANES + t, :] = lx_ref[...].astype(F32)
    u = _causal_conv(xbuf, cw_ref, cb_ref, t)
    xbuf[0:SUBLANES, :] = xbuf[t:t + SUBLANES, :]

    lam = lam_ref[...]
    log_sig = jnp.minimum(lam, 0.0) - _log1p(jnp.exp(-jnp.abs(lam)))
    rate = (RG_C * LOG2E) * log_sig
    for g in range(D_LRU // GATE_W):
        sl = slice(g * GATE_W, (g + 1) * GATE_W)
        ug = u[:, sl]
        gates = jnp.dot(ug.astype(BF16), wg_ref[g], preferred_element_type=F32)
        r = _sigmoid(gates[:, :GATE_W] + ba_ref[:, sl])
        i = _sigmoid(gates[:, GATE_W:] + bx_ref[:, sl])
        a = jnp.exp2(r * rate[:, sl])
        mult = jnp.sqrt(1.0 - a * a)
        a_s[:, sl] = a
        v_s[:, sl] = mult * (i * ug)

    row = lax.broadcasted_iota(jnp.int32, (SUBLANES, D_LRU), 0)

    def scan_body(k, hprev):
        r0 = pl.multiple_of(k * SUBLANES, SUBLANES)
        a = a_s[pl.ds(r0, SUBLANES), :]
        v = v_s[pl.ds(r0, SUBLANES), :]
        for s in (1, 2, 4):
            keep = row >= s
            a_sh = jnp.where(keep, pltpu.roll(a, s, 0), 1.0)
            v_sh = jnp.where(keep, pltpu.roll(v, s, 0), 0.0)
            v = v + a * v_sh
            a = a * a_sh
        h = v + a * hprev
        h_s[pl.ds(r0, SUBLANES), :] = h
        return h[SUBLANES - 1:SUBLANES, :]

    hcarry[...] = lax.fori_loop(0, t // SUBLANES, scan_body, hcarry[...], unroll=4)

    g_in = lg_ref[...].astype(F32)
    inner = g_in * (0.7978845608028654 + 0.035677408136300125 * (g_in * g_in))
    half_g = 0.5 * g_in
    gelu = half_g + half_g * jnp.tanh(inner)
    y_ref[...] = _rms(gelu * h_s[...], nw_ref[...]).astype(BF16)


def _lru(lx, lg, cw, cb, wg, ba, bx, lam, nw, bsz, seq):
    t = T_LRU
    nj = seq // t
    row = pl.BlockSpec((t, D_LRU), lambda b, j: (b * nj + j, 0))
    vec = _const_spec((1, D_LRU))
    return pl.pallas_call(
        _lru_kernel,
        grid=(bsz, nj),
        in_specs=[row, row, _const_spec(cw.shape), vec, _const_spec(wg.shape),
                  vec, vec, vec, vec],
        out_specs=row,
        out_shape=jax.ShapeDtypeStruct((bsz * seq, D_LRU), BF16),
        scratch_shapes=[pltpu.VMEM((t + SUBLANES, D_LRU), F32),
                        pltpu.VMEM((t, D_LRU), F32),
                        pltpu.VMEM((t, D_LRU), F32),
                        pltpu.VMEM((t, D_LRU), F32),
                        pltpu.VMEM((1, D_LRU), F32)],
        compiler_params=_cparams(("arbitrary", "arbitrary")),
        name="rglru",
    )(lx, lg, cw, cb, wg, ba, bx, lam, nw)


def _split3(x):
    hi = x.astype(BF16)
    r1 = x - hi.astype(F32)
    mid = r1.astype(BF16)
    lo = (r1 - mid.astype(F32)).astype(BF16)
    return hi, mid, lo


def _ssd_kernel(xbc_ref, z_ref, dt_ref, cw_ref, cb_ref, dtb_ref, alog_ref, dvec_ref, nw_ref,
                tril_ref, fut_ref, y_ref, xbuf, xc_s, state, y_s):
    tt = xbc_ref.shape[0]
    j = pl.program_id(1)

    @pl.when(j == 0)
    def _():
        xbuf[0:SUBLANES, :] = jnp.zeros((SUBLANES, xbuf.shape[1]), F32)
        state[...] = jnp.zeros_like(state)

    xbuf[SUBLANES:SUBLANES + tt, :] = xbc_ref[...].astype(F32)
    u = _causal_conv(xbuf, cw_ref, cb_ref, tt)
    xbuf[0:SUBLANES, :] = xbuf[tt:tt + SUBLANES, :]
    xc_s[...] = u * _sigmoid(u)
    for c in range(tt // SSD_CHUNK):
        _ssd_chunk(c, xc_s, dt_ref, dtb_ref, alog_ref, tril_ref, fut_ref, state, y_s)

    zf = z_ref[...].astype(F32)
    y = (y_s[...] + xc_s[:, 0:D_SSD] * dvec_ref[...]) * (zf * _sigmoid(zf))
    y_ref[...] = _rms(y, nw_ref[...]).astype(BF16)


def _ssd_chunk(c, xc_s, dt_ref, dtb_ref, alog_ref, tril_ref, fut_ref, state, y_s):
    t = SSD_CHUNK
    rows = slice(c * t, (c + 1) * t)
    gn = SSD_GROUPS * SSD_STATE
    xs = xc_s[rows, 0:D_SSD]
    bm = xc_s[rows, D_SSD:D_SSD + gn]
    cm = xc_s[rows, D_SSD + gn:]

    dt = _softplus(dt_ref[rows, :] + dtb_ref[...])
    d_a = dt * (-jnp.exp(alog_ref[...]))
    tril = tril_ref[...]
    future = fut_ref[...]
    hi, mid, lo = _split3(d_a)
    a_cs = (jnp.dot(tril, hi, preferred_element_type=F32)
            + jnp.dot(tril, mid, preferred_element_type=F32)
            + jnp.dot(tril, lo, preferred_element_type=F32))
    a_cs_t = a_cs.T
    dt_t = dt.T
    a_last_t = a_cs_t[:, t - 1:t]
    w_state_t = jnp.exp(a_last_t - a_cs_t) * dt_t
    chunk_decay_t = jnp.exp(a_last_t)
    src_t = a_cs_t - jnp.log(dt_t)

    lane = lax.broadcasted_iota(jnp.int32, (1, LANES), 1)
    first = lane < SSD_HEAD_DIM

    heads_per_group = SSD_HEADS // SSD_GROUPS
    for g in range(SSD_GROUPS):
        gsl = slice(g * SSD_STATE, (g + 1) * SSD_STATE)
        c_g = cm[:, gsl].astype(BF16)
        b_g = bm[:, gsl]
        scores = lax.dot_general(c_g, b_g.astype(BF16), (((1,), (1,)), ((), ())),
                                 preferred_element_type=F32)
        b_t = b_g.T
        st_g = state[g]
        y_off = jnp.dot(c_g, st_g.astype(BF16), preferred_element_type=F32)
        for q in range(heads_per_group // 2):
            h0 = g * heads_per_group + 2 * q
            psl = slice(h0 * SSD_HEAD_DIM, (h0 + 2) * SSD_HEAD_DIM)
            lsl = slice(2 * q * SSD_HEAD_DIM, (2 * q + 2) * SSD_HEAD_DIM)
            x_pair = xs[:, psl].astype(BF16)
            yd, ns, cols, cd = [], [], [], []
            for h in (h0, h0 + 1):
                col = jnp.broadcast_to(a_cs[:, h:h + 1], (t, t))
                lmat_dt = jnp.exp(col - src_t[h:h + 1, :] + future)
                m = (scores * lmat_dt).astype(BF16)
                yd.append(jnp.dot(m, x_pair, preferred_element_type=F32))
                bw = (b_t * w_state_t[h:h + 1, :]).astype(BF16)
                ns.append(jnp.dot(bw, x_pair, preferred_element_type=F32))
                cols.append(col)
                cd.append(jnp.broadcast_to(chunk_decay_t[h:h + 1, :], (1, LANES)))
            y_pair = (jnp.where(first, yd[0], yd[1])
                      + jnp.exp(jnp.where(first, cols[0], cols[1])) * y_off[:, lsl])
            y_s[rows, psl] = y_pair
            state[g, :, lsl] = (st_g[:, lsl] * jnp.where(first, cd[0], cd[1])
                                + jnp.where(first, ns[0], ns[1]))


def _ssd(xbc, z, dt, cw, cb, dtb, alog, dvec, nw, bsz, seq):
    t = T_SSD
    nj = seq // t
    dx = xbc.shape[1]
    row = lambda w: pl.BlockSpec((t, w), lambda b, j: (b * nj + j, 0))
    ch = SSD_CHUNK
    causal = jnp.arange(ch)[:, None] >= jnp.arange(ch)[None, :]
    return pl.pallas_call(
        _ssd_kernel,
        grid=(bsz, nj),
        in_specs=[row(dx), row(D_SSD), row(LANES), _const_spec(cw.shape), _const_spec((1, dx)),
                  _const_spec((1, LANES)), _const_spec((1, LANES)),
                  _const_spec((1, D_SSD)), _const_spec((1, D_SSD)),
                  _const_spec((ch, ch)), _const_spec((ch, ch))],
        out_specs=row(D_SSD),
        out_shape=jax.ShapeDtypeStruct((bsz * seq, D_SSD), BF16),
        scratch_shapes=[pltpu.VMEM((t + SUBLANES, dx), F32),
                        pltpu.VMEM((t, dx), F32),
                        pltpu.VMEM((SSD_GROUPS, SSD_STATE, D_SSD // SSD_GROUPS), F32),
                        pltpu.VMEM((t, D_SSD), F32)],
        compiler_params=_cparams(("arbitrary", "arbitrary")),
        name="ssd",
    )(xbc, z, dt, cw, cb, dtb, alog, dvec, nw, causal.astype(BF16),
      jnp.where(causal, 0.0, NEG_BIG).astype(F32))


ROUTE_E, ROUTE_W, ROUTE_R = 0, 2, 4
NEG_BIG = -1e30


def _first_argmax(vals, lane_f):
    m = jnp.max(vals, axis=-1, keepdims=True)
    idx = jnp.min(jnp.where(vals == m, lane_f, float(LANES)), axis=-1, keepdims=True)
    return m, idx


def _outproj_kernel(yl_ref, ys_ref, x_ref, wo_ref, nw_ref, wr_ref, rb_ref,
                    x1_ref, hp_ref, route_ref, cnt_ref, stril, running):
    @pl.when(pl.program_id(0) == 0)
    def _():
        ri = lax.broadcasted_iota(jnp.int32, (TM_SUB, TM_SUB), 0)
        ci = lax.broadcasted_iota(jnp.int32, (TM_SUB, TM_SUB), 1)
        stril[...] = (ri > ci).astype(BF16)
        running[...] = jnp.zeros_like(running)

    for s in range(x_ref.shape[0] // TM_SUB):
        _outproj_rows(pl.ds(s * TM_SUB, TM_SUB), yl_ref, ys_ref, x_ref, wo_ref, nw_ref, wr_ref, rb_ref,
                      x1_ref, hp_ref, route_ref, stril, running)
    cnt_ref[...] = running[...]


def _outproj_rows(rows, yl_ref, ys_ref, x_ref, wo_ref, nw_ref, wr_ref, rb_ref,
                  x1_ref, hp_ref, route_ref, stril, running):
    tm = TM_SUB
    y_cat = jnp.concatenate([yl_ref[rows, :], ys_ref[rows, :]], axis=1)
    x1 = x_ref[rows, :] + jnp.dot(y_cat, wo_ref[...], preferred_element_type=F32)
    x1_ref[rows, :] = x1
    h = _rms(x1, nw_ref[...])
    h_hi = h.astype(BF16)
    h_rt = h_hi.astype(F32)
    bits = lax.bitcast_convert_type(h_rt, U32)
    hp_ref[rows, :] = bits[:, D_HALF:] | (bits[:, :D_HALF] >> 16)
    logits = jnp.dot(h_hi, wr_ref[...], preferred_element_type=F32) + rb_ref[...]

    lane = lax.broadcasted_iota(jnp.int32, (tm, LANES), 1)
    lane_f = lane.astype(F32)
    is_c = lane < MOE_GROUPS
    lc = jnp.where(is_c, logits, NEG_BIG)
    m_c, g_idx = _first_argmax(lc, lane_f)
    g_w = 1.0 / jnp.sum(jnp.where(is_c, jnp.exp(lc - m_c), 0.0), axis=-1, keepdims=True)
    lo = float(MOE_GROUPS) + float(EXPERTS_PER_GROUP) * g_idx
    is_f = (lane_f >= lo) & (lane_f < lo + float(EXPERTS_PER_GROUP))
    lf = jnp.where(is_f, logits, NEG_BIG)
    v1, i1 = _first_argmax(lf, lane_f)
    v2, i2 = _first_argmax(jnp.where(lane_f == i1, NEG_BIG, lf), lane_f)
    ex = jnp.exp(v2 - v1)
    w1 = g_w / (1.0 + ex)
    w2 = g_w * ex / (1.0 + ex)

    oh1 = (lane_f == i1).astype(F32)
    oh2 = (lane_f == i2).astype(F32)
    both = oh1 + oh2
    before = jnp.dot(stril[...], both.astype(BF16), preferred_element_type=F32) + running[...]
    r1 = jnp.sum(oh1 * before, axis=-1, keepdims=True)
    r2 = jnp.sum(oh2 * before, axis=-1, keepdims=True)
    running[...] = running[...] + jnp.sum(both, axis=0, keepdims=True)

    e1 = i1 - float(MOE_GROUPS)
    e2 = i2 - float(MOE_GROUPS)
    route = jnp.zeros((tm, LANES), F32)
    for off, (a, b) in ((ROUTE_E, (e1, e2)), (ROUTE_W, (w1, w2)), (ROUTE_R, (r1, r2))):
        route = jnp.where(lane == off, a, jnp.where(lane == off + 1, b, route))
    route_ref[rows, :] = route


def _outproj(layer, yl, ys, x, wo, nw, wr, rb):
    n = x.shape[0]
    tm = TM_PROJ
    row = lambda w: pl.BlockSpec((tm, w), lambda i: (i, 0))
    return pl.pallas_call(
        _outproj_kernel,
        grid=(n // tm,),
        in_specs=[row(D_LRU), row(D_SSD), row(D_MODEL),
                  pl.BlockSpec((None,) + wo.shape[1:], lambda i: (layer, 0, 0)),
                  _const_spec((1, D_MODEL)),
                  _const_spec(wr.shape), _const_spec((1, LANES))],
        out_specs=[row(D_MODEL), row(D_HALF), row(LANES), _const_spec((1, LANES))],
        out_shape=[jax.ShapeDtypeStruct((n, D_MODEL), F32),
                   jax.ShapeDtypeStruct((n, D_HALF), U32),
                   jax.ShapeDtypeStruct((n, LANES), F32),
                   jax.ShapeDtypeStruct((1, LANES), F32)],
        scratch_shapes=[pltpu.VMEM((TM_SUB, TM_SUB), BF16), pltpu.VMEM((1, LANES), F32)],
        compiler_params=_cparams(("arbitrary",)),
        name="outproj",
    )(yl, ys, x, wo, nw, wr, rb)


SCHED_EXPERT, SCHED_FIRST, SCHED_NEXT, SCHED_SLOT = 0, 1, 2, 3


def _expert_kernel(sch_ref, nu_ref, rt_ref, rtn_ref, hp_hbm, wg_hbm, wu_hbm, wd_hbm, yb_ref,
                   hp_v, xs0, xs1, wg_f, wu_f, wd_f, wg_s, wu_s, wd_s, sem, wsem, *, layer):
    j = pl.program_id(0)
    out_rows = TM_EXP * ROW_TILES

    def weight_copies(expert, s):
        return (pltpu.make_async_copy(wg_hbm.at[layer, expert], wg_f.at[s], wsem.at[s, 0]),
                pltpu.make_async_copy(wu_hbm.at[layer, expert], wu_f.at[s], wsem.at[s, 1]),
                pltpu.make_async_copy(wd_hbm.at[layer, expert], wd_f.at[s], wsem.at[s, 2]))

    def gather(idx_ref, row, dst):
        for r in range(TM_EXP):
            dst[r:r + 1, :] = hp_v[pl.ds(idx_ref[row, r], 1), :]

    @pl.when(j == 0)
    def _():
        for cp in weight_copies(sch_ref[SCHED_EXPERT, 0], sch_ref[SCHED_SLOT, 0]):
            cp.start()
        cp = pltpu.make_async_copy(hp_hbm, hp_v, sem)
        cp.start()
        cp.wait()
        gather(rt_ref, 0, xs0)

    def block(half, src, prefetch):
        i = 2 * j + half
        e = sch_ref[SCHED_EXPERT, i]
        slot = sch_ref[SCHED_SLOT, i]
        active = i < nu_ref[0]
        out = pl.ds(half * out_rows, out_rows)

        @pl.when(active & (sch_ref[SCHED_FIRST, i] == 1))
        def _():
            for cp in weight_copies(e, slot):
                cp.wait()
            wg_s[...] = wg_f[slot].astype(BF16)
            wu_s[...] = wu_f[slot].astype(BF16)
            wd_s[...] = wd_f[slot].astype(BF16)
            nxt = sch_ref[SCHED_NEXT, i]

            @pl.when(nxt != e)
            def _():
                for cp in weight_copies(nxt, 1 - slot):
                    cp.start()

        @pl.when(active)
        def _():
            prefetch()
            packed = src[...]
            x_lo = lax.bitcast_convert_type(packed << 16, F32).astype(BF16)
            x_hi = lax.bitcast_convert_type((packed >> 16) << 16, F32).astype(BF16)
            gate = (jnp.dot(x_lo, wg_s[0:D_HALF, :], preferred_element_type=F32)
                    + jnp.dot(x_hi, wg_s[D_HALF:, :], preferred_element_type=F32))
            up = (jnp.dot(x_lo, wu_s[0:D_HALF, :], preferred_element_type=F32)
                  + jnp.dot(x_hi, wu_s[D_HALF:, :], preferred_element_type=F32))
            hid = (gate * _sigmoid(gate) * up).astype(BF16)
            y = jnp.dot(hid, wd_s[...], preferred_element_type=F32)
            for c in range(ROW_TILES):
                yb_ref[pl.ds(half * out_rows + c, TM_EXP, stride=ROW_TILES), :] = (
                    y[:, c * LANES:(c + 1) * LANES])

        @pl.when(jnp.logical_not(active))
        def _():
            yb_ref[out, :] = jnp.zeros((out_rows, LANES), F32)

    block(0, xs0, lambda: gather(rt_ref, 1, xs1))
    block(1, xs1, lambda: gather(rtn_ref, 0, xs0))


def _experts(layer, sched, n_used, row_tok, hp, wg, wu, wd):
    n_blocks = sched.shape[1]
    tm = TM_EXP
    n_steps = n_blocks // 2
    hbm = pl.BlockSpec(memory_space=pl.ANY)
    grid_spec = pltpu.PrefetchScalarGridSpec(
        num_scalar_prefetch=2,
        grid=(n_steps,),
        in_specs=[pl.BlockSpec((None, 2, tm), lambda j, sch, nu: (j, 0, 0), memory_space=pltpu.SMEM),
                  pl.BlockSpec((None, 2, tm), lambda j, sch, nu: (jnp.minimum(j + 1, n_steps - 1), 0, 0),
                               memory_space=pltpu.SMEM),
                  hbm, hbm, hbm, hbm],
        out_specs=pl.BlockSpec((2 * tm * ROW_TILES, LANES), lambda j, sch, nu: (j, 0)),
        scratch_shapes=[pltpu.VMEM(hp.shape, U32),
                        pltpu.VMEM((tm, D_HALF), U32),
                        pltpu.VMEM((tm, D_HALF), U32),
                        pltpu.VMEM((2, D_MODEL, D_EXPERT), F32),
                        pltpu.VMEM((2, D_MODEL, D_EXPERT), F32),
                        pltpu.VMEM((2, D_EXPERT, D_MODEL), F32),
                        pltpu.VMEM((D_MODEL, D_EXPERT), BF16),
                        pltpu.VMEM((D_MODEL, D_EXPERT), BF16),
                        pltpu.VMEM((D_EXPERT, D_MODEL), BF16),
                        pltpu.SemaphoreType.DMA(()),
                        pltpu.SemaphoreType.DMA((2, 3))],
    )
    return pl.pallas_call(
        functools.partial(_expert_kernel, layer=layer),
        grid_spec=grid_spec,
        out_shape=jax.ShapeDtypeStruct((n_blocks * tm * ROW_TILES, LANES), F32),
        compiler_params=_cparams(("arbitrary",)),
        name="experts",
    )(sched, n_used, row_tok.reshape(n_steps, 2, tm), row_tok.reshape(n_steps, 2, tm), hp, wg, wu, wd)


def _combine_kernel(dcur_ref, dnxt_ref, x_ref, w_ref, nw_ref, yb_ref, o_ref, gbuf, sems,
                    *, final_norm):
    t = T_TOK
    i = pl.program_id(0)
    n_steps = pl.num_programs(0)
    slot = i % 2

    def copy(dref, s, k, tok):
        src = pl.ds(pl.multiple_of(dref[k, tok] * ROW_TILES, ROW_TILES), ROW_TILES)
        row0 = tok * ROW_TILES
        dst = pl.ds(row0 if isinstance(tok, int) else pl.multiple_of(row0, ROW_TILES), ROW_TILES)
        return pltpu.make_async_copy(yb_ref.at[src, :], gbuf.at[s, k, dst, :], sems.at[s])

    def issue(dref, s):
        def body(tok, c):
            for k in range(TOP_K):
                copy(dref, s, k, tok).start()
            return c
        lax.fori_loop(0, t, body, 0, unroll=8)

    @pl.when(i == 0)
    def _():
        issue(dcur_ref, 0)

    @pl.when(i + 1 < n_steps)
    def _():
        for tok in range(t):
            for k in range(TOP_K):
                copy(dnxt_ref, 1 - slot, k, tok).start(priority=k)

    def wait(tok, c):
        for k in range(TOP_K):
            copy(dcur_ref, slot, k, tok).wait()
        return c

    lax.fori_loop(0, t, wait, 0, unroll=8)

    w0 = w_ref[:, ROUTE_W:ROUTE_W + 1]
    w1 = w_ref[:, ROUTE_W + 1:ROUTE_W + 2]
    parts = []
    for c in range(ROW_TILES):
        g0 = gbuf[slot, 0, pl.ds(c, t, stride=ROW_TILES), :]
        g1 = gbuf[slot, 1, pl.ds(c, t, stride=ROW_TILES), :]
        parts.append(x_ref[:, c * LANES:(c + 1) * LANES] + w0 * g0 + w1 * g1)
    out = jnp.concatenate(parts, axis=-1)
    if final_norm:
        out = _rms(out, nw_ref[...])
    o_ref[...] = out


def _combine(dest3, x, route, nw, yb, final_norm):
    n = x.shape[0]
    t = T_TOK
    n_steps = n // t
    dspec = lambda f: pl.BlockSpec((None, TOP_K, t), lambda i: (f(i), 0, 0), memory_space=pltpu.SMEM)
    return pl.pallas_call(
        functools.partial(_combine_kernel, final_norm=final_norm),
        grid=(n_steps,),
        in_specs=[dspec(lambda i: i), dspec(lambda i: jnp.minimum(i + 1, n_steps - 1)),
                  pl.BlockSpec((t, D_MODEL), lambda i: (i, 0)),
                  pl.BlockSpec((t, LANES), lambda i: (i, 0)),
                  _const_spec((1, D_MODEL)),
                  pl.BlockSpec(memory_space=pl.ANY)],
        out_specs=pl.BlockSpec((t, D_MODEL), lambda i: (i, 0)),
        out_shape=jax.ShapeDtypeStruct((n, D_MODEL), F32),
        scratch_shapes=[pltpu.VMEM((2, TOP_K, t * ROW_TILES, LANES), F32),
                        pltpu.SemaphoreType.DMA((2,))],
        compiler_params=_cparams(("arbitrary",)),
        name="combine",
    )(dest3, dest3, x, route, nw, yb)


T_SLOT = 1024


def _slots_kernel(route_ref, ps_ref, dest_ref):
    route = route_ref[...]
    lane_f = lax.broadcasted_iota(jnp.int32, route.shape, 1).astype(F32)
    ps = ps_ref[...]
    slots = []
    for k in range(TOP_K):
        e = route[:, ROUTE_E + k:ROUTE_E + k + 1] + float(MOE_GROUPS)
        base = jnp.sum(jnp.where(lane_f == e, ps, 0.0), axis=-1, keepdims=True)
        slots.append(base + route[:, ROUTE_R + k:ROUTE_R + k + 1])
    slab = jnp.where(lane_f == 0.0, slots[0], jnp.where(lane_f == 1.0, slots[1], 0.0))
    for i in range(T_SLOT // T_TOK):
        part = slab[i * T_TOK:(i + 1) * T_TOK, :].T
        dest_ref[i] = part[0:TOP_K, :].astype(jnp.int32)


def _slots(route, pad_start):
    n = route.shape[0]
    ps = jnp.pad(pad_start.astype(F32), (MOE_GROUPS, LANES - MOE_GROUPS - N_EXPERTS)).reshape(1, LANES)
    per = T_SLOT // T_TOK
    return pl.pallas_call(
        _slots_kernel,
        grid=(n // T_SLOT,),
        in_specs=[pl.BlockSpec((T_SLOT, LANES), lambda i: (i, 0)), _const_spec((1, LANES))],
        out_specs=pl.BlockSpec((per, TOP_K, T_TOK), lambda i: (i, 0, 0)),
        out_shape=jax.ShapeDtypeStruct((n // T_TOK, TOP_K, T_TOK), jnp.int32),
        compiler_params=_cparams(("arbitrary",)),
        name="slots",
    )(route, ps)


def _route(route, cnt, n_blocks):
    n = route.shape[0]
    counts = cnt[0, MOE_GROUPS:MOE_GROUPS + N_EXPERTS].astype(jnp.int32)
    padded = (counts + TM_EXP - 1) // TM_EXP * TM_EXP
    pad_end = jnp.cumsum(padded)
    pad_start = pad_end - padded
    dest3 = _slots(route, pad_start)
    n_used = (pad_end[-1:] // TM_EXP).astype(jnp.int32)
    blk = jnp.arange(n_blocks, dtype=jnp.int32)
    blk_expert = jnp.minimum(jnp.sum(pad_end[None, :] <= (blk * TM_EXP)[:, None], axis=1),
                             N_EXPERTS - 1).astype(jnp.int32)
    eid = jnp.arange(N_EXPERTS, dtype=jnp.int32)
    nonempty = counts > 0
    later = (eid[None, :] > eid[:, None]) & nonempty[None, :]
    next_used = jnp.min(jnp.where(later, eid[None, :], N_EXPERTS), axis=1)
    next_used = jnp.where(next_used == N_EXPERTS, eid, next_used)
    ordinal = jnp.cumsum(nonempty.astype(jnp.int32)) - 1
    first = ((blk * TM_EXP == pad_start[blk_expert]) & (blk < n_used[0])).astype(jnp.int32)
    sched = jnp.stack([blk_expert, first, next_used[blk_expert], ordinal[blk_expert] % 2])
    token = jnp.arange(n, dtype=jnp.int32).reshape(n // T_TOK, 1, T_TOK)
    token = jnp.broadcast_to(token, dest3.shape)
    row_tok = jnp.zeros((n_blocks * TM_EXP,), jnp.int32).at[dest3.reshape(-1)].add(
        token.reshape(-1), unique_indices=True)
    return dest3, sched.astype(jnp.int32), n_used, row_tok


def _gate_blocks(wa, wx):
    per = GATE_W // LRU_BW
    eye = jnp.eye(per, dtype=F32)

    def bd(w):
        w = w.reshape(LRU_HEADS // per, per, LRU_BW, LRU_BW)
        full = jnp.einsum('gpij,pq->gpiqj', w, eye)
        return full.reshape(LRU_HEADS // per, GATE_W, GATE_W)

    return jnp.concatenate([bd(wa), bd(wx)], axis=-1).astype(BF16)


def _pad_rows(w, rows):
    return jnp.pad(w, ((0, rows - w.shape[0]), (0, 0)))


def _pad_lanes(v):
    return jnp.pad(v, (0, LANES - v.shape[0])).reshape(1, LANES)


def kernel(x, norm_mix, w_in, lru_conv_w, lru_conv_b, lru_wa, lru_ba, lru_wx, lru_bx, lru_lambda, lru_norm, ssd_conv_w, ssd_conv_b, ssd_dt_bias, ssd_a_log, ssd_d, ssd_norm, w_out, norm_ffn, w_coarse, b_coarse, w_fine, b_fine, w_gate, w_up, w_down, final_norm):
    bsz, seq, d = x.shape
    n = bsz * seq
    depth = w_in.shape[0]
    n_assign = n * TOP_K
    n_blocks = -(-(n_assign + N_EXPERTS * (TM_EXP - 1)) // TM_EXP)
    n_blocks += n_blocks % 2
    o_dt = 2 * D_LRU + D_SSD + SSD_XBC
    w_in_bf = w_in.astype(BF16)
    w_out_bf = w_out.astype(BF16)

    xt = x.reshape(n, d)
    for i in range(depth):
        wdt = jnp.pad(w_in_bf[i, :, o_dt:], ((0, 0), (0, LANES - SSD_HEADS)))
        lx, lg, z, xbc, dt = _inproj(i, xt, norm_mix[i].reshape(1, d), w_in_bf, wdt)
        y_lru = _lru(lx, lg, _pad_rows(lru_conv_w[i], SUBLANES), lru_conv_b[i].reshape(1, -1),
                     _gate_blocks(lru_wa[i], lru_wx[i]), lru_ba[i].reshape(1, -1),
                     lru_bx[i].reshape(1, -1), lru_lambda[i].reshape(1, -1),
                     lru_norm[i].reshape(1, -1), bsz, seq)
        y_ssd = _ssd(xbc, z, dt, _pad_rows(ssd_conv_w[i], SUBLANES), ssd_conv_b[i].reshape(1, -1),
                     _pad_lanes(ssd_dt_bias[i]), _pad_lanes(ssd_a_log[i]),
                     jnp.repeat(ssd_d[i], SSD_HEAD_DIM).reshape(1, -1),
                     ssd_norm[i].reshape(1, -1), bsz, seq)
        w_r = jnp.concatenate([w_coarse[i], w_fine[i].transpose(1, 0, 2).reshape(d, N_EXPERTS)], axis=1)
        w_r = jnp.pad(w_r, ((0, 0), (0, LANES - w_r.shape[1])))
        r_bias = _pad_lanes(jnp.concatenate([b_coarse[i], b_fine[i].reshape(-1)]))
        x1, hp, route, cnt = _outproj(i, y_lru, y_ssd, xt, w_out_bf, norm_ffn[i].reshape(1, d),
                                      w_r.astype(BF16), r_bias)
        dest3, sched, n_used, row_tok = _route(route, cnt, n_blocks)
        yb = _experts(i, sched, n_used, row_tok, hp, w_gate, w_up, w_down)
        xt = _combine(dest3, x1, route, final_norm.reshape(1, d), yb, final_norm=(i == depth - 1))
    return xt.reshape(bsz, seq, d)
```

```python
import functools

import jax
import jax.numpy as jnp
from jax import lax
from jax.experimental import pallas as pl
from jax.experimental.pallas import tpu as pltpu

F32 = jnp.float32
BF16 = jnp.bfloat16
U32 = jnp.uint32

D_MODEL = 1024
D_LRU = 1024
LRU_HEADS = 16
LRU_BW = 64
RG_C = 8.0
CONV_K = 4
D_SSD = 1024
SSD_HEAD_DIM = 64
SSD_HEADS = 16
SSD_GROUPS = 4
SSD_STATE = 128
SSD_CHUNK = 128
MOE_GROUPS = 4
EXPERTS_PER_GROUP = 8
N_EXPERTS = 32
TOP_K = 2
D_EXPERT = 512
EPS = 1e-6
LOG2E = 1.4426950408889634

LANES = 128
SUBLANES = 8
VMEM_LIMIT = 60 * 1024 * 1024

TM_PROJ = 512
TM_SUB = 256
T_LRU = 512
T_SSD = 512
TM_EXP = 256
T_TOK = 256
GATE_W = 256
D_HALF = D_MODEL // 2
ROW_TILES = D_MODEL // LANES


def _cparams(sem):
    return pltpu.CompilerParams(dimension_semantics=sem, vmem_limit_bytes=VMEM_LIMIT)


def _const_spec(shape):
    n = len(shape)
    return pl.BlockSpec(shape, lambda *_: (0,) * n)


def _sigmoid(x):
    return 1.0 / (1.0 + jnp.exp(-x))


def _log1p(e):
    u = 1.0 + e
    d = u - 1.0
    return jnp.where(d == 0.0, e, jnp.log(u) * (e / jnp.where(d == 0.0, 1.0, d)))


def _softplus(x):
    return jnp.maximum(x, 0.0) + _log1p(jnp.exp(-jnp.abs(x)))


def _rms(x, w):
    ms = jnp.mean(x * x, axis=-1, keepdims=True)
    return x * lax.rsqrt(ms + EPS) * w


W_COL = 1024
SSD_XBC = D_SSD + 2 * SSD_GROUPS * SSD_STATE


def _inproj_kernel(x_ref, nw_ref, wlx_ref, wlg_ref, wz_ref, wx_ref, wbc_ref, wdt_ref,
                   lx_ref, lg_ref, z_ref, xbc_ref, dt_ref):
    h = _rms(x_ref[...], nw_ref[...]).astype(BF16)
    lx_ref[...] = jnp.dot(h, wlx_ref[...], preferred_element_type=F32).astype(BF16)
    lg_ref[...] = jnp.dot(h, wlg_ref[...], preferred_element_type=F32).astype(BF16)
    z_ref[...] = jnp.dot(h, wz_ref[...], preferred_element_type=F32).astype(BF16)
    xbc_ref[:, 0:W_COL] = jnp.dot(h, wx_ref[...], preferred_element_type=F32).astype(BF16)
    xbc_ref[:, W_COL:] = jnp.dot(h, wbc_ref[...], preferred_element_type=F32).astype(BF16)
    dt_ref[...] = jnp.dot(h, wdt_ref[...], preferred_element_type=F32)


def _inproj(layer, x, nw, w_in, wdt):
    n = x.shape[0]
    tm = TM_PROJ
    row = lambda w: pl.BlockSpec((tm, w), lambda i: (i, 0))
    wcol = lambda c: pl.BlockSpec((None, D_MODEL, W_COL), lambda i: (layer, 0, c))
    return pl.pallas_call(
        _inproj_kernel,
        grid=(n // tm,),
        in_specs=[row(D_MODEL), _const_spec((1, D_MODEL)),
                  wcol(0), wcol(1), wcol(2), wcol(3), wcol(4), _const_spec(wdt.shape)],
        out_specs=[row(D_LRU), row(D_LRU), row(D_SSD), row(SSD_XBC), row(LANES)],
        out_shape=[jax.ShapeDtypeStruct((n, D_LRU), BF16),
                   jax.ShapeDtypeStruct((n, D_LRU), BF16),
                   jax.ShapeDtypeStruct((n, D_SSD), BF16),
                   jax.ShapeDtypeStruct((n, SSD_XBC), BF16),
                   jax.ShapeDtypeStruct((n, LANES), F32)],
        compiler_params=_cparams(("arbitrary",)),
        name="inproj",
    )(x, nw, w_in, w_in, w_in, w_in, w_in, wdt)


def _causal_conv(xbuf, cw_ref, cb_ref, t):
    cw = cw_ref[...]
    n = t + SUBLANES
    full = xbuf[...]
    acc = cb_ref[...] + cw[CONV_K - 1:CONV_K, :] * full[SUBLANES:, :]
    for k in range(CONV_K - 1):
        shifted = pltpu.roll(full, n - (SUBLANES - 3 + k), 0)[0:t, :]
        acc = acc + cw[k:k + 1, :] * shifted
    return acc


def _lru_kernel(lx_ref, lg_ref, cw_ref, cb_ref, wg_ref, ba_ref, bx_ref, lam_ref, nw_ref,
                y_ref, xbuf, a_s, v_s, h_s, hcarry):
    t = T_LRU
    j = pl.program_id(1)

    @pl.when(j == 0)
    def _():
        xbuf[0:SUBLANES, :] = jnp.zeros((SUBLANES, D_LRU), F32)
        hcarry[...] = jnp.zeros_like(hcarry)

    xbuf[SUBLANES:SUBLANES + t, :] = lx_ref[...].astype(F32)
    u = _causal_conv(xbuf, cw_ref, cb_ref, t)
    xbuf[0:SUBLANES, :] = xbuf[t:t + SUBLANES, :]

    lam = lam_ref[...]
    log_sig = jnp.minimum(lam, 0.0) - _log1p(jnp.exp(-jnp.abs(lam)))
    rate = (RG_C * LOG2E) * log_sig
    for g in range(D_LRU // GATE_W):
        sl = slice(g * GATE_W, (g + 1) * GATE_W)
        ug = u[:, sl]
        gates = jnp.dot(ug.astype(BF16), wg_ref[g], preferred_element_type=F32)
        r = _sigmoid(gates[:, :GATE_W] + ba_ref[:, sl])
        i = _sigmoid(gates[:, GATE_W:] + bx_ref[:, sl])
        a = jnp.exp2(r * rate[:, sl])
        mult = jnp.sqrt(1.0 - a * a)
        a_s[:, sl] = a
        v_s[:, sl] = mult * (i * ug)

    row = lax.broadcasted_iota(jnp.int32, (SUBLANES, D_LRU), 0)

    def scan_body(k, hprev):
        r0 = pl.multiple_of(k * SUBLANES, SUBLANES)
        a = a_s[pl.ds(r0, SUBLANES), :]
        v = v_s[pl.ds(r0, SUBLANES), :]
        for s in (1, 2, 4):
            keep = row >= s
            a_sh = jnp.where(keep, pltpu.roll(a, s, 0), 1.0)
            v_sh = jnp.where(keep, pltpu.roll(v, s, 0), 0.0)
            v = v + a * v_sh
            a = a * a_sh
        h = v + a * hprev
        h_s[pl.ds(r0, SUBLANES), :] = h
        return h[SUBLANES - 1:SUBLANES, :]

    hcarry[...] = lax.fori_loop(0, t // SUBLANES, scan_body, hcarry[...], unroll=4)

    g_in = lg_ref[...].astype(F32)
    inner = g_in * (0.7978845608028654 + 0.035677408136300125 * (g_in * g_in))
    half_g = 0.5 * g_in
    gelu = half_g + half_g * jnp.tanh(inner)
    y_ref[...] = _rms(gelu * h_s[...], nw_ref[...]).astype(BF16)


def _lru(lx, lg, cw, cb, wg, ba, bx, lam, nw, bsz, seq):
    t = T_LRU
    nj = seq // t
    row = pl.BlockSpec((t, D_LRU), lambda b, j: (b * nj + j, 0))
    vec = _const_spec((1, D_LRU))
    return pl.pallas_call(
        _lru_kernel,
        grid=(bsz, nj),
        in_specs=[row, row, _const_spec(cw.shape), vec, _const_spec(wg.shape),
                  vec, vec, vec, vec],
        out_specs=row,
        out_shape=jax.ShapeDtypeStruct((bsz * seq, D_LRU), BF16),
        scratch_shapes=[pltpu.VMEM((t + SUBLANES, D_LRU), F32),
                        pltpu.VMEM((t, D_LRU), F32),
                        pltpu.VMEM((t, D_LRU), F32),
                        pltpu.VMEM((t, D_LRU), F32),
                        pltpu.VMEM((1, D_LRU), F32)],
        compiler_params=_cparams(("arbitrary", "arbitrary")),
        name="rglru",
    )(lx, lg, cw, cb, wg, ba, bx, lam, nw)


def _split3(x):
    hi = x.astype(BF16)
    r1 = x - hi.astype(F32)
    mid = r1.astype(BF16)
    lo = (r1 - mid.astype(F32)).astype(BF16)
    return hi, mid, lo


def _ssd_kernel(xbc_ref, z_ref, dt_ref, cw_ref, cb_ref, dtb_ref, alog_ref, dvec_ref, nw_ref,
                tril_ref, fut_ref, y_ref, xbuf, xc_s, state, y_s):
    tt = xbc_ref.shape[0]
    j = pl.program_id(1)

    @pl.when(j == 0)
    def _():
        xbuf[0:SUBLANES, :] = jnp.zeros((SUBLANES, xbuf.shape[1]), F32)
        state[...] = jnp.zeros_like(state)

    xbuf[SUBLANES:SUBLANES + tt, :] = xbc_ref[...].astype(F32)
    u = _causal_conv(xbuf, cw_ref, cb_ref, tt)
    xbuf[0:SUBLANES, :] = xbuf[tt:tt + SUBLANES, :]
    xc_s[...] = u * _sigmoid(u)
    for c in range(tt // SSD_CHUNK):
        _ssd_chunk(c, xc_s, dt_ref, dtb_ref, alog_ref, tril_ref, fut_ref, state, y_s)

    zf = z_ref[...].astype(F32)
    y = (y_s[...] + xc_s[:, 0:D_SSD] * dvec_ref[...]) * (zf * _sigmoid(zf))
    y_ref[...] = _rms(y, nw_ref[...]).astype(BF16)


def _ssd_chunk(c, xc_s, dt_ref, dtb_ref, alog_ref, tril_ref, fut_ref, state, y_s):
    t = SSD_CHUNK
    rows = slice(c * t, (c + 1) * t)
    gn = SSD_GROUPS * SSD_STATE
    xs = xc_s[rows, 0:D_SSD]
    bm = xc_s[rows, D_SSD:D_SSD + gn]
    cm = xc_s[rows, D_SSD + gn:]

    dt = _softplus(dt_ref[rows, :] + dtb_ref[...])
    d_a = dt * (-jnp.exp(alog_ref[...]))
    tril = tril_ref[...]
    future = fut_ref[...]
    hi, mid, lo = _split3(d_a)
    a_cs = (jnp.dot(tril, hi, preferred_element_type=F32)
            + jnp.dot(tril, mid, preferred_element_type=F32)
            + jnp.dot(tril, lo, preferred_element_type=F32))
    a_cs_t = a_cs.T
    dt_t = dt.T
    a_last_t = a_cs_t[:, t - 1:t]
    w_state_t = jnp.exp(a_last_t - a_cs_t) * dt_t
    chunk_decay_t = jnp.exp(a_last_t)
    src_t = a_cs_t - jnp.log(dt_t)

    lane = lax.broadcasted_iota(jnp.int32, (1, LANES), 1)
    first = lane < SSD_HEAD_DIM

    heads_per_group = SSD_HEADS // SSD_GROUPS
    for g in range(SSD_GROUPS):
        gsl = slice(g * SSD_STATE, (g + 1) * SSD_STATE)
        c_g = cm[:, gsl].astype(BF16)
        b_g = bm[:, gsl]
        scores = lax.dot_general(c_g, b_g.astype(BF16), (((1,), (1,)), ((), ())),
                                 preferred_element_type=F32)
        b_t = b_g.T
        st_g = state[g]
        y_off = jnp.dot(c_g, st_g.astype(BF16), preferred_element_type=F32)
        for q in range(heads_per_group // 2):
            h0 = g * heads_per_group + 2 * q
            psl = slice(h0 * SSD_HEAD_DIM, (h0 + 2) * SSD_HEAD_DIM)
            lsl = slice(2 * q * SSD_HEAD_DIM, (2 * q + 2) * SSD_HEAD_DIM)
            x_pair = xs[:, psl].astype(BF16)
            yd, ns, cols, cd = [], [], [], []
            for h in (h0, h0 + 1):
                col = jnp.broadcast_to(a_cs[:, h:h + 1], (t, t))
                lmat_dt = jnp.exp(col - src_t[h:h + 1, :] + future)
                m = (scores * lmat_dt).astype(BF16)
                yd.append(jnp.dot(m, x_pair, preferred_element_type=F32))
                bw = (b_t * w_state_t[h:h + 1, :]).astype(BF16)
                ns.append(jnp.dot(bw, x_pair, preferred_element_type=F32))
                cols.append(col)
                cd.append(jnp.broadcast_to(chunk_decay_t[h:h + 1, :], (1, LANES)))
            y_pair = (jnp.where(first, yd[0], yd[1])
                      + jnp.exp(jnp.where(first, cols[0], cols[1])) * y_off[:, lsl])
            y_s[rows, psl] = y_pair
            state[g, :, lsl] = (st_g[:, lsl] * jnp.where(first, cd[0], cd[1])
                                + jnp.where(first, ns[0], ns[1]))


def _ssd(xbc, z, dt, cw, cb, dtb, alog, dvec, nw, bsz, seq):
    t = T_SSD
    nj = seq // t
    dx = xbc.shape[1]
    row = lambda w: pl.BlockSpec((t, w), lambda b, j: (b * nj + j, 0))
    ch = SSD_CHUNK
    causal = jnp.arange(ch)[:, None] >= jnp.arange(ch)[None, :]
    return pl.pallas_call(
        _ssd_kernel,
        grid=(bsz, nj),
        in_specs=[row(dx), row(D_SSD), row(LANES), _const_spec(cw.shape), _const_spec((1, dx)),
                  _const_spec((1, LANES)), _const_spec((1, LANES)),
                  _const_spec((1, D_SSD)), _const_spec((1, D_SSD)),
                  _const_spec((ch, ch)), _const_spec((ch, ch))],
        out_specs=row(D_SSD),
        out_shape=jax.ShapeDtypeStruct((bsz * seq, D_SSD), BF16),
        scratch_shapes=[pltpu.VMEM((t + SUBLANES, dx), F32),
                        pltpu.VMEM((t, dx), F32),
                        pltpu.VMEM((SSD_GROUPS, SSD_STATE, D_SSD // SSD_GROUPS), F32),
                        pltpu.VMEM((t, D_SSD), F32)],
        compiler_params=_cparams(("arbitrary", "arbitrary")),
        name="ssd",
    )(xbc, z, dt, cw, cb, dtb, alog, dvec, nw, causal.astype(BF16),
      jnp.where(causal, 0.0, NEG_BIG).astype(F32))


ROUTE_E, ROUTE_W, ROUTE_R = 0, 2, 4
NEG_BIG = -1e30


def _first_argmax(vals, lane_f):
    m = jnp.max(vals, axis=-1, keepdims=True)
    idx = jnp.min(jnp.where(vals == m, lane_f, float(LANES)), axis=-1, keepdims=True)
    return m, idx


def _outproj_kernel(yl_ref, ys_ref, x_ref, wo_ref, nw_ref, wr_ref, rb_ref,
                    x1_ref, hp_ref, route_ref, cnt_ref, stril, running):
    @pl.when(pl.program_id(0) == 0)
    def _():
        ri = lax.broadcasted_iota(jnp.int32, (TM_SUB, TM_SUB), 0)
        ci = lax.broadcasted_iota(jnp.int32, (TM_SUB, TM_SUB), 1)
        stril[...] = (ri > ci).astype(BF16)
        running[...] = jnp.zeros_like(running)

    for s in range(x_ref.shape[0] // TM_SUB):
        _outproj_rows(pl.ds(s * TM_SUB, TM_SUB), yl_ref, ys_ref, x_ref, wo_ref, nw_ref, wr_ref, rb_ref,
                      x1_ref, hp_ref, route_ref, stril, running)
    cnt_ref[...] = running[...]


def _outproj_rows(rows, yl_ref, ys_ref, x_ref, wo_ref, nw_ref, wr_ref, rb_ref,
                  x1_ref, hp_ref, route_ref, stril, running):
    tm = TM_SUB
    y_cat = jnp.concatenate([yl_ref[rows, :], ys_ref[rows, :]], axis=1)
    x1 = x_ref[rows, :] + jnp.dot(y_cat, wo_ref[...], preferred_element_type=F32)
    x1_ref[rows, :] = x1
    h = _rms(x1, nw_ref[...])
    h_hi = h.astype(BF16)
    h_rt = h_hi.astype(F32)
    bits = lax.bitcast_convert_type(h_rt, U32)
    hp_ref[rows, :] = bits[:, D_HALF:] | (bits[:, :D_HALF] >> 16)
    logits = jnp.dot(h_hi, wr_ref[...], preferred_element_type=F32) + rb_ref[...]

    lane = lax.broadcasted_iota(jnp.int32, (tm, LANES), 1)
    lane_f = lane.astype(F32)
    is_c = lane < MOE_GROUPS
    lc = jnp.where(is_c, logits, NEG_BIG)
    m_c, g_idx = _first_argmax(lc, lane_f)
    g_w = 1.0 / jnp.sum(jnp.where(is_c, jnp.exp(lc - m_c), 0.0), axis=-1, keepdims=True)
    lo = float(MOE_GROUPS) + float(EXPERTS_PER_GROUP) * g_idx
    is_f = (lane_f >= lo) & (lane_f < lo + float(EXPERTS_PER_GROUP))
    lf = jnp.where(is_f, logits, NEG_BIG)
    v1, i1 = _first_argmax(lf, lane_f)
    v2, i2 = _first_argmax(jnp.where(lane_f == i1, NEG_BIG, lf), lane_f)
    ex = jnp.exp(v2 - v1)
    w1 = g_w / (1.0 + ex)
    w2 = g_w * ex / (1.0 + ex)

    oh1 = (lane_f == i1).astype(F32)
    oh2 = (lane_f == i2).astype(F32)
    both = oh1 + oh2
    before = jnp.dot(stril[...], both.astype(BF16), preferred_element_type=F32) + running[...]
    r1 = jnp.sum(oh1 * before, axis=-1, keepdims=True)
    r2 = jnp.sum(oh2 * before, axis=-1, keepdims=True)
    running[...] = running[...] + jnp.sum(both, axis=0, keepdims=True)

    e1 = i1 - float(MOE_GROUPS)
    e2 = i2 - float(MOE_GROUPS)
    route = jnp.zeros((tm, LANES), F32)
    for off, (a, b) in ((ROUTE_E, (e1, e2)), (ROUTE_W, (w1, w2)), (ROUTE_R, (r1, r2))):
        route = jnp.where(lane == off, a, jnp.where(lane == off + 1, b, route))
    route_ref[rows, :] = route


def _outproj(layer, yl, ys, x, wo, nw, wr, rb):
    n = x.shape[0]
    tm = TM_PROJ
    row = lambda w: pl.BlockSpec((tm, w), lambda i: (i, 0))
    return pl.pallas_call(
        _outproj_kernel,
        grid=(n // tm,),
        in_specs=[row(D_LRU), row(D_SSD), row(D_MODEL),
                  pl.BlockSpec((None,) + wo.shape[1:], lambda i: (layer, 0, 0)),
                  _const_spec((1, D_MODEL)),
                  _const_spec(wr.shape), _const_spec((1, LANES))],
        out_specs=[row(D_MODEL), row(D_HALF), row(LANES), _const_spec((1, LANES))],
        out_shape=[jax.ShapeDtypeStruct((n, D_MODEL), F32),
                   jax.ShapeDtypeStruct((n, D_HALF), U32),
                   jax.ShapeDtypeStruct((n, LANES), F32),
                   jax.ShapeDtypeStruct((1, LANES), F32)],
        scratch_shapes=[pltpu.VMEM((TM_SUB, TM_SUB), BF16), pltpu.VMEM((1, LANES), F32)],
        compiler_params=_cparams(("arbitrary",)),
        name="outproj",
    )(yl, ys, x, wo, nw, wr, rb)


SCHED_EXPERT, SCHED_FIRST, SCHED_NEXT, SCHED_SLOT = 0, 1, 2, 3


def _expert_kernel(sch_ref, nu_ref, rt_ref, rtn_ref, hp_hbm, wg_hbm, wu_hbm, wd_hbm, yb_ref,
                   hp_v, xs0, xs1, wg_f, wu_f, wd_f, wg_s, wu_s, wd_s, sem, wsem, *, layer):
    j = pl.program_id(0)
    out_rows = TM_EXP * ROW_TILES

    def weight_copies(expert, s):
        return (pltpu.make_async_copy(wg_hbm.at[layer, expert], wg_f.at[s], wsem.at[s, 0]),
                pltpu.make_async_copy(wu_hbm.at[layer, expert], wu_f.at[s], wsem.at[s, 1]),
                pltpu.make_async_copy(wd_hbm.at[layer, expert], wd_f.at[s], wsem.at[s, 2]))

    def gather(idx_ref, row, dst):
        for r in range(TM_EXP):
            dst[r:r + 1, :] = hp_v[pl.ds(idx_ref[row, r], 1), :]

    @pl.when(j == 0)
    def _():
        for cp in weight_copies(sch_ref[SCHED_EXPERT, 0], sch_ref[SCHED_SLOT, 0]):
            cp.start()
        cp = pltpu.make_async_copy(hp_hbm, hp_v, sem)
        cp.start()
        cp.wait()
        gather(rt_ref, 0, xs0)

    def block(half, src, prefetch):
        i = 2 * j + half
        e = sch_ref[SCHED_EXPERT, i]
        slot = sch_ref[SCHED_SLOT, i]
        active = i < nu_ref[0]
        out = pl.ds(half * out_rows, out_rows)

        @pl.when(active & (sch_ref[SCHED_FIRST, i] == 1))
        def _():
            for cp in weight_copies(e, slot):
                cp.wait()
            wg_s[...] = wg_f[slot].astype(BF16)
            wu_s[...] = wu_f[slot].astype(BF16)
            wd_s[...] = wd_f[slot].astype(BF16)
            nxt = sch_ref[SCHED_NEXT, i]

            @pl.when(nxt != e)
            def _():
                for cp in weight_copies(nxt, 1 - slot):
                    cp.start()

        @pl.when(active)
        def _():
            prefetch()
            packed = src[...]
            x_lo = lax.bitcast_convert_type(packed << 16, F32).astype(BF16)
            x_hi = lax.bitcast_convert_type((packed >> 16) << 16, F32).astype(BF16)
            gate = (jnp.dot(x_lo, wg_s[0:D_HALF, :], preferred_element_type=F32)
                    + jnp.dot(x_hi, wg_s[D_HALF:, :], preferred_element_type=F32))
            up = (jnp.dot(x_lo, wu_s[0:D_HALF, :], preferred_element_type=F32)
                  + jnp.dot(x_hi, wu_s[D_HALF:, :], preferred_element_type=F32))
            hid = (gate * _sigmoid(gate) * up).astype(BF16)
            y = jnp.dot(hid, wd_s[...], preferred_element_type=F32)
            for c in range(ROW_TILES):
                yb_ref[pl.ds(half * out_rows + c, TM_EXP, stride=ROW_TILES), :] = (
                    y[:, c * LANES:(c + 1) * LANES])

        @pl.when(jnp.logical_not(active))
        def _():
            yb_ref[out, :] = jnp.zeros((out_rows, LANES), F32)

    block(0, xs0, lambda: gather(rt_ref, 1, xs1))
    block(1, xs1, lambda: gather(rtn_ref, 0, xs0))


def _experts(layer, sched, n_used, row_tok, hp, wg, wu, wd):
    n_blocks = sched.shape[1]
    tm = TM_EXP
    n_steps = n_blocks // 2
    hbm = pl.BlockSpec(memory_space=pl.ANY)
    grid_spec = pltpu.PrefetchScalarGridSpec(
        num_scalar_prefetch=2,
        grid=(n_steps,),
        in_specs=[pl.BlockSpec((None, 2, tm), lambda j, sch, nu: (j, 0, 0), memory_space=pltpu.SMEM),
                  pl.BlockSpec((None, 2, tm), lambda j, sch, nu: (jnp.minimum(j + 1, n_steps - 1), 0, 0),
                               memory_space=pltpu.SMEM),
                  hbm, hbm, hbm, hbm],
        out_specs=pl.BlockSpec((2 * tm * ROW_TILES, LANES), lambda j, sch, nu: (j, 0)),
        scratch_shapes=[pltpu.VMEM(hp.shape, U32),
                        pltpu.VMEM((tm, D_HALF), U32),
                        pltpu.VMEM((tm, D_HALF), U32),
                        pltpu.VMEM((2, D_MODEL, D_EXPERT), F32),
                        pltpu.VMEM((2, D_MODEL, D_EXPERT), F32),
                        pltpu.VMEM((2, D_EXPERT, D_MODEL), F32),
                        pltpu.VMEM((D_MODEL, D_EXPERT), BF16),
                        pltpu.VMEM((D_MODEL, D_EXPERT), BF16),
                        pltpu.VMEM((D_EXPERT, D_MODEL), BF16),
                        pltpu.SemaphoreType.DMA(()),
                        pltpu.SemaphoreType.DMA((2, 3))],
    )
    return pl.pallas_call(
        functools.partial(_expert_kernel, layer=layer),
        grid_spec=grid_spec,
        out_shape=jax.ShapeDtypeStruct((n_blocks * tm * ROW_TILES, LANES), F32),
        compiler_params=_cparams(("arbitrary",)),
        name="experts",
    )(sched, n_used, row_tok.reshape(n_steps, 2, tm), row_tok.reshape(n_steps, 2, tm), hp, wg, wu, wd)


def _combine_kernel(dcur_ref, dnxt_ref, x_ref, w_ref, nw_ref, yb_ref, o_ref, gbuf, sems,
                    *, final_norm):
    t = T_TOK
    i = pl.program_id(0)
    n_steps = pl.num_programs(0)
    slot = i % 2

    def copy(dref, s, k, tok):
        src = pl.ds(pl.multiple_of(dref[k, tok] * ROW_TILES, ROW_TILES), ROW_TILES)
        row0 = tok * ROW_TILES
        dst = pl.ds(row0 if isinstance(tok, int) else pl.multiple_of(row0, ROW_TILES), ROW_TILES)
        return pltpu.make_async_copy(yb_ref.at[src, :], gbuf.at[s, k, dst, :], sems.at[s])

    def issue(dref, s):
        def body(tok, c):
            for k in range(TOP_K):
                copy(dref, s, k, tok).start()
            return c
        lax.fori_loop(0, t, body, 0, unroll=8)

    @pl.when(i == 0)
    def _():
        issue(dcur_ref, 0)

    @pl.when(i + 1 < n_steps)
    def _():
        for tok in range(t):
            for k in range(TOP_K):
                copy(dnxt_ref, 1 - slot, k, tok).start(priority=k)

    def wait(tok, c):
        for k in range(TOP_K):
            copy(dcur_ref, slot, k, tok).wait()
        return c

    lax.fori_loop(0, t, wait, 0, unroll=8)

    w0 = w_ref[:, ROUTE_W:ROUTE_W + 1]
    w1 = w_ref[:, ROUTE_W + 1:ROUTE_W + 2]
    parts = []
    for c in range(ROW_TILES):
        g0 = gbuf[slot, 0, pl.ds(c, t, stride=ROW_TILES), :]
        g1 = gbuf[slot, 1, pl.ds(c, t, stride=ROW_TILES), :]
        parts.append(x_ref[:, c * LANES:(c + 1) * LANES] + w0 * g0 + w1 * g1)
    out = jnp.concatenate(parts, axis=-1)
    if final_norm:
        out = _rms(out, nw_ref[...])
    o_ref[...] = out


def _combine(dest3, x, route, nw, yb, final_norm):
    n = x.shape[0]
    t = T_TOK
    n_steps = n // t
    dspec = lambda f: pl.BlockSpec((None, TOP_K, t), lambda i: (f(i), 0, 0), memory_space=pltpu.SMEM)
    return pl.pallas_call(
        functools.partial(_combine_kernel, final_norm=final_norm),
        grid=(n_steps,),
        in_specs=[dspec(lambda i: i), dspec(lambda i: jnp.minimum(i + 1, n_steps - 1)),
                  pl.BlockSpec((t, D_MODEL), lambda i: (i, 0)),
                  pl.BlockSpec((t, LANES), lambda i: (i, 0)),
                  _const_spec((1, D_MODEL)),
                  pl.BlockSpec(memory_space=pl.ANY)],
        out_specs=pl.BlockSpec((t, D_MODEL), lambda i: (i, 0)),
        out_shape=jax.ShapeDtypeStruct((n, D_MODEL), F32),
        scratch_shapes=[pltpu.VMEM((2, TOP_K, t * ROW_TILES, LANES), F32),
                        pltpu.SemaphoreType.DMA((2,))],
        compiler_params=_cparams(("arbitrary",)),
        name="combine",
    )(dest3, dest3, x, route, nw, yb)


T_SLOT = 1024


def _slots_kernel(route_ref, ps_ref, dest_ref):
    route = route_ref[...]
    lane_f = lax.broadcasted_iota(jnp.int32, route.shape, 1).astype(F32)
    ps = ps_ref[...]
    slots = []
    for k in range(TOP_K):
        e = route[:, ROUTE_E + k:ROUTE_E + k + 1] + float(MOE_GROUPS)
        base = jnp.sum(jnp.where(lane_f == e, ps, 0.0), axis=-1, keepdims=True)
        slots.append(base + route[:, ROUTE_R + k:ROUTE_R + k + 1])
    slab = jnp.where(lane_f == 0.0, slots[0], jnp.where(lane_f == 1.0, slots[1], 0.0))
    for i in range(T_SLOT // T_TOK):
        part = slab[i * T_TOK:(i + 1) * T_TOK, :].T
        dest_ref[i] = part[0:TOP_K, :].astype(jnp.int32)


def _slots(route, pad_start):
    n = route.shape[0]
    ps = jnp.pad(pad_start.astype(F32), (MOE_GROUPS, LANES - MOE_GROUPS - N_EXPERTS)).reshape(1, LANES)
    per = T_SLOT // T_TOK
    return pl.pallas_call(
        _slots_kernel,
        grid=(n // T_SLOT,),
        in_specs=[pl.BlockSpec((T_SLOT, LANES), lambda i: (i, 0)), _const_spec((1, LANES))],
        out_specs=pl.BlockSpec((per, TOP_K, T_TOK), lambda i: (i, 0, 0)),
        out_shape=jax.ShapeDtypeStruct((n // T_TOK, TOP_K, T_TOK), jnp.int32),
        compiler_params=_cparams(("arbitrary",)),
        name="slots",
    )(route, ps)


def _route(route, cnt, n_blocks):
    n = route.shape[0]
    counts = cnt[0, MOE_GROUPS:MOE_GROUPS + N_EXPERTS].astype(jnp.int32)
    padded = (counts + TM_EXP - 1) // TM_EXP * TM_EXP
    pad_end = jnp.cumsum(padded)
    pad_start = pad_end - padded
    dest3 = _slots(route, pad_start)
    n_used = (pad_end[-1:] // TM_EXP).astype(jnp.int32)
    blk = jnp.arange(n_blocks, dtype=jnp.int32)
    blk_expert = jnp.minimum(jnp.sum(pad_end[None, :] <= (blk * TM_EXP)[:, None], axis=1),
                             N_EXPERTS - 1).astype(jnp.int32)
    eid = jnp.arange(N_EXPERTS, dtype=jnp.int32)
    nonempty = counts > 0
    later = (eid[None, :] > eid[:, None]) & nonempty[None, :]
    next_used = jnp.min(jnp.where(later, eid[None, :], N_EXPERTS), axis=1)
    next_used = jnp.where(next_used == N_EXPERTS, eid, next_used)
    ordinal = jnp.cumsum(nonempty.astype(jnp.int32)) - 1
    first = ((blk * TM_EXP == pad_start[blk_expert]) & (blk < n_used[0])).astype(jnp.int32)
    sched = jnp.stack([blk_expert, first, next_used[blk_expert], ordinal[blk_expert] % 2])
    token = jnp.arange(n, dtype=jnp.int32).reshape(n // T_TOK, 1, T_TOK)
    token = jnp.broadcast_to(token, dest3.shape)
    row_tok = jnp.zeros((n_blocks * TM_EXP,), jnp.int32).at[dest3.reshape(-1)].add(
        token.reshape(-1), unique_indices=True)
    return dest3, sched.astype(jnp.int32), n_used, row_tok


def _gate_blocks(wa, wx):
    per = GATE_W // LRU_BW
    eye = jnp.eye(per, dtype=F32)

    def bd(w):
        w = w.reshape(LRU_HEADS // per, per, LRU_BW, LRU_BW)
        full = jnp.einsum('gpij,pq->gpiqj', w, eye)
        return full.reshape(LRU_HEADS // per, GATE_W, GATE_W)

    return jnp.concatenate([bd(wa), bd(wx)], axis=-1).astype(BF16)


def kernel(x, norm_mix, w_in, lru_conv_w, lru_conv_b, lru_wa, lru_ba, lru_wx, lru_bx, lru_lambda, lru_norm, ssd_conv_w, ssd_conv_b, ssd_dt_bias, ssd_a_log, ssd_d, ssd_norm, w_out, norm_ffn, w_coarse, b_coarse, w_fine, b_fine, w_gate, w_up, w_down, final_norm):
    bsz, seq, d = x.shape
    n = bsz * seq
    depth = w_in.shape[0]
    n_assign = n * TOP_K
    n_blocks = -(-(n_assign + N_EXPERTS * (TM_EXP - 1)) // TM_EXP)
    n_blocks += n_blocks % 2
    o_dt = 2 * D_LRU + D_SSD + SSD_XBC

    lane_pad = lambda v: jnp.pad(v, ((0, 0), (0, LANES - v.shape[1])))[:, None, :]
    row_pad = lambda w: jnp.pad(w, ((0, 0), (0, SUBLANES - w.shape[1]), (0, 0)))
    w_in_bf = w_in[:, :, :o_dt].astype(BF16)
    wdt_all = jnp.pad(w_in[:, :, o_dt:].astype(BF16), ((0, 0), (0, 0), (0, LANES - SSD_HEADS)))
    w_out_bf = w_out.astype(BF16)
    gates_all = jax.vmap(_gate_blocks)(lru_wa, lru_wx)
    lru_cw, ssd_cw = row_pad(lru_conv_w), row_pad(ssd_conv_w)
    dtb_all, alog_all = lane_pad(ssd_dt_bias), lane_pad(ssd_a_log)
    dvec_all = jnp.repeat(ssd_d, SSD_HEAD_DIM, axis=1)[:, None, :]
    w_r = jnp.concatenate([w_coarse, w_fine.transpose(0, 2, 1, 3).reshape(depth, d, N_EXPERTS)], axis=2)
    w_r_all = jnp.pad(w_r, ((0, 0), (0, 0), (0, LANES - w_r.shape[2]))).astype(BF16)
    r_bias_all = lane_pad(jnp.concatenate([b_coarse, b_fine.reshape(depth, -1)], axis=1))
    vec = lambda p, i: p[i].reshape(1, -1)

    xt = x.reshape(n, d)
    for i in range(depth):
        lx, lg, z, xbc, dt = _inproj(i, xt, vec(norm_mix, i), w_in_bf, wdt_all[i])
        y_lru = _lru(lx, lg, lru_cw[i], vec(lru_conv_b, i), gates_all[i], vec(lru_ba, i),
                     vec(lru_bx, i), vec(lru_lambda, i), vec(lru_norm, i), bsz, seq)
        y_ssd = _ssd(xbc, z, dt, ssd_cw[i], vec(ssd_conv_b, i), dtb_all[i], alog_all[i],
                     dvec_all[i], vec(ssd_norm, i), bsz, seq)
        x1, hp, route, cnt = _outproj(i, y_lru, y_ssd, xt, w_out_bf, vec(norm_ffn, i),
                                      w_r_all[i], r_bias_all[i])
        dest3, sched, n_used, row_tok = _route(route, cnt, n_blocks)
        yb = _experts(i, sched, n_used, row_tok, hp, w_gate, w_up, w_down)
        xt = _combine(dest3, x1, route, final_norm.reshape(1, d), yb, final_norm=(i == depth - 1))
    return xt.reshape(bsz, seq, d)
```

```python
import functools

import jax
import jax.numpy as jnp
from jax import lax
from jax.experimental import pallas as pl
from jax.experimental.pallas import tpu as pltpu

F32 = jnp.float32
BF16 = jnp.bfloat16
U32 = jnp.uint32

D_MODEL = 1024
D_LRU = 1024
LRU_HEADS = 16
LRU_BW = 64
RG_C = 8.0
CONV_K = 4
D_SSD = 1024
SSD_HEAD_DIM = 64
SSD_HEADS = 16
SSD_GROUPS = 4
SSD_STATE = 128
SSD_CHUNK = 128
MOE_GROUPS = 4
EXPERTS_PER_GROUP = 8
N_EXPERTS = 32
TOP_K = 2
D_EXPERT = 512
EPS = 1e-6
LOG2E = 1.4426950408889634

LANES = 128
SUBLANES = 8
VMEM_LIMIT = 60 * 1024 * 1024

TM_PROJ = 512
TM_SUB = 256
T_LRU = 512
T_SSD = 512
TM_EXP = 256
T_TOK = 256
GATE_W = 256
D_HALF = D_MODEL // 2
ROW_TILES = D_MODEL // LANES


def _cparams(sem):
    return pltpu.CompilerParams(dimension_semantics=sem, vmem_limit_bytes=VMEM_LIMIT)


def _const_spec(shape):
    n = len(shape)
    return pl.BlockSpec(shape, lambda *_: (0,) * n)


def _sigmoid(x):
    return 1.0 / (1.0 + jnp.exp(-x))


def _log1p(e):
    u = 1.0 + e
    d = u - 1.0
    return jnp.where(d == 0.0, e, jnp.log(u) * (e / jnp.where(d == 0.0, 1.0, d)))


def _softplus(x):
    return jnp.maximum(x, 0.0) + _log1p(jnp.exp(-jnp.abs(x)))


def _rms(x, w):
    ms = jnp.mean(x * x, axis=-1, keepdims=True)
    return x * lax.rsqrt(ms + EPS) * w


W_COL = 1024
SSD_XBC = D_SSD + 2 * SSD_GROUPS * SSD_STATE


def _inproj_kernel(x_ref, nw_ref, wlx_ref, wlg_ref, wz_ref, wx_ref, wbc_ref, wdt_ref,
                   lx_ref, lg_ref, z_ref, xbc_ref, dt_ref):
    h = _rms(x_ref[...], nw_ref[...]).astype(BF16)
    lx_ref[...] = jnp.dot(h, wlx_ref[...], preferred_element_type=F32).astype(BF16)
    lg_ref[...] = jnp.dot(h, wlg_ref[...], preferred_element_type=F32).astype(BF16)
    z_ref[...] = jnp.dot(h, wz_ref[...], preferred_element_type=F32).astype(BF16)
    xbc_ref[:, 0:W_COL] = jnp.dot(h, wx_ref[...], preferred_element_type=F32).astype(BF16)
    xbc_ref[:, W_COL:] = jnp.dot(h, wbc_ref[...], preferred_element_type=F32).astype(BF16)
    dt_ref[...] = jnp.dot(h, wdt_ref[...], preferred_element_type=F32)


def _inproj(layer, x, nw, w_in, wdt):
    n = x.shape[0]
    tm = TM_PROJ
    row = lambda w: pl.BlockSpec((tm, w), lambda i: (i, 0))
    wcol = lambda c: pl.BlockSpec((None, D_MODEL, W_COL), lambda i: (layer, 0, c))
    return pl.pallas_call(
        _inproj_kernel,
        grid=(n // tm,),
        in_specs=[row(D_MODEL), _const_spec((1, D_MODEL)),
                  wcol(0), wcol(1), wcol(2), wcol(3), wcol(4), _const_spec(wdt.shape)],
        out_specs=[row(D_LRU), row(D_LRU), row(D_SSD), row(SSD_XBC), row(LANES)],
        out_shape=[jax.ShapeDtypeStruct((n, D_LRU), BF16),
                   jax.ShapeDtypeStruct((n, D_LRU), BF16),
                   jax.ShapeDtypeStruct((n, D_SSD), BF16),
                   jax.ShapeDtypeStruct((n, SSD_XBC), BF16),
                   jax.ShapeDtypeStruct((n, LANES), F32)],
        compiler_params=_cparams(("arbitrary",)),
        name="inproj",
    )(x, nw, w_in, w_in, w_in, w_in, w_in, wdt)


def _causal_conv(xbuf, cw_ref, cb_ref, t):
    cw = cw_ref[...]
    n = t + SUBLANES
    full = xbuf[...]
    acc = cb_ref[...] + cw[CONV_K - 1:CONV_K, :] * full[SUBLANES:, :]
    for k in range(CONV_K - 1):
        shifted = pltpu.roll(full, n - (SUBLANES - 3 + k), 0)[0:t, :]
        acc = acc + cw[k:k + 1, :] * shifted
    return acc


def _lru_kernel(lx_ref, lg_ref, cw_ref, cb_ref, wg_ref, ba_ref, bx_ref, lam_ref, nw_ref,
                y_ref, xbuf, a_s, v_s, h_s, hcarry):
    t = T_LRU
    j = pl.program_id(1)

    @pl.when(j == 0)
    def _():
        xbuf[0:SUBLANES, :] = jnp.zeros((SUBLANES, D_LRU), F32)
        hcarry[...] = jnp.zeros_like(hcarry)

    xbuf[SUBLANES:SUBLANES + t, :] = lx_ref[...].astype(F32)
    u = _causal_conv(xbuf, cw_ref, cb_ref, t)
    xbuf[0:SUBLANES, :] = xbuf[t:t + SUBLANES, :]

    lam = lam_ref[...]
    log_sig = jnp.minimum(lam, 0.0) - _log1p(jnp.exp(-jnp.abs(lam)))
    rate = (RG_C * LOG2E) * log_sig
    for g in range(D_LRU // GATE_W):
        sl = slice(g * GATE_W, (g + 1) * GATE_W)
        ug = u[:, sl]
        gates = jnp.dot(ug.astype(BF16), wg_ref[g], preferred_element_type=F32)
        r = _sigmoid(gates[:, :GATE_W] + ba_ref[:, sl])
        i = _sigmoid(gates[:, GATE_W:] + bx_ref[:, sl])
        a = jnp.exp2(r * rate[:, sl])
        mult = jnp.sqrt(1.0 - a * a)
        a_s[:, sl] = a
        v_s[:, sl] = mult * (i * ug)

    row = lax.broadcasted_iota(jnp.int32, (SUBLANES, D_LRU), 0)

    def scan_body(k, hprev):
        r0 = pl.multiple_of(k * SUBLANES, SUBLANES)
        a = a_s[pl.ds(r0, SUBLANES), :]
        v = v_s[pl.ds(r0, SUBLANES), :]
        for s in (1, 2, 4):
            keep = row >= s
            a_sh = jnp.where(keep, pltpu.roll(a, s, 0), 1.0)
            v_sh = jnp.where(keep, pltpu.roll(v, s, 0), 0.0)
            v = v + a * v_sh
            a = a * a_sh
        h = v + a * hprev
        h_s[pl.ds(r0, SUBLANES), :] = h
        return h[SUBLANES - 1:SUBLANES, :]

    hcarry[...] = lax.fori_loop(0, t // SUBLANES, scan_body, hcarry[...], unroll=4)

    g_in = lg_ref[...].astype(F32)
    inner = g_in * (0.7978845608028654 + 0.035677408136300125 * (g_in * g_in))
    half_g = 0.5 * g_in
    gelu = half_g + half_g * jnp.tanh(inner)
    y_ref[...] = _rms(gelu * h_s[...], nw_ref[...]).astype(BF16)


def _lru(lx, lg, cw, cb, wg, ba, bx, lam, nw, bsz, seq):
    t = T_LRU
    nj = seq // t
    row = pl.BlockSpec((t, D_LRU), lambda b, j: (b * nj + j, 0))
    vec = _const_spec((1, D_LRU))
    return pl.pallas_call(
        _lru_kernel,
        grid=(bsz, nj),
        in_specs=[row, row, _const_spec(cw.shape), vec, _const_spec(wg.shape),
                  vec, vec, vec, vec],
        out_specs=row,
        out_shape=jax.ShapeDtypeStruct((bsz * seq, D_LRU), BF16),
        scratch_shapes=[pltpu.VMEM((t + SUBLANES, D_LRU), F32),
                        pltpu.VMEM((t, D_LRU), F32),
                        pltpu.VMEM((t, D_LRU), F32),
                        pltpu.VMEM((t, D_LRU), F32),
                        pltpu.VMEM((1, D_LRU), F32)],
        compiler_params=_cparams(("arbitrary", "arbitrary")),
        name="rglru",
    )(lx, lg, cw, cb, wg, ba, bx, lam, nw)


def _split3(x):
    hi = x.astype(BF16)
    r1 = x - hi.astype(F32)
    mid = r1.astype(BF16)
    lo = (r1 - mid.astype(F32)).astype(BF16)
    return hi, mid, lo


def _ssd_kernel(xbc_ref, z_ref, dt_ref, cw_ref, cb_ref, dtb_ref, alog_ref, dvec_ref, nw_ref,
                tril_ref, fut_ref, y_ref, xbuf, xc_s, state, y_s):
    tt = xbc_ref.shape[0]
    j = pl.program_id(1)

    @pl.when(j == 0)
    def _():
        xbuf[0:SUBLANES, :] = jnp.zeros((SUBLANES, xbuf.shape[1]), F32)
        state[...] = jnp.zeros_like(state)

    xbuf[SUBLANES:SUBLANES + tt, :] = xbc_ref[...].astype(F32)
    u = _causal_conv(xbuf, cw_ref, cb_ref, tt)
    xbuf[0:SUBLANES, :] = xbuf[tt:tt + SUBLANES, :]
    xc_s[...] = u * _sigmoid(u)
    for c in range(tt // SSD_CHUNK):
        _ssd_chunk(c, xc_s, dt_ref, dtb_ref, alog_ref, tril_ref, fut_ref, state, y_s)

    zf = z_ref[...].astype(F32)
    y = (y_s[...] + xc_s[:, 0:D_SSD] * dvec_ref[...]) * (zf * _sigmoid(zf))
    y_ref[...] = _rms(y, nw_ref[...]).astype(BF16)


def _ssd_chunk(c, xc_s, dt_ref, dtb_ref, alog_ref, tril_ref, fut_ref, state, y_s):
    t = SSD_CHUNK
    rows = slice(c * t, (c + 1) * t)
    gn = SSD_GROUPS * SSD_STATE
    xs = xc_s[rows, 0:D_SSD]
    bm = xc_s[rows, D_SSD:D_SSD + gn]
    cm = xc_s[rows, D_SSD + gn:]

    dt = _softplus(dt_ref[rows, :] + dtb_ref[...])
    d_a = dt * (-jnp.exp(alog_ref[...]))
    tril = tril_ref[...]
    future = fut_ref[...]
    hi, mid, lo = _split3(d_a)
    a_cs = (jnp.dot(tril, hi, preferred_element_type=F32)
            + jnp.dot(tril, mid, preferred_element_type=F32)
            + jnp.dot(tril, lo, preferred_element_type=F32))
    a_cs_t = a_cs.T
    dt_t = dt.T
    a_last_t = a_cs_t[:, t - 1:t]
    w_state_t = jnp.exp(a_last_t - a_cs_t) * dt_t
    chunk_decay_t = jnp.exp(a_last_t)
    src_t = a_cs_t - jnp.log(dt_t)

    lane = lax.broadcasted_iota(jnp.int32, (1, LANES), 1)
    first = lane < SSD_HEAD_DIM

    heads_per_group = SSD_HEADS // SSD_GROUPS
    for g in range(SSD_GROUPS):
        gsl = slice(g * SSD_STATE, (g + 1) * SSD_STATE)
        c_g = cm[:, gsl].astype(BF16)
        b_g = bm[:, gsl]
        scores = lax.dot_general(c_g, b_g.astype(BF16), (((1,), (1,)), ((), ())),
                                 preferred_element_type=F32)
        b_t = b_g.T
        st_g = state[g]
        y_off = jnp.dot(c_g, st_g.astype(BF16), preferred_element_type=F32)
        for q in range(heads_per_group // 2):
            h0 = g * heads_per_group + 2 * q
            psl = slice(h0 * SSD_HEAD_DIM, (h0 + 2) * SSD_HEAD_DIM)
            lsl = slice(2 * q * SSD_HEAD_DIM, (2 * q + 2) * SSD_HEAD_DIM)
            x_pair = xs[:, psl].astype(BF16)
            yd, ns, cols, cd = [], [], [], []
            for h in (h0, h0 + 1):
                col = jnp.broadcast_to(a_cs[:, h:h + 1], (t, t))
                lmat_dt = jnp.exp(col - src_t[h:h + 1, :] + future)
                m = (scores * lmat_dt).astype(BF16)
                yd.append(jnp.dot(m, x_pair, preferred_element_type=F32))
                bw = (b_t * w_state_t[h:h + 1, :]).astype(BF16)
                ns.append(jnp.dot(bw, x_pair, preferred_element_type=F32))
                cols.append(col)
                cd.append(jnp.broadcast_to(chunk_decay_t[h:h + 1, :], (1, LANES)))
            y_pair = (jnp.where(first, yd[0], yd[1])
                      + jnp.exp(jnp.where(first, cols[0], cols[1])) * y_off[:, lsl])
            y_s[rows, psl] = y_pair
            state[g, :, lsl] = (st_g[:, lsl] * jnp.where(first, cd[0], cd[1])
                                + jnp.where(first, ns[0], ns[1]))


def _ssd(xbc, z, dt, cw, cb, dtb, alog, dvec, nw, bsz, seq):
    t = T_SSD
    nj = seq // t
    dx = xbc.shape[1]
    row = lambda w: pl.BlockSpec((t, w), lambda b, j: (b * nj + j, 0))
    ch = SSD_CHUNK
    causal = jnp.arange(ch)[:, None] >= jnp.arange(ch)[None, :]
    return pl.pallas_call(
        _ssd_kernel,
        grid=(bsz, nj),
        in_specs=[row(dx), row(D_SSD), row(LANES), _const_spec(cw.shape), _const_spec((1, dx)),
                  _const_spec((1, LANES)), _const_spec((1, LANES)),
                  _const_spec((1, D_SSD)), _const_spec((1, D_SSD)),
                  _const_spec((ch, ch)), _const_spec((ch, ch))],
        out_specs=row(D_SSD),
        out_shape=jax.ShapeDtypeStruct((bsz * seq, D_SSD), BF16),
        scratch_shapes=[pltpu.VMEM((t + SUBLANES, dx), F32),
                        pltpu.VMEM((t, dx), F32),
                        pltpu.VMEM((SSD_GROUPS, SSD_STATE, D_SSD // SSD_GROUPS), F32),
                        pltpu.VMEM((t, D_SSD), F32)],
        compiler_params=_cparams(("arbitrary", "arbitrary")),
        name="ssd",
    )(xbc, z, dt, cw, cb, dtb, alog, dvec, nw, causal.astype(BF16),
      jnp.where(causal, 0.0, NEG_BIG).astype(F32))


ROUTE_E, ROUTE_W, ROUTE_R = 0, 2, 4
NEG_BIG = -1e30


def _first_argmax(vals, lane_f):
    m = jnp.max(vals, axis=-1, keepdims=True)
    idx = jnp.min(jnp.where(vals == m, lane_f, float(LANES)), axis=-1, keepdims=True)
    return m, idx


def _outproj_kernel(yl_ref, ys_ref, x_ref, wo_ref, nw_ref, wr_ref, rb_ref,
                    x1_ref, hp_ref, route_ref, cnt_ref, stril, running):
    @pl.when(pl.program_id(0) == 0)
    def _():
        ri = lax.broadcasted_iota(jnp.int32, (TM_SUB, TM_SUB), 0)
        ci = lax.broadcasted_iota(jnp.int32, (TM_SUB, TM_SUB), 1)
        stril[...] = (ri > ci).astype(BF16)
        running[...] = jnp.zeros_like(running)

    for s in range(x_ref.shape[0] // TM_SUB):
        _outproj_rows(pl.ds(s * TM_SUB, TM_SUB), yl_ref, ys_ref, x_ref, wo_ref, nw_ref, wr_ref, rb_ref,
                      x1_ref, hp_ref, route_ref, stril, running)
    cnt_ref[...] = running[...]


def _outproj_rows(rows, yl_ref, ys_ref, x_ref, wo_ref, nw_ref, wr_ref, rb_ref,
                  x1_ref, hp_ref, route_ref, stril, running):
    tm = TM_SUB
    y_cat = jnp.concatenate([yl_ref[rows, :], ys_ref[rows, :]], axis=1)
    x1 = x_ref[rows, :] + jnp.dot(y_cat, wo_ref[...], preferred_element_type=F32)
    x1_ref[rows, :] = x1
    h = _rms(x1, nw_ref[...])
    h_hi = h.astype(BF16)
    h_rt = h_hi.astype(F32)
    bits = lax.bitcast_convert_type(h_rt, U32)
    hp_ref[rows, :] = bits[:, D_HALF:] | (bits[:, :D_HALF] >> 16)
    logits = jnp.dot(h_hi, wr_ref[...], preferred_element_type=F32) + rb_ref[...]

    lane = lax.broadcasted_iota(jnp.int32, (tm, LANES), 1)
    lane_f = lane.astype(F32)
    is_c = lane < MOE_GROUPS
    lc = jnp.where(is_c, logits, NEG_BIG)
    m_c, g_idx = _first_argmax(lc, lane_f)
    g_w = 1.0 / jnp.sum(jnp.where(is_c, jnp.exp(lc - m_c), 0.0), axis=-1, keepdims=True)
    lo = float(MOE_GROUPS) + float(EXPERTS_PER_GROUP) * g_idx
    is_f = (lane_f >= lo) & (lane_f < lo + float(EXPERTS_PER_GROUP))
    lf = jnp.where(is_f, logits, NEG_BIG)
    v1, i1 = _first_argmax(lf, lane_f)
    v2, i2 = _first_argmax(jnp.where(lane_f == i1, NEG_BIG, lf), lane_f)
    ex = jnp.exp(v2 - v1)
    w1 = g_w / (1.0 + ex)
    w2 = g_w * ex / (1.0 + ex)

    oh1 = (lane_f == i1).astype(F32)
    oh2 = (lane_f == i2).astype(F32)
    both = oh1 + oh2
    before = jnp.dot(stril[...], both.astype(BF16), preferred_element_type=F32) + running[...]
    r1 = jnp.sum(oh1 * before, axis=-1, keepdims=True)
    r2 = jnp.sum(oh2 * before, axis=-1, keepdims=True)
    running[...] = running[...] + jnp.sum(both, axis=0, keepdims=True)

    e1 = i1 - float(MOE_GROUPS)
    e2 = i2 - float(MOE_GROUPS)
    route = jnp.zeros((tm, LANES), F32)
    for off, (a, b) in ((ROUTE_E, (e1, e2)), (ROUTE_W, (w1, w2)), (ROUTE_R, (r1, r2))):
        route = jnp.where(lane == off, a, jnp.where(lane == off + 1, b, route))
    route_ref[rows, :] = route


def _outproj(layer, yl, ys, x, wo, nw, wr, rb):
    n = x.shape[0]
    tm = TM_PROJ
    row = lambda w: pl.BlockSpec((tm, w), lambda i: (i, 0))
    return pl.pallas_call(
        _outproj_kernel,
        grid=(n // tm,),
        in_specs=[row(D_LRU), row(D_SSD), row(D_MODEL),
                  pl.BlockSpec((None,) + wo.shape[1:], lambda i: (layer, 0, 0)),
                  _const_spec((1, D_MODEL)),
                  _const_spec(wr.shape), _const_spec((1, LANES))],
        out_specs=[row(D_MODEL), row(D_HALF), row(LANES), _const_spec((1, LANES))],
        out_shape=[jax.ShapeDtypeStruct((n, D_MODEL), F32),
                   jax.ShapeDtypeStruct((n, D_HALF), U32),
                   jax.ShapeDtypeStruct((n, LANES), F32),
                   jax.ShapeDtypeStruct((1, LANES), F32)],
        scratch_shapes=[pltpu.VMEM((TM_SUB, TM_SUB), BF16), pltpu.VMEM((1, LANES), F32)],
        compiler_params=_cparams(("arbitrary",)),
        name="outproj",
    )(yl, ys, x, wo, nw, wr, rb)


SCHED_EXPERT, SCHED_FIRST, SCHED_NEXT, SCHED_SLOT = 0, 1, 2, 3


def _expert_kernel(sch_ref, nu_ref, rt_ref, rtn_ref, hp_hbm, wg_hbm, wu_hbm, wd_hbm, yb_ref,
                   hp_v, xs0, xs1, wg_f, wu_f, wd_f, wg_s, wu_s, wd_s, sem, wsem, *, layer):
    j = pl.program_id(0)
    out_rows = TM_EXP * ROW_TILES

    def weight_copies(expert, s):
        return (pltpu.make_async_copy(wg_hbm.at[layer, expert], wg_f.at[s], wsem.at[s, 0]),
                pltpu.make_async_copy(wu_hbm.at[layer, expert], wu_f.at[s], wsem.at[s, 1]),
                pltpu.make_async_copy(wd_hbm.at[layer, expert], wd_f.at[s], wsem.at[s, 2]))

    def gather(idx_ref, row, dst):
        for r in range(TM_EXP):
            dst[r:r + 1, :] = hp_v[pl.ds(idx_ref[row, r], 1), :]

    @pl.when(j == 0)
    def _():
        for cp in weight_copies(sch_ref[SCHED_EXPERT, 0], sch_ref[SCHED_SLOT, 0]):
            cp.start()
        cp = pltpu.make_async_copy(hp_hbm, hp_v, sem)
        cp.start()
        cp.wait()
        gather(rt_ref, 0, xs0)

    def block(half, src, prefetch):
        i = 2 * j + half
        e = sch_ref[SCHED_EXPERT, i]
        slot = sch_ref[SCHED_SLOT, i]
        active = i < nu_ref[0]
        out = pl.ds(half * out_rows, out_rows)

        @pl.when(active & (sch_ref[SCHED_FIRST, i] == 1))
        def _():
            for cp in weight_copies(e, slot):
                cp.wait()
            wg_s[...] = wg_f[slot].astype(BF16)
            wu_s[...] = wu_f[slot].astype(BF16)
            wd_s[...] = wd_f[slot].astype(BF16)
            nxt = sch_ref[SCHED_NEXT, i]

            @pl.when(nxt != e)
            def _():
                for cp in weight_copies(nxt, 1 - slot):
                    cp.start()

        @pl.when(active)
        def _():
            prefetch()
            packed = src[...]
            x_lo = lax.bitcast_convert_type(packed << 16, F32).astype(BF16)
            x_hi = lax.bitcast_convert_type((packed >> 16) << 16, F32).astype(BF16)
            gate = (jnp.dot(x_lo, wg_s[0:D_HALF, :], preferred_element_type=F32)
                    + jnp.dot(x_hi, wg_s[D_HALF:, :], preferred_element_type=F32))
            up = (jnp.dot(x_lo, wu_s[0:D_HALF, :], preferred_element_type=F32)
                  + jnp.dot(x_hi, wu_s[D_HALF:, :], preferred_element_type=F32))
            hid = (gate * _sigmoid(gate) * up).astype(BF16)
            y = jnp.dot(hid, wd_s[...], preferred_element_type=F32)
            for c in range(ROW_TILES):
                yb_ref[pl.ds(half * out_rows + c, TM_EXP, stride=ROW_TILES), :] = (
                    y[:, c * LANES:(c + 1) * LANES])

        @pl.when(jnp.logical_not(active))
        def _():
            yb_ref[out, :] = jnp.zeros((out_rows, LANES), F32)

    block(0, xs0, lambda: gather(rt_ref, 1, xs1))
    block(1, xs1, lambda: gather(rtn_ref, 0, xs0))


def _experts(layer, sched, n_used, row_tok, hp, wg, wu, wd):
    n_blocks = sched.shape[1]
    tm = TM_EXP
    n_steps = n_blocks // 2
    hbm = pl.BlockSpec(memory_space=pl.ANY)
    grid_spec = pltpu.PrefetchScalarGridSpec(
        num_scalar_prefetch=2,
        grid=(n_steps,),
        in_specs=[pl.BlockSpec((None, 2, tm), lambda j, sch, nu: (j, 0, 0), memory_space=pltpu.SMEM),
                  pl.BlockSpec((None, 2, tm), lambda j, sch, nu: (jnp.minimum(j + 1, n_steps - 1), 0, 0),
                               memory_space=pltpu.SMEM),
                  hbm, hbm, hbm, hbm],
        out_specs=pl.BlockSpec((2 * tm * ROW_TILES, LANES), lambda j, sch, nu: (j, 0)),
        scratch_shapes=[pltpu.VMEM(hp.shape, U32),
                        pltpu.VMEM((tm, D_HALF), U32),
                        pltpu.VMEM((tm, D_HALF), U32),
                        pltpu.VMEM((2, D_MODEL, D_EXPERT), F32),
                        pltpu.VMEM((2, D_MODEL, D_EXPERT), F32),
                        pltpu.VMEM((2, D_EXPERT, D_MODEL), F32),
                        pltpu.VMEM((D_MODEL, D_EXPERT), BF16),
                        pltpu.VMEM((D_MODEL, D_EXPERT), BF16),
                        pltpu.VMEM((D_EXPERT, D_MODEL), BF16),
                        pltpu.SemaphoreType.DMA(()),
                        pltpu.SemaphoreType.DMA((2, 3))],
    )
    return pl.pallas_call(
        functools.partial(_expert_kernel, layer=layer),
        grid_spec=grid_spec,
        out_shape=jax.ShapeDtypeStruct((n_blocks * tm * ROW_TILES, LANES), F32),
        compiler_params=_cparams(("arbitrary",)),
        name="experts",
    )(sched, n_used, row_tok.reshape(n_steps, 2, tm), row_tok.reshape(n_steps, 2, tm), hp, wg, wu, wd)


def _moe_residual(dcur_ref, dnxt_ref, x_ref, w_ref, yb_ref, gbuf, sems):
    t = x_ref.shape[0]
    i = pl.program_id(0)
    n_steps = pl.num_programs(0)
    slot = i % 2

    def copy(dref, s, k, tile, tok):
        row0 = (tile * T_TOK + tok) * ROW_TILES
        if not isinstance(tok, int):
            row0 = pl.multiple_of(row0, ROW_TILES)
        src = pl.ds(pl.multiple_of(dref[tile, k, tok] * ROW_TILES, ROW_TILES), ROW_TILES)
        return pltpu.make_async_copy(yb_ref.at[src, :], gbuf.at[s, k, pl.ds(row0, ROW_TILES), :],
                                     sems.at[s])

    def per_row(fn):
        for tile in range(t // T_TOK):
            def body(tok, c):
                for k in range(TOP_K):
                    fn(k, tile, tok)
                return c
            lax.fori_loop(0, T_TOK, body, 0, unroll=8)

    @pl.when(i == 0)
    def _():
        per_row(lambda k, tile, tok: copy(dcur_ref, 0, k, tile, tok).start())

    @pl.when(i + 1 < n_steps)
    def _():
        for tile in range(t // T_TOK):
            for tok in range(T_TOK):
                for k in range(TOP_K):
                    copy(dnxt_ref, 1 - slot, k, tile, tok).start(priority=k)

    per_row(lambda k, tile, tok: copy(dcur_ref, slot, k, tile, tok).wait())

    w0 = w_ref[:, ROUTE_W:ROUTE_W + 1]
    w1 = w_ref[:, ROUTE_W + 1:ROUTE_W + 2]
    parts = []
    for c in range(ROW_TILES):
        g0 = gbuf[slot, 0, pl.ds(c, t, stride=ROW_TILES), :]
        g1 = gbuf[slot, 1, pl.ds(c, t, stride=ROW_TILES), :]
        parts.append(x_ref[:, c * LANES:(c + 1) * LANES] + w0 * g0 + w1 * g1)
    return jnp.concatenate(parts, axis=-1)


def _moe_operands(t, n, dest3, x, route, yb):
    n_steps = n // t
    tiles = t // T_TOK
    dspec = lambda f: pl.BlockSpec((tiles, TOP_K, T_TOK), lambda i: (f(i), 0, 0), memory_space=pltpu.SMEM)
    in_specs = [dspec(lambda i: i), dspec(lambda i: jnp.minimum(i + 1, n_steps - 1)),
                pl.BlockSpec((t, D_MODEL), lambda i: (i, 0)),
                pl.BlockSpec((t, LANES), lambda i: (i, 0)),
                pl.BlockSpec(memory_space=pl.ANY)]
    scratch = [pltpu.VMEM((2, TOP_K, t * ROW_TILES, LANES), F32), pltpu.SemaphoreType.DMA((2,))]
    return in_specs, scratch, (dest3, dest3, x, route, yb)


def _combine_kernel(dcur_ref, dnxt_ref, x_ref, w_ref, yb_ref, nw_ref, o_ref, gbuf, sems, *, final_norm):
    out = _moe_residual(dcur_ref, dnxt_ref, x_ref, w_ref, yb_ref, gbuf, sems)
    if final_norm:
        out = _rms(out, nw_ref[...])
    o_ref[...] = out


def _combine(dest3, x, route, nw, yb, final_norm):
    n = x.shape[0]
    t = T_TOK
    in_specs, scratch, args = _moe_operands(t, n, dest3, x, route, yb)
    return pl.pallas_call(
        functools.partial(_combine_kernel, final_norm=final_norm),
        grid=(n // t,),
        in_specs=in_specs + [_const_spec((1, D_MODEL))],
        out_specs=pl.BlockSpec((t, D_MODEL), lambda i: (i, 0)),
        out_shape=jax.ShapeDtypeStruct((n, D_MODEL), F32),
        scratch_shapes=scratch,
        compiler_params=_cparams(("arbitrary",)),
        name="combine",
    )(*args, nw)


T_SLOT = 1024


def _slots_kernel(route_ref, ps_ref, dest_ref):
    route = route_ref[...]
    lane_f = lax.broadcasted_iota(jnp.int32, route.shape, 1).astype(F32)
    ps = ps_ref[...]
    slots = []
    for k in range(TOP_K):
        e = route[:, ROUTE_E + k:ROUTE_E + k + 1] + float(MOE_GROUPS)
        base = jnp.sum(jnp.where(lane_f == e, ps, 0.0), axis=-1, keepdims=True)
        slots.append(base + route[:, ROUTE_R + k:ROUTE_R + k + 1])
    slab = jnp.where(lane_f == 0.0, slots[0], jnp.where(lane_f == 1.0, slots[1], 0.0))
    for i in range(T_SLOT // T_TOK):
        part = slab[i * T_TOK:(i + 1) * T_TOK, :].T
        dest_ref[i] = part[0:TOP_K, :].astype(jnp.int32)


def _slots(route, pad_start):
    n = route.shape[0]
    ps = jnp.pad(pad_start.astype(F32), (MOE_GROUPS, LANES - MOE_GROUPS - N_EXPERTS)).reshape(1, LANES)
    per = T_SLOT // T_TOK
    return pl.pallas_call(
        _slots_kernel,
        grid=(n // T_SLOT,),
        in_specs=[pl.BlockSpec((T_SLOT, LANES), lambda i: (i, 0)), _const_spec((1, LANES))],
        out_specs=pl.BlockSpec((per, TOP_K, T_TOK), lambda i: (i, 0, 0)),
        out_shape=jax.ShapeDtypeStruct((n // T_TOK, TOP_K, T_TOK), jnp.int32),
        compiler_params=_cparams(("arbitrary",)),
        name="slots",
    )(route, ps)


def _route(route, cnt, n_blocks):
    n = route.shape[0]
    counts = cnt[0, MOE_GROUPS:MOE_GROUPS + N_EXPERTS].astype(jnp.int32)
    padded = (counts + TM_EXP - 1) // TM_EXP * TM_EXP
    pad_end = jnp.cumsum(padded)
    pad_start = pad_end - padded
    dest3 = _slots(route, pad_start)
    n_used = (pad_end[-1:] // TM_EXP).astype(jnp.int32)
    blk = jnp.arange(n_blocks, dtype=jnp.int32)
    blk_expert = jnp.minimum(jnp.sum(pad_end[None, :] <= (blk * TM_EXP)[:, None], axis=1),
                             N_EXPERTS - 1).astype(jnp.int32)
    eid = jnp.arange(N_EXPERTS, dtype=jnp.int32)
    nonempty = counts > 0
    later = (eid[None, :] > eid[:, None]) & nonempty[None, :]
    next_used = jnp.min(jnp.where(later, eid[None, :], N_EXPERTS), axis=1)
    next_used = jnp.where(next_used == N_EXPERTS, eid, next_used)
    ordinal = jnp.cumsum(nonempty.astype(jnp.int32)) - 1
    first = ((blk * TM_EXP == pad_start[blk_expert]) & (blk < n_used[0])).astype(jnp.int32)
    sched = jnp.stack([blk_expert, first, next_used[blk_expert], ordinal[blk_expert] % 2])
    token = jnp.arange(n, dtype=jnp.int32).reshape(n // T_TOK, 1, T_TOK)
    token = jnp.broadcast_to(token, dest3.shape)
    row_tok = jnp.zeros((n_blocks * TM_EXP,), jnp.int32).at[dest3.reshape(-1)].add(
        token.reshape(-1), unique_indices=True)
    return dest3, sched.astype(jnp.int32), n_used, row_tok


def _gate_blocks(wa, wx):
    per = GATE_W // LRU_BW
    eye = jnp.eye(per, dtype=F32)

    def bd(w):
        w = w.reshape(LRU_HEADS // per, per, LRU_BW, LRU_BW)
        full = jnp.einsum('gpij,pq->gpiqj', w, eye)
        return full.reshape(LRU_HEADS // per, GATE_W, GATE_W)

    return jnp.concatenate([bd(wa), bd(wx)], axis=-1).astype(BF16)


def kernel(x, norm_mix, w_in, lru_conv_w, lru_conv_b, lru_wa, lru_ba, lru_wx, lru_bx, lru_lambda, lru_norm, ssd_conv_w, ssd_conv_b, ssd_dt_bias, ssd_a_log, ssd_d, ssd_norm, w_out, norm_ffn, w_coarse, b_coarse, w_fine, b_fine, w_gate, w_up, w_down, final_norm):
    bsz, seq, d = x.shape
    n = bsz * seq
    depth = w_in.shape[0]
    n_assign = n * TOP_K
    n_blocks = -(-(n_assign + N_EXPERTS * (TM_EXP - 1)) // TM_EXP)
    n_blocks += n_blocks % 2
    o_dt = 2 * D_LRU + D_SSD + SSD_XBC

    lane_pad = lambda v: jnp.pad(v, ((0, 0), (0, LANES - v.shape[1])))[:, None, :]
    row_pad = lambda w: jnp.pad(w, ((0, 0), (0, SUBLANES - w.shape[1]), (0, 0)))
    w_in_all = w_in.astype(BF16)
    w_in_bf = w_in_all[:, :, :o_dt]
    wdt_all = jnp.pad(w_in_all[:, :, o_dt:], ((0, 0), (0, 0), (0, LANES - SSD_HEADS)))
    w_out_bf = w_out.astype(BF16)
    gates_all = jax.vmap(_gate_blocks)(lru_wa, lru_wx)
    lru_cw, ssd_cw = row_pad(lru_conv_w), row_pad(ssd_conv_w)
    dtb_all, alog_all = lane_pad(ssd_dt_bias), lane_pad(ssd_a_log)
    dvec_all = jnp.repeat(ssd_d, SSD_HEAD_DIM, axis=1)[:, None, :]
    w_r = jnp.concatenate([w_coarse, w_fine.transpose(0, 2, 1, 3).reshape(depth, d, N_EXPERTS)], axis=2)
    w_r_all = jnp.pad(w_r, ((0, 0), (0, 0), (0, LANES - w_r.shape[2]))).astype(BF16)
    r_bias_all = lane_pad(jnp.concatenate([b_coarse, b_fine.reshape(depth, -1)], axis=1))
    vec = lambda p, i: p[i].reshape(1, -1)

    xt = x.reshape(n, d)
    for i in range(depth):
        lx, lg, z, xbc, dt = _inproj(i, xt, vec(norm_mix, i), w_in_bf, wdt_all[i])
        y_lru = _lru(lx, lg, lru_cw[i], vec(lru_conv_b, i), gates_all[i], vec(lru_ba, i),
                     vec(lru_bx, i), vec(lru_lambda, i), vec(lru_norm, i), bsz, seq)
        y_ssd = _ssd(xbc, z, dt, ssd_cw[i], vec(ssd_conv_b, i), dtb_all[i], alog_all[i],
                     dvec_all[i], vec(ssd_norm, i), bsz, seq)
        x1, hp, route, cnt = _outproj(i, y_lru, y_ssd, xt, w_out_bf, vec(norm_ffn, i),
                                      w_r_all[i], r_bias_all[i])
        dest3, sched, n_used, row_tok = _route(route, cnt, n_blocks)
        yb = _experts(i, sched, n_used, row_tok, hp, w_gate, w_up, w_down)
        xt = _combine(dest3, x1, route, final_norm.reshape(1, d), yb, final_norm=(i == depth - 1))
    return xt.reshape(bsz, seq, d)
```

```python
import functools

import jax
import jax.numpy as jnp
from jax import lax
from jax.experimental import pallas as pl
from jax.experimental.pallas import tpu as pltpu

F32 = jnp.float32
BF16 = jnp.bfloat16
U32 = jnp.uint32

D_MODEL = 1024
D_LRU = 1024
LRU_HEADS = 16
LRU_BW = 64
RG_C = 8.0
CONV_K = 4
D_SSD = 1024
SSD_HEAD_DIM = 64
SSD_HEADS = 16
SSD_GROUPS = 4
SSD_STATE = 128
SSD_CHUNK = 128
MOE_GROUPS = 4
EXPERTS_PER_GROUP = 8
N_EXPERTS = 32
TOP_K = 2
D_EXPERT = 512
EPS = 1e-6
LOG2E = 1.4426950408889634

LANES = 128
SUBLANES = 8
VMEM_LIMIT = 60 * 1024 * 1024

TM_PROJ = 512
TM_SUB = 256
T_LRU = 512
T_SSD = 512
TM_EXP = 256
T_TOK = 512
GATE_W = 256
D_HALF = D_MODEL // 2
ROW_TILES = D_MODEL // LANES


def _cparams(sem):
    return pltpu.CompilerParams(dimension_semantics=sem, vmem_limit_bytes=VMEM_LIMIT)


def _const_spec(shape):
    n = len(shape)
    return pl.BlockSpec(shape, lambda *_: (0,) * n)


def _sigmoid(x):
    return 1.0 / (1.0 + jnp.exp(-x))


def _log1p(e):
    u = 1.0 + e
    d = u - 1.0
    return jnp.where(d == 0.0, e, jnp.log(u) * (e / jnp.where(d == 0.0, 1.0, d)))


def _softplus(x):
    return jnp.maximum(x, 0.0) + _log1p(jnp.exp(-jnp.abs(x)))


def _rms(x, w):
    ms = jnp.mean(x * x, axis=-1, keepdims=True)
    return x * lax.rsqrt(ms + EPS) * w


W_COL = 1024
SSD_XBC = D_SSD + 2 * SSD_GROUPS * SSD_STATE


def _inproj_kernel(x_ref, nw_ref, wlx_ref, wlg_ref, wz_ref, wx_ref, wbc_ref, wdt_ref,
                   lx_ref, lg_ref, z_ref, xbc_ref, dt_ref):
    h = _rms(x_ref[...], nw_ref[...]).astype(BF16)
    lx_ref[...] = jnp.dot(h, wlx_ref[...], preferred_element_type=F32).astype(BF16)
    lg_ref[...] = jnp.dot(h, wlg_ref[...], preferred_element_type=F32).astype(BF16)
    z_ref[...] = jnp.dot(h, wz_ref[...], preferred_element_type=F32).astype(BF16)
    xbc_ref[:, 0:W_COL] = jnp.dot(h, wx_ref[...], preferred_element_type=F32).astype(BF16)
    xbc_ref[:, W_COL:] = jnp.dot(h, wbc_ref[...], preferred_element_type=F32).astype(BF16)
    dt_ref[...] = jnp.dot(h, wdt_ref[...], preferred_element_type=F32)


def _inproj(layer, x, nw, w_in, wdt):
    n = x.shape[0]
    tm = TM_PROJ
    row = lambda w: pl.BlockSpec((tm, w), lambda i: (i, 0))
    wcol = lambda c: pl.BlockSpec((None, D_MODEL, W_COL), lambda i: (layer, 0, c))
    return pl.pallas_call(
        _inproj_kernel,
        grid=(n // tm,),
        in_specs=[row(D_MODEL), _const_spec((1, D_MODEL)),
                  wcol(0), wcol(1), wcol(2), wcol(3), wcol(4), _const_spec(wdt.shape)],
        out_specs=[row(D_LRU), row(D_LRU), row(D_SSD), row(SSD_XBC), row(LANES)],
        out_shape=[jax.ShapeDtypeStruct((n, D_LRU), BF16),
                   jax.ShapeDtypeStruct((n, D_LRU), BF16),
                   jax.ShapeDtypeStruct((n, D_SSD), BF16),
                   jax.ShapeDtypeStruct((n, SSD_XBC), BF16),
                   jax.ShapeDtypeStruct((n, LANES), F32)],
        compiler_params=_cparams(("arbitrary",)),
        name="inproj",
    )(x, nw, w_in, w_in, w_in, w_in, w_in, wdt)


def _causal_conv(xbuf, cw_ref, cb_ref, t):
    cw = cw_ref[...]
    n = t + SUBLANES
    full = xbuf[...]
    acc = cb_ref[...] + cw[CONV_K - 1:CONV_K, :] * full[SUBLANES:, :]
    for k in range(CONV_K - 1):
        shifted = pltpu.roll(full, n - (SUBLANES - 3 + k), 0)[0:t, :]
        acc = acc + cw[k:k + 1, :] * shifted
    return acc


def _lru_kernel(lx_ref, lg_ref, cw_ref, cb_ref, wg_ref, ba_ref, bx_ref, lam_ref, nw_ref,
                y_ref, xbuf, a_s, v_s, h_s, hcarry):
    t = T_LRU
    j = pl.program_id(1)

    @pl.when(j == 0)
    def _():
        xbuf[0:SUBLANES, :] = jnp.zeros((SUBLANES, D_LRU), F32)
        hcarry[...] = jnp.zeros_like(hcarry)

    xbuf[SUBLANES:SUBLANES + t, :] = lx_ref[...].astype(F32)
    u = _causal_conv(xbuf, cw_ref, cb_ref, t)
    xbuf[0:SUBLANES, :] = xbuf[t:t + SUBLANES, :]

    lam = lam_ref[...]
    log_sig = jnp.minimum(lam, 0.0) - _log1p(jnp.exp(-jnp.abs(lam)))
    rate = (RG_C * LOG2E) * log_sig
    for g in range(D_LRU // GATE_W):
        sl = slice(g * GATE_W, (g + 1) * GATE_W)
        ug = u[:, sl]
        gates = jnp.dot(ug.astype(BF16), wg_ref[g], preferred_element_type=F32)
        r = _sigmoid(gates[:, :GATE_W] + ba_ref[:, sl])
        i = _sigmoid(gates[:, GATE_W:] + bx_ref[:, sl])
        a = jnp.exp2(r * rate[:, sl])
        mult = jnp.sqrt(1.0 - a * a)
        a_s[:, sl] = a
        v_s[:, sl] = mult * (i * ug)

    row = lax.broadcasted_iota(jnp.int32, (SUBLANES, D_LRU), 0)

    def scan_body(k, hprev):
        r0 = pl.multiple_of(k * SUBLANES, SUBLANES)
        a = a_s[pl.ds(r0, SUBLANES), :]
        v = v_s[pl.ds(r0, SUBLANES), :]
        for s in (1, 2, 4):
            keep = row >= s
            a_sh = jnp.where(keep, pltpu.roll(a, s, 0), 1.0)
            v_sh = jnp.where(keep, pltpu.roll(v, s, 0), 0.0)
            v = v + a * v_sh
            a = a * a_sh
        h = v + a * hprev
        h_s[pl.ds(r0, SUBLANES), :] = h
        return h[SUBLANES - 1:SUBLANES, :]

    hcarry[...] = lax.fori_loop(0, t // SUBLANES, scan_body, hcarry[...], unroll=4)

    g_in = lg_ref[...].astype(F32)
    inner = g_in * (0.7978845608028654 + 0.035677408136300125 * (g_in * g_in))
    half_g = 0.5 * g_in
    gelu = half_g + half_g * jnp.tanh(inner)
    y_ref[...] = _rms(gelu * h_s[...], nw_ref[...]).astype(BF16)


def _lru(lx, lg, cw, cb, wg, ba, bx, lam, nw, bsz, seq):
    t = T_LRU
    nj = seq // t
    row = pl.BlockSpec((t, D_LRU), lambda b, j: (b * nj + j, 0))
    vec = _const_spec((1, D_LRU))
    return pl.pallas_call(
        _lru_kernel,
        grid=(bsz, nj),
        in_specs=[row, row, _const_spec(cw.shape), vec, _const_spec(wg.shape),
                  vec, vec, vec, vec],
        out_specs=row,
        out_shape=jax.ShapeDtypeStruct((bsz * seq, D_LRU), BF16),
        scratch_shapes=[pltpu.VMEM((t + SUBLANES, D_LRU), F32),
                        pltpu.VMEM((t, D_LRU), F32),
                        pltpu.VMEM((t, D_LRU), F32),
                        pltpu.VMEM((t, D_LRU), F32),
                        pltpu.VMEM((1, D_LRU), F32)],
        compiler_params=_cparams(("arbitrary", "arbitrary")),
        name="rglru",
    )(lx, lg, cw, cb, wg, ba, bx, lam, nw)


def _split3(x):
    hi = x.astype(BF16)
    r1 = x - hi.astype(F32)
    mid = r1.astype(BF16)
    lo = (r1 - mid.astype(F32)).astype(BF16)
    return hi, mid, lo


def _ssd_kernel(xbc_ref, z_ref, dt_ref, cw_ref, cb_ref, dtb_ref, alog_ref, dvec_ref, nw_ref,
                tril_ref, fut_ref, y_ref, xbuf, xc_s, state, y_s):
    tt = xbc_ref.shape[0]
    j = pl.program_id(1)

    @pl.when(j == 0)
    def _():
        xbuf[0:SUBLANES, :] = jnp.zeros((SUBLANES, xbuf.shape[1]), F32)
        state[...] = jnp.zeros_like(state)

    xbuf[SUBLANES:SUBLANES + tt, :] = xbc_ref[...].astype(F32)
    u = _causal_conv(xbuf, cw_ref, cb_ref, tt)
    xbuf[0:SUBLANES, :] = xbuf[tt:tt + SUBLANES, :]
    xc_s[...] = u * _sigmoid(u)
    for c in range(tt // SSD_CHUNK):
        _ssd_chunk(c, xc_s, dt_ref, dtb_ref, alog_ref, tril_ref, fut_ref, state, y_s)

    zf = z_ref[...].astype(F32)
    y = (y_s[...] + xc_s[:, 0:D_SSD] * dvec_ref[...]) * (zf * _sigmoid(zf))
    y_ref[...] = _rms(y, nw_ref[...]).astype(BF16)


def _ssd_chunk(c, xc_s, dt_ref, dtb_ref, alog_ref, tril_ref, fut_ref, state, y_s):
    t = SSD_CHUNK
    rows = slice(c * t, (c + 1) * t)
    gn = SSD_GROUPS * SSD_STATE
    xs = xc_s[rows, 0:D_SSD]
    bm = xc_s[rows, D_SSD:D_SSD + gn]
    cm = xc_s[rows, D_SSD + gn:]

    dt = _softplus(dt_ref[rows, :] + dtb_ref[...])
    d_a = dt * (-jnp.exp(alog_ref[...]))
    tril = tril_ref[...]
    future = fut_ref[...]
    hi, mid, lo = _split3(d_a)
    a_cs = (jnp.dot(tril, hi, preferred_element_type=F32)
            + jnp.dot(tril, mid, preferred_element_type=F32)
            + jnp.dot(tril, lo, preferred_element_type=F32))
    a_cs_t = a_cs.T
    dt_t = dt.T
    a_last_t = a_cs_t[:, t - 1:t]
    w_state_t = jnp.exp(a_last_t - a_cs_t) * dt_t
    chunk_decay_t = jnp.exp(a_last_t)
    src_t = a_cs_t - jnp.log(dt_t)

    lane = lax.broadcasted_iota(jnp.int32, (1, LANES), 1)
    first = lane < SSD_HEAD_DIM

    heads_per_group = SSD_HEADS // SSD_GROUPS
    for g in range(SSD_GROUPS):
        gsl = slice(g * SSD_STATE, (g + 1) * SSD_STATE)
        c_g = cm[:, gsl].astype(BF16)
        b_g = bm[:, gsl]
        scores = lax.dot_general(c_g, b_g.astype(BF16), (((1,), (1,)), ((), ())),
                                 preferred_element_type=F32)
        b_t = b_g.T
        st_g = state[g]
        y_off = jnp.dot(c_g, st_g.astype(BF16), preferred_element_type=F32)
        for q in range(heads_per_group // 2):
            h0 = g * heads_per_group + 2 * q
            psl = slice(h0 * SSD_HEAD_DIM, (h0 + 2) * SSD_HEAD_DIM)
            lsl = slice(2 * q * SSD_HEAD_DIM, (2 * q + 2) * SSD_HEAD_DIM)
            x_pair = xs[:, psl].astype(BF16)
            yd, ns, cols, cd = [], [], [], []
            for h in (h0, h0 + 1):
                col = jnp.broadcast_to(a_cs[:, h:h + 1], (t, t))
                lmat_dt = jnp.exp(col - src_t[h:h + 1, :] + future)
                m = (scores * lmat_dt).astype(BF16)
                yd.append(jnp.dot(m, x_pair, preferred_element_type=F32))
                bw = (b_t * w_state_t[h:h + 1, :]).astype(BF16)
                ns.append(jnp.dot(bw, x_pair, preferred_element_type=F32))
                cols.append(col)
                cd.append(jnp.broadcast_to(chunk_decay_t[h:h + 1, :], (1, LANES)))
            y_pair = (jnp.where(first, yd[0], yd[1])
                      + jnp.exp(jnp.where(first, cols[0], cols[1])) * y_off[:, lsl])
            y_s[rows, psl] = y_pair
            state[g, :, lsl] = (st_g[:, lsl] * jnp.where(first, cd[0], cd[1])
                                + jnp.where(first, ns[0], ns[1]))


def _ssd(xbc, z, dt, cw, cb, dtb, alog, dvec, nw, bsz, seq):
    t = T_SSD
    nj = seq // t
    dx = xbc.shape[1]
    row = lambda w: pl.BlockSpec((t, w), lambda b, j: (b * nj + j, 0))
    ch = SSD_CHUNK
    causal = jnp.arange(ch)[:, None] >= jnp.arange(ch)[None, :]
    return pl.pallas_call(
        _ssd_kernel,
        grid=(bsz, nj),
        in_specs=[row(dx), row(D_SSD), row(LANES), _const_spec(cw.shape), _const_spec((1, dx)),
                  _const_spec((1, LANES)), _const_spec((1, LANES)),
                  _const_spec((1, D_SSD)), _const_spec((1, D_SSD)),
                  _const_spec((ch, ch)), _const_spec((ch, ch))],
        out_specs=row(D_SSD),
        out_shape=jax.ShapeDtypeStruct((bsz * seq, D_SSD), BF16),
        scratch_shapes=[pltpu.VMEM((t + SUBLANES, dx), F32),
                        pltpu.VMEM((t, dx), F32),
                        pltpu.VMEM((SSD_GROUPS, SSD_STATE, D_SSD // SSD_GROUPS), F32),
                        pltpu.VMEM((t, D_SSD), F32)],
        compiler_params=_cparams(("arbitrary", "arbitrary")),
        name="ssd",
    )(xbc, z, dt, cw, cb, dtb, alog, dvec, nw, causal.astype(BF16),
      jnp.where(causal, 0.0, NEG_BIG).astype(F32))


ROUTE_E, ROUTE_W, ROUTE_R = 0, 2, 4
NEG_BIG = -1e30


def _first_argmax(vals, lane_f):
    m = jnp.max(vals, axis=-1, keepdims=True)
    idx = jnp.min(jnp.where(vals == m, lane_f, float(LANES)), axis=-1, keepdims=True)
    return m, idx


def _outproj_kernel(yl_ref, ys_ref, x_ref, wo_ref, nw_ref, wr_ref, rb_ref,
                    x1_ref, hp_ref, route_ref, cnt_ref, stril, running):
    @pl.when(pl.program_id(0) == 0)
    def _():
        ri = lax.broadcasted_iota(jnp.int32, (TM_SUB, TM_SUB), 0)
        ci = lax.broadcasted_iota(jnp.int32, (TM_SUB, TM_SUB), 1)
        stril[...] = (ri > ci).astype(BF16)
        running[...] = jnp.zeros_like(running)

    for s in range(x_ref.shape[0] // TM_SUB):
        _outproj_rows(pl.ds(s * TM_SUB, TM_SUB), yl_ref, ys_ref, x_ref, wo_ref, nw_ref, wr_ref, rb_ref,
                      x1_ref, hp_ref, route_ref, stril, running)
    cnt_ref[...] = running[...]


def _outproj_rows(rows, yl_ref, ys_ref, x_ref, wo_ref, nw_ref, wr_ref, rb_ref,
                  x1_ref, hp_ref, route_ref, stril, running):
    tm = TM_SUB
    y_cat = jnp.concatenate([yl_ref[rows, :], ys_ref[rows, :]], axis=1)
    x1 = x_ref[rows, :] + jnp.dot(y_cat, wo_ref[...], preferred_element_type=F32)
    x1_ref[rows, :] = x1
    h = _rms(x1, nw_ref[...])
    h_hi = h.astype(BF16)
    h_rt = h_hi.astype(F32)
    bits = lax.bitcast_convert_type(h_rt, U32)
    hp_ref[rows, :] = bits[:, D_HALF:] | (bits[:, :D_HALF] >> 16)
    logits = jnp.dot(h_hi, wr_ref[...], preferred_element_type=F32) + rb_ref[...]

    lane = lax.broadcasted_iota(jnp.int32, (tm, LANES), 1)
    lane_f = lane.astype(F32)
    is_c = lane < MOE_GROUPS
    lc = jnp.where(is_c, logits, NEG_BIG)
    m_c, g_idx = _first_argmax(lc, lane_f)
    g_w = 1.0 / jnp.sum(jnp.where(is_c, jnp.exp(lc - m_c), 0.0), axis=-1, keepdims=True)
    lo = float(MOE_GROUPS) + float(EXPERTS_PER_GROUP) * g_idx
    is_f = (lane_f >= lo) & (lane_f < lo + float(EXPERTS_PER_GROUP))
    lf = jnp.where(is_f, logits, NEG_BIG)
    v1, i1 = _first_argmax(lf, lane_f)
    v2, i2 = _first_argmax(jnp.where(lane_f == i1, NEG_BIG, lf), lane_f)
    ex = jnp.exp(v2 - v1)
    w1 = g_w / (1.0 + ex)
    w2 = g_w * ex / (1.0 + ex)

    oh1 = (lane_f == i1).astype(F32)
    oh2 = (lane_f == i2).astype(F32)
    both = oh1 + oh2
    before = jnp.dot(stril[...], both.astype(BF16), preferred_element_type=F32) + running[...]
    r1 = jnp.sum(oh1 * before, axis=-1, keepdims=True)
    r2 = jnp.sum(oh2 * before, axis=-1, keepdims=True)
    running[...] = running[...] + jnp.sum(both, axis=0, keepdims=True)

    e1 = i1 - float(MOE_GROUPS)
    e2 = i2 - float(MOE_GROUPS)
    route = jnp.zeros((tm, LANES), F32)
    for off, (a, b) in ((ROUTE_E, (e1, e2)), (ROUTE_W, (w1, w2)), (ROUTE_R, (r1, r2))):
        route = jnp.where(lane == off, a, jnp.where(lane == off + 1, b, route))
    route_ref[rows, :] = route


def _outproj(layer, yl, ys, x, wo, nw, wr, rb):
    n = x.shape[0]
    tm = TM_PROJ
    row = lambda w: pl.BlockSpec((tm, w), lambda i: (i, 0))
    return pl.pallas_call(
        _outproj_kernel,
        grid=(n // tm,),
        in_specs=[row(D_LRU), row(D_SSD), row(D_MODEL),
                  pl.BlockSpec((None,) + wo.shape[1:], lambda i: (layer, 0, 0)),
                  _const_spec((1, D_MODEL)),
                  _const_spec(wr.shape), _const_spec((1, LANES))],
        out_specs=[row(D_MODEL), row(D_HALF), row(LANES), _const_spec((1, LANES))],
        out_shape=[jax.ShapeDtypeStruct((n, D_MODEL), F32),
                   jax.ShapeDtypeStruct((n, D_HALF), U32),
                   jax.ShapeDtypeStruct((n, LANES), F32),
                   jax.ShapeDtypeStruct((1, LANES), F32)],
        scratch_shapes=[pltpu.VMEM((TM_SUB, TM_SUB), BF16), pltpu.VMEM((1, LANES), F32)],
        compiler_params=_cparams(("arbitrary",)),
        name="outproj",
    )(yl, ys, x, wo, nw, wr, rb)


SCHED_EXPERT, SCHED_FIRST, SCHED_NEXT, SCHED_SLOT = 0, 1, 2, 3


def _expert_kernel(sch_ref, nu_ref, rt_ref, rtn_ref, hp_hbm, wg_hbm, wu_hbm, wd_hbm, yb_ref,
                   hp_v, xs0, xs1, wg_f, wu_f, wd_f, wg_s, wu_s, wd_s, sem, wsem, *, layer):
    j = pl.program_id(0)
    out_rows = TM_EXP * ROW_TILES

    def weight_copies(expert, s):
        return (pltpu.make_async_copy(wg_hbm.at[layer, expert], wg_f.at[s], wsem.at[s, 0]),
                pltpu.make_async_copy(wu_hbm.at[layer, expert], wu_f.at[s], wsem.at[s, 1]),
                pltpu.make_async_copy(wd_hbm.at[layer, expert], wd_f.at[s], wsem.at[s, 2]))

    def gather(idx_ref, row, dst):
        for r in range(TM_EXP):
            dst[r:r + 1, :] = hp_v[pl.ds(idx_ref[row, r], 1), :]

    @pl.when(j == 0)
    def _():
        for cp in weight_copies(sch_ref[SCHED_EXPERT, 0], sch_ref[SCHED_SLOT, 0]):
            cp.start()
        cp = pltpu.make_async_copy(hp_hbm, hp_v, sem)
        cp.start()
        cp.wait()
        gather(rt_ref, 0, xs0)

    def block(half, src, prefetch):
        i = 2 * j + half
        e = sch_ref[SCHED_EXPERT, i]
        slot = sch_ref[SCHED_SLOT, i]
        active = i < nu_ref[0]
        out = pl.ds(half * out_rows, out_rows)

        @pl.when(active & (sch_ref[SCHED_FIRST, i] == 1))
        def _():
            for cp in weight_copies(e, slot):
                cp.wait()
            wg_s[...] = wg_f[slot].astype(BF16)
            wu_s[...] = wu_f[slot].astype(BF16)
            wd_s[...] = wd_f[slot].astype(BF16)
            nxt = sch_ref[SCHED_NEXT, i]

            @pl.when(nxt != e)
            def _():
                for cp in weight_copies(nxt, 1 - slot):
                    cp.start()

        @pl.when(active)
        def _():
            prefetch()
            packed = src[...]
            x_lo = lax.bitcast_convert_type(packed << 16, F32).astype(BF16)
            x_hi = lax.bitcast_convert_type((packed >> 16) << 16, F32).astype(BF16)
            gate = (jnp.dot(x_lo, wg_s[0:D_HALF, :], preferred_element_type=F32)
                    + jnp.dot(x_hi, wg_s[D_HALF:, :], preferred_element_type=F32))
            up = (jnp.dot(x_lo, wu_s[0:D_HALF, :], preferred_element_type=F32)
                  + jnp.dot(x_hi, wu_s[D_HALF:, :], preferred_element_type=F32))
            hid = (gate * _sigmoid(gate) * up).astype(BF16)
            y = jnp.dot(hid, wd_s[...], preferred_element_type=F32)
            for c in range(ROW_TILES):
                yb_ref[pl.ds(half * out_rows + c, TM_EXP, stride=ROW_TILES), :] = (
                    y[:, c * LANES:(c + 1) * LANES])

        @pl.when(jnp.logical_not(active))
        def _():
            yb_ref[out, :] = jnp.zeros((out_rows, LANES), F32)

    block(0, xs0, lambda: gather(rt_ref, 1, xs1))
    block(1, xs1, lambda: gather(rtn_ref, 0, xs0))


def _experts(layer, sched, n_used, row_tok, hp, wg, wu, wd):
    n_blocks = sched.shape[1]
    tm = TM_EXP
    n_steps = n_blocks // 2
    hbm = pl.BlockSpec(memory_space=pl.ANY)
    grid_spec = pltpu.PrefetchScalarGridSpec(
        num_scalar_prefetch=2,
        grid=(n_steps,),
        in_specs=[pl.BlockSpec((None, 2, tm), lambda j, sch, nu: (j, 0, 0), memory_space=pltpu.SMEM),
                  pl.BlockSpec((None, 2, tm), lambda j, sch, nu: (jnp.minimum(j + 1, n_steps - 1), 0, 0),
                               memory_space=pltpu.SMEM),
                  hbm, hbm, hbm, hbm],
        out_specs=pl.BlockSpec((2 * tm * ROW_TILES, LANES), lambda j, sch, nu: (j, 0)),
        scratch_shapes=[pltpu.VMEM(hp.shape, U32),
                        pltpu.VMEM((tm, D_HALF), U32),
                        pltpu.VMEM((tm, D_HALF), U32),
                        pltpu.VMEM((2, D_MODEL, D_EXPERT), F32),
                        pltpu.VMEM((2, D_MODEL, D_EXPERT), F32),
                        pltpu.VMEM((2, D_EXPERT, D_MODEL), F32),
                        pltpu.VMEM((D_MODEL, D_EXPERT), BF16),
                        pltpu.VMEM((D_MODEL, D_EXPERT), BF16),
                        pltpu.VMEM((D_EXPERT, D_MODEL), BF16),
                        pltpu.SemaphoreType.DMA(()),
                        pltpu.SemaphoreType.DMA((2, 3))],
    )
    return pl.pallas_call(
        functools.partial(_expert_kernel, layer=layer),
        grid_spec=grid_spec,
        out_shape=jax.ShapeDtypeStruct((n_blocks * tm * ROW_TILES, LANES), F32),
        compiler_params=_cparams(("arbitrary",)),
        name="experts",
    )(sched, n_used, row_tok.reshape(n_steps, 2, tm), row_tok.reshape(n_steps, 2, tm), hp, wg, wu, wd)


def _combine_kernel(dcur_ref, dnxt_ref, x_ref, w_ref, nw_ref, yb_ref, o_ref, gbuf, sems,
                    *, final_norm):
    t = T_TOK
    i = pl.program_id(0)
    n_steps = pl.num_programs(0)
    slot = i % 2

    def copy(dref, s, k, tok):
        src = pl.ds(pl.multiple_of(dref[k, tok] * ROW_TILES, ROW_TILES), ROW_TILES)
        row0 = tok * ROW_TILES
        dst = pl.ds(row0 if isinstance(tok, int) else pl.multiple_of(row0, ROW_TILES), ROW_TILES)
        return pltpu.make_async_copy(yb_ref.at[src, :], gbuf.at[s, k, dst, :], sems.at[s])

    def issue(dref, s):
        def body(tok, c):
            for k in range(TOP_K):
                copy(dref, s, k, tok).start()
            return c
        lax.fori_loop(0, t, body, 0, unroll=8)

    @pl.when(i == 0)
    def _():
        issue(dcur_ref, 0)

    @pl.when(i + 1 < n_steps)
    def _():
        for tok in range(t):
            for k in range(TOP_K):
                copy(dnxt_ref, 1 - slot, k, tok).start(priority=k)

    def wait(tok, c):
        for k in range(TOP_K):
            copy(dcur_ref, slot, k, tok).wait()
        return c

    lax.fori_loop(0, t, wait, 0, unroll=8)

    w0 = w_ref[:, ROUTE_W:ROUTE_W + 1]
    w1 = w_ref[:, ROUTE_W + 1:ROUTE_W + 2]
    parts = []
    for c in range(ROW_TILES):
        g0 = gbuf[slot, 0, pl.ds(c, t, stride=ROW_TILES), :]
        g1 = gbuf[slot, 1, pl.ds(c, t, stride=ROW_TILES), :]
        parts.append(x_ref[:, c * LANES:(c + 1) * LANES] + w0 * g0 + w1 * g1)
    out = jnp.concatenate(parts, axis=-1)
    if final_norm:
        out = _rms(out, nw_ref[...])
    o_ref[...] = out


def _combine(dest3, x, route, nw, yb, final_norm):
    n = x.shape[0]
    t = T_TOK
    n_steps = n // t
    dspec = lambda f: pl.BlockSpec((None, TOP_K, t), lambda i: (f(i), 0, 0), memory_space=pltpu.SMEM)
    return pl.pallas_call(
        functools.partial(_combine_kernel, final_norm=final_norm),
        grid=(n_steps,),
        in_specs=[dspec(lambda i: i), dspec(lambda i: jnp.minimum(i + 1, n_steps - 1)),
                  pl.BlockSpec((t, D_MODEL), lambda i: (i, 0)),
                  pl.BlockSpec((t, LANES), lambda i: (i, 0)),
                  _const_spec((1, D_MODEL)),
                  pl.BlockSpec(memory_space=pl.ANY)],
        out_specs=pl.BlockSpec((t, D_MODEL), lambda i: (i, 0)),
        out_shape=jax.ShapeDtypeStruct((n, D_MODEL), F32),
        scratch_shapes=[pltpu.VMEM((2, TOP_K, t * ROW_TILES, LANES), F32),
                        pltpu.SemaphoreType.DMA((2,))],
        compiler_params=_cparams(("arbitrary",)),
        name="combine",
    )(dest3, dest3, x, route, nw, yb)


T_SLOT = 1024


def _slots_kernel(route_ref, ps_ref, dest_ref):
    route = route_ref[...]
    lane_f = lax.broadcasted_iota(jnp.int32, route.shape, 1).astype(F32)
    ps = ps_ref[...]
    slots = []
    for k in range(TOP_K):
        e = route[:, ROUTE_E + k:ROUTE_E + k + 1] + float(MOE_GROUPS)
        base = jnp.sum(jnp.where(lane_f == e, ps, 0.0), axis=-1, keepdims=True)
        slots.append(base + route[:, ROUTE_R + k:ROUTE_R + k + 1])
    slab = jnp.where(lane_f == 0.0, slots[0], jnp.where(lane_f == 1.0, slots[1], 0.0))
    for i in range(T_SLOT // T_TOK):
        part = slab[i * T_TOK:(i + 1) * T_TOK, :].T
        dest_ref[i] = part[0:TOP_K, :].astype(jnp.int32)


def _slots(route, pad_start):
    n = route.shape[0]
    ps = jnp.pad(pad_start.astype(F32), (MOE_GROUPS, LANES - MOE_GROUPS - N_EXPERTS)).reshape(1, LANES)
    per = T_SLOT // T_TOK
    return pl.pallas_call(
        _slots_kernel,
        grid=(n // T_SLOT,),
        in_specs=[pl.BlockSpec((T_SLOT, LANES), lambda i: (i, 0)), _const_spec((1, LANES))],
        out_specs=pl.BlockSpec((per, TOP_K, T_TOK), lambda i: (i, 0, 0)),
        out_shape=jax.ShapeDtypeStruct((n // T_TOK, TOP_K, T_TOK), jnp.int32),
        compiler_params=_cparams(("arbitrary",)),
        name="slots",
    )(route, ps)


def _route(route, cnt, n_blocks):
    n = route.shape[0]
    counts = cnt[0, MOE_GROUPS:MOE_GROUPS + N_EXPERTS].astype(jnp.int32)
    padded = (counts + TM_EXP - 1) // TM_EXP * TM_EXP
    pad_end = jnp.cumsum(padded)
    pad_start = pad_end - padded
    dest3 = _slots(route, pad_start)
    n_used = (pad_end[-1:] // TM_EXP).astype(jnp.int32)
    blk = jnp.arange(n_blocks, dtype=jnp.int32)
    blk_expert = jnp.minimum(jnp.sum(pad_end[None, :] <= (blk * TM_EXP)[:, None], axis=1),
                             N_EXPERTS - 1).astype(jnp.int32)
    eid = jnp.arange(N_EXPERTS, dtype=jnp.int32)
    nonempty = counts > 0
    later = (eid[None, :] > eid[:, None]) & nonempty[None, :]
    next_used = jnp.min(jnp.where(later, eid[None, :], N_EXPERTS), axis=1)
    next_used = jnp.where(next_used == N_EXPERTS, eid, next_used)
    ordinal = jnp.cumsum(nonempty.astype(jnp.int32)) - 1
    first = ((blk * TM_EXP == pad_start[blk_expert]) & (blk < n_used[0])).astype(jnp.int32)
    sched = jnp.stack([blk_expert, first, next_used[blk_expert], ordinal[blk_expert] % 2])
    token = jnp.arange(n, dtype=jnp.int32).reshape(n // T_TOK, 1, T_TOK)
    token = jnp.broadcast_to(token, dest3.shape)
    row_tok = jnp.zeros((n_blocks * TM_EXP,), jnp.int32).at[dest3.reshape(-1)].add(
        token.reshape(-1), unique_indices=True)
    return dest3, sched.astype(jnp.int32), n_used, row_tok


def _gate_blocks(wa, wx):
    per = GATE_W // LRU_BW
    eye = jnp.eye(per, dtype=F32)

    def bd(w):
        w = w.reshape(LRU_HEADS // per, per, LRU_BW, LRU_BW)
        full = jnp.einsum('gpij,pq->gpiqj', w, eye)
        return full.reshape(LRU_HEADS // per, GATE_W, GATE_W)

    return jnp.concatenate([bd(wa), bd(wx)], axis=-1).astype(BF16)


def kernel(x, norm_mix, w_in, lru_conv_w, lru_conv_b, lru_wa, lru_ba, lru_wx, lru_bx, lru_lambda, lru_norm, ssd_conv_w, ssd_conv_b, ssd_dt_bias, ssd_a_log, ssd_d, ssd_norm, w_out, norm_ffn, w_coarse, b_coarse, w_fine, b_fine, w_gate, w_up, w_down, final_norm):
    bsz, seq, d = x.shape
    n = bsz * seq
    depth = w_in.shape[0]
    n_assign = n * TOP_K
    n_blocks = -(-(n_assign + N_EXPERTS * (TM_EXP - 1)) // TM_EXP)
    n_blocks += n_blocks % 2
    o_dt = 2 * D_LRU + D_SSD + SSD_XBC

    lane_pad = lambda v: jnp.pad(v, ((0, 0), (0, LANES - v.shape[1])))[:, None, :]
    row_pad = lambda w: jnp.pad(w, ((0, 0), (0, SUBLANES - w.shape[1]), (0, 0)))
    w_in_bf = w_in[:, :, :o_dt].astype(BF16)
    wdt_all = jnp.pad(w_in[:, :, o_dt:].astype(BF16), ((0, 0), (0, 0), (0, LANES - SSD_HEADS)))
    w_out_bf = w_out.astype(BF16)
    gates_all = jax.vmap(_gate_blocks)(lru_wa, lru_wx)
    lru_cw, ssd_cw = row_pad(lru_conv_w), row_pad(ssd_conv_w)
    dtb_all, alog_all = lane_pad(ssd_dt_bias), lane_pad(ssd_a_log)
    dvec_all = jnp.repeat(ssd_d, SSD_HEAD_DIM, axis=1)[:, None, :]
    w_r = jnp.concatenate([w_coarse, w_fine.transpose(0, 2, 1, 3).reshape(depth, d, N_EXPERTS)], axis=2)
    w_r_all = jnp.pad(w_r, ((0, 0), (0, 0), (0, LANES - w_r.shape[2]))).astype(BF16)
    r_bias_all = lane_pad(jnp.concatenate([b_coarse, b_fine.reshape(depth, -1)], axis=1))
    vec = lambda p, i: p[i].reshape(1, -1)

    xt = x.reshape(n, d)
    for i in range(depth):
        lx, lg, z, xbc, dt = _inproj(i, xt, vec(norm_mix, i), w_in_bf, wdt_all[i])
        y_lru = _lru(lx, lg, lru_cw[i], vec(lru_conv_b, i), gates_all[i], vec(lru_ba, i),
                     vec(lru_bx, i), vec(lru_lambda, i), vec(lru_norm, i), bsz, seq)
        y_ssd = _ssd(xbc, z, dt, ssd_cw[i], vec(ssd_conv_b, i), dtb_all[i], alog_all[i],
                     dvec_all[i], vec(ssd_norm, i), bsz, seq)
        x1, hp, route, cnt = _outproj(i, y_lru, y_ssd, xt, w_out_bf, vec(norm_ffn, i),
                                      w_r_all[i], r_bias_all[i])
        dest3, sched, n_used, row_tok = _route(route, cnt, n_blocks)
        yb = _experts(i, sched, n_used, row_tok, hp, w_gate, w_up, w_down)
        xt = _combine(dest3, x1, route, final_norm.reshape(1, d), yb, final_norm=(i == depth - 1))
    return xt.reshape(bsz, seq, d)
```

```python
import functools

import jax
import jax.numpy as jnp
from jax import lax
from jax.experimental import pallas as pl
from jax.experimental.pallas import tpu as pltpu

F32 = jnp.float32
BF16 = jnp.bfloat16
U32 = jnp.uint32

D_MODEL = 1024
D_LRU = 1024
LRU_HEADS = 16
LRU_BW = 64
RG_C = 8.0
CONV_K = 4
D_SSD = 1024
SSD_HEAD_DIM = 64
SSD_HEADS = 16
SSD_GROUPS = 4
SSD_STATE = 128
SSD_CHUNK = 128
MOE_GROUPS = 4
EXPERTS_PER_GROUP = 8
N_EXPERTS = 32
TOP_K = 2
D_EXPERT = 512
EPS = 1e-6
LOG2E = 1.4426950408889634

LANES = 128
SUBLANES = 8
VMEM_LIMIT = 60 * 1024 * 1024

TM_PROJ = 512
TM_SUB = 256
T_LRU = 512
T_SSD = 512
TM_EXP = 256
T_TOK = 128
GATE_W = 256
D_HALF = D_MODEL // 2
ROW_TILES = D_MODEL // LANES


def _cparams(sem):
    return pltpu.CompilerParams(dimension_semantics=sem, vmem_limit_bytes=VMEM_LIMIT)


def _const_spec(shape):
    n = len(shape)
    return pl.BlockSpec(shape, lambda *_: (0,) * n)


def _sigmoid(x):
    return 1.0 / (1.0 + jnp.exp(-x))


def _log1p(e):
    u = 1.0 + e
    d = u - 1.0
    return jnp.where(d == 0.0, e, jnp.log(u) * (e / jnp.where(d == 0.0, 1.0, d)))


def _softplus(x):
    return jnp.maximum(x, 0.0) + _log1p(jnp.exp(-jnp.abs(x)))


def _rms(x, w):
    ms = jnp.mean(x * x, axis=-1, keepdims=True)
    return x * lax.rsqrt(ms + EPS) * w


W_COL = 1024
SSD_XBC = D_SSD + 2 * SSD_GROUPS * SSD_STATE


def _inproj_kernel(x_ref, nw_ref, wlx_ref, wlg_ref, wz_ref, wx_ref, wbc_ref, wdt_ref,
                   lx_ref, lg_ref, z_ref, xbc_ref, dt_ref):
    h = _rms(x_ref[...], nw_ref[...]).astype(BF16)
    lx_ref[...] = jnp.dot(h, wlx_ref[...], preferred_element_type=F32).astype(BF16)
    lg_ref[...] = jnp.dot(h, wlg_ref[...], preferred_element_type=F32).astype(BF16)
    z_ref[...] = jnp.dot(h, wz_ref[...], preferred_element_type=F32).astype(BF16)
    xbc_ref[:, 0:W_COL] = jnp.dot(h, wx_ref[...], preferred_element_type=F32).astype(BF16)
    xbc_ref[:, W_COL:] = jnp.dot(h, wbc_ref[...], preferred_element_type=F32).astype(BF16)
    dt_ref[...] = jnp.dot(h, wdt_ref[...], preferred_element_type=F32)


def _inproj(layer, x, nw, w_in, wdt):
    n = x.shape[0]
    tm = TM_PROJ
    row = lambda w: pl.BlockSpec((tm, w), lambda i: (i, 0))
    wcol = lambda c: pl.BlockSpec((None, D_MODEL, W_COL), lambda i: (layer, 0, c))
    return pl.pallas_call(
        _inproj_kernel,
        grid=(n // tm,),
        in_specs=[row(D_MODEL), _const_spec((1, D_MODEL)),
                  wcol(0), wcol(1), wcol(2), wcol(3), wcol(4), _const_spec(wdt.shape)],
        out_specs=[row(D_LRU), row(D_LRU), row(D_SSD), row(SSD_XBC), row(LANES)],
        out_shape=[jax.ShapeDtypeStruct((n, D_LRU), BF16),
                   jax.ShapeDtypeStruct((n, D_LRU), BF16),
                   jax.ShapeDtypeStruct((n, D_SSD), BF16),
                   jax.ShapeDtypeStruct((n, SSD_XBC), BF16),
                   jax.ShapeDtypeStruct((n, LANES), F32)],
        compiler_params=_cparams(("arbitrary",)),
        name="inproj",
    )(x, nw, w_in, w_in, w_in, w_in, w_in, wdt)


def _causal_conv(xbuf, cw_ref, cb_ref, t):
    cw = cw_ref[...]
    n = t + SUBLANES
    full = xbuf[...]
    acc = cb_ref[...] + cw[CONV_K - 1:CONV_K, :] * full[SUBLANES:, :]
    for k in range(CONV_K - 1):
        shifted = pltpu.roll(full, n - (SUBLANES - 3 + k), 0)[0:t, :]
        acc = acc + cw[k:k + 1, :] * shifted
    return acc


def _lru_kernel(lx_ref, lg_ref, cw_ref, cb_ref, wg_ref, ba_ref, bx_ref, lam_ref, nw_ref,
                y_ref, xbuf, a_s, v_s, h_s, hcarry):
    t = T_LRU
    j = pl.program_id(1)

    @pl.when(j == 0)
    def _():
        xbuf[0:SUBLANES, :] = jnp.zeros((SUBLANES, D_LRU), F32)
        hcarry[...] = jnp.zeros_like(hcarry)

    xbuf[SUBLANES:SUBLANES + t, :] = lx_ref[...].astype(F32)
    u = _causal_conv(xbuf, cw_ref, cb_ref, t)
    xbuf[0:SUBLANES, :] = xbuf[t:t + SUBLANES, :]

    lam = lam_ref[...]
    log_sig = jnp.minimum(lam, 0.0) - _log1p(jnp.exp(-jnp.abs(lam)))
    rate = (RG_C * LOG2E) * log_sig
    for g in range(D_LRU // GATE_W):
        sl = slice(g * GATE_W, (g + 1) * GATE_W)
        ug = u[:, sl]
        gates = jnp.dot(ug.astype(BF16), wg_ref[g], preferred_element_type=F32)
        r = _sigmoid(gates[:, :GATE_W] + ba_ref[:, sl])
        i = _sigmoid(gates[:, GATE_W:] + bx_ref[:, sl])
        a = jnp.exp2(r * rate[:, sl])
        mult = jnp.sqrt(1.0 - a * a)
        a_s[:, sl] = a
        v_s[:, sl] = mult * (i * ug)

    row = lax.broadcasted_iota(jnp.int32, (SUBLANES, D_LRU), 0)

    def scan_body(k, hprev):
        r0 = pl.multiple_of(k * SUBLANES, SUBLANES)
        a = a_s[pl.ds(r0, SUBLANES), :]
        v = v_s[pl.ds(r0, SUBLANES), :]
        for s in (1, 2, 4):
            keep = row >= s
            a_sh = jnp.where(keep, pltpu.roll(a, s, 0), 1.0)
            v_sh = jnp.where(keep, pltpu.roll(v, s, 0), 0.0)
            v = v + a * v_sh
            a = a * a_sh
        h = v + a * hprev
        h_s[pl.ds(r0, SUBLANES), :] = h
        return h[SUBLANES - 1:SUBLANES, :]

    hcarry[...] = lax.fori_loop(0, t // SUBLANES, scan_body, hcarry[...], unroll=4)

    g_in = lg_ref[...].astype(F32)
    inner = g_in * (0.7978845608028654 + 0.035677408136300125 * (g_in * g_in))
    half_g = 0.5 * g_in
    gelu = half_g + half_g * jnp.tanh(inner)
    y_ref[...] = _rms(gelu * h_s[...], nw_ref[...]).astype(BF16)


def _lru(lx, lg, cw, cb, wg, ba, bx, lam, nw, bsz, seq):
    t = T_LRU
    nj = seq // t
    row = pl.BlockSpec((t, D_LRU), lambda b, j: (b * nj + j, 0))
    vec = _const_spec((1, D_LRU))
    return pl.pallas_call(
        _lru_kernel,
        grid=(bsz, nj),
        in_specs=[row, row, _const_spec(cw.shape), vec, _const_spec(wg.shape),
                  vec, vec, vec, vec],
        out_specs=row,
        out_shape=jax.ShapeDtypeStruct((bsz * seq, D_LRU), BF16),
        scratch_shapes=[pltpu.VMEM((t + SUBLANES, D_LRU), F32),
                        pltpu.VMEM((t, D_LRU), F32),
                        pltpu.VMEM((t, D_LRU), F32),
                        pltpu.VMEM((t, D_LRU), F32),
                        pltpu.VMEM((1, D_LRU), F32)],
        compiler_params=_cparams(("arbitrary", "arbitrary")),
        name="rglru",
    )(lx, lg, cw, cb, wg, ba, bx, lam, nw)


def _split3(x):
    hi = x.astype(BF16)
    r1 = x - hi.astype(F32)
    mid = r1.astype(BF16)
    lo = (r1 - mid.astype(F32)).astype(BF16)
    return hi, mid, lo


def _ssd_kernel(xbc_ref, z_ref, dt_ref, cw_ref, cb_ref, dtb_ref, alog_ref, dvec_ref, nw_ref,
                tril_ref, fut_ref, y_ref, xbuf, xc_s, state, y_s):
    tt = xbc_ref.shape[0]
    j = pl.program_id(1)

    @pl.when(j == 0)
    def _():
        xbuf[0:SUBLANES, :] = jnp.zeros((SUBLANES, xbuf.shape[1]), F32)
        state[...] = jnp.zeros_like(state)

    xbuf[SUBLANES:SUBLANES + tt, :] = xbc_ref[...].astype(F32)
    u = _causal_conv(xbuf, cw_ref, cb_ref, tt)
    xbuf[0:SUBLANES, :] = xbuf[tt:tt + SUBLANES, :]
    xc_s[...] = u * _sigmoid(u)
    for c in range(tt // SSD_CHUNK):
        _ssd_chunk(c, xc_s, dt_ref, dtb_ref, alog_ref, tril_ref, fut_ref, state, y_s)

    zf = z_ref[...].astype(F32)
    y = (y_s[...] + xc_s[:, 0:D_SSD] * dvec_ref[...]) * (zf * _sigmoid(zf))
    y_ref[...] = _rms(y, nw_ref[...]).astype(BF16)


def _ssd_chunk(c, xc_s, dt_ref, dtb_ref, alog_ref, tril_ref, fut_ref, state, y_s):
    t = SSD_CHUNK
    rows = slice(c * t, (c + 1) * t)
    gn = SSD_GROUPS * SSD_STATE
    xs = xc_s[rows, 0:D_SSD]
    bm = xc_s[rows, D_SSD:D_SSD + gn]
    cm = xc_s[rows, D_SSD + gn:]

    dt = _softplus(dt_ref[rows, :] + dtb_ref[...])
    d_a = dt * (-jnp.exp(alog_ref[...]))
    tril = tril_ref[...]
    future = fut_ref[...]
    hi, mid, lo = _split3(d_a)
    a_cs = (jnp.dot(tril, hi, preferred_element_type=F32)
            + jnp.dot(tril, mid, preferred_element_type=F32)
            + jnp.dot(tril, lo, preferred_element_type=F32))
    a_cs_t = a_cs.T
    dt_t = dt.T
    a_last_t = a_cs_t[:, t - 1:t]
    w_state_t = jnp.exp(a_last_t - a_cs_t) * dt_t
    chunk_decay_t = jnp.exp(a_last_t)
    src_t = a_cs_t - jnp.log(dt_t)

    lane = lax.broadcasted_iota(jnp.int32, (1, LANES), 1)
    first = lane < SSD_HEAD_DIM

    heads_per_group = SSD_HEADS // SSD_GROUPS
    for g in range(SSD_GROUPS):
        gsl = slice(g * SSD_STATE, (g + 1) * SSD_STATE)
        c_g = cm[:, gsl].astype(BF16)
        b_g = bm[:, gsl]
        scores = lax.dot_general(c_g, b_g.astype(BF16), (((1,), (1,)), ((), ())),
                                 preferred_element_type=F32)
        b_t = b_g.T
        st_g = state[g]
        y_off = jnp.dot(c_g, st_g.astype(BF16), preferred_element_type=F32)
        for q in range(heads_per_group // 2):
            h0 = g * heads_per_group + 2 * q
            psl = slice(h0 * SSD_HEAD_DIM, (h0 + 2) * SSD_HEAD_DIM)
            lsl = slice(2 * q * SSD_HEAD_DIM, (2 * q + 2) * SSD_HEAD_DIM)
            x_pair = xs[:, psl].astype(BF16)
            yd, ns, cols, cd = [], [], [], []
            for h in (h0, h0 + 1):
                col = jnp.broadcast_to(a_cs[:, h:h + 1], (t, t))
                lmat_dt = jnp.exp(col - src_t[h:h + 1, :] + future)
                m = (scores * lmat_dt).astype(BF16)
                yd.append(jnp.dot(m, x_pair, preferred_element_type=F32))
                bw = (b_t * w_state_t[h:h + 1, :]).astype(BF16)
                ns.append(jnp.dot(bw, x_pair, preferred_element_type=F32))
                cols.append(col)
                cd.append(jnp.broadcast_to(chunk_decay_t[h:h + 1, :], (1, LANES)))
            y_pair = (jnp.where(first, yd[0], yd[1])
                      + jnp.exp(jnp.where(first, cols[0], cols[1])) * y_off[:, lsl])
            y_s[rows, psl] = y_pair
            state[g, :, lsl] = (st_g[:, lsl] * jnp.where(first, cd[0], cd[1])
                                + jnp.where(first, ns[0], ns[1]))


def _ssd(xbc, z, dt, cw, cb, dtb, alog, dvec, nw, bsz, seq):
    t = T_SSD
    nj = seq // t
    dx = xbc.shape[1]
    row = lambda w: pl.BlockSpec((t, w), lambda b, j: (b * nj + j, 0))
    ch = SSD_CHUNK
    causal = jnp.arange(ch)[:, None] >= jnp.arange(ch)[None, :]
    return pl.pallas_call(
        _ssd_kernel,
        grid=(bsz, nj),
        in_specs=[row(dx), row(D_SSD), row(LANES), _const_spec(cw.shape), _const_spec((1, dx)),
                  _const_spec((1, LANES)), _const_spec((1, LANES)),
                  _const_spec((1, D_SSD)), _const_spec((1, D_SSD)),
                  _const_spec((ch, ch)), _const_spec((ch, ch))],
        out_specs=row(D_SSD),
        out_shape=jax.ShapeDtypeStruct((bsz * seq, D_SSD), BF16),
        scratch_shapes=[pltpu.VMEM((t + SUBLANES, dx), F32),
                        pltpu.VMEM((t, dx), F32),
                        pltpu.VMEM((SSD_GROUPS, SSD_STATE, D_SSD // SSD_GROUPS), F32),
                        pltpu.VMEM((t, D_SSD), F32)],
        compiler_params=_cparams(("arbitrary", "arbitrary")),
        name="ssd",
    )(xbc, z, dt, cw, cb, dtb, alog, dvec, nw, causal.astype(BF16),
      jnp.where(causal, 0.0, NEG_BIG).astype(F32))


ROUTE_E, ROUTE_W, ROUTE_R = 0, 2, 4
NEG_BIG = -1e30


def _first_argmax(vals, lane_f):
    m = jnp.max(vals, axis=-1, keepdims=True)
    idx = jnp.min(jnp.where(vals == m, lane_f, float(LANES)), axis=-1, keepdims=True)
    return m, idx


def _outproj_kernel(yl_ref, ys_ref, x_ref, wo_ref, nw_ref, wr_ref, rb_ref,
                    x1_ref, hp_ref, route_ref, cnt_ref, stril, running):
    @pl.when(pl.program_id(0) == 0)
    def _():
        ri = lax.broadcasted_iota(jnp.int32, (TM_SUB, TM_SUB), 0)
        ci = lax.broadcasted_iota(jnp.int32, (TM_SUB, TM_SUB), 1)
        stril[...] = (ri > ci).astype(BF16)
        running[...] = jnp.zeros_like(running)

    for s in range(x_ref.shape[0] // TM_SUB):
        _outproj_rows(pl.ds(s * TM_SUB, TM_SUB), yl_ref, ys_ref, x_ref, wo_ref, nw_ref, wr_ref, rb_ref,
                      x1_ref, hp_ref, route_ref, stril, running)
    cnt_ref[...] = running[...]


def _outproj_rows(rows, yl_ref, ys_ref, x_ref, wo_ref, nw_ref, wr_ref, rb_ref,
                  x1_ref, hp_ref, route_ref, stril, running):
    tm = TM_SUB
    y_cat = jnp.concatenate([yl_ref[rows, :], ys_ref[rows, :]], axis=1)
    x1 = x_ref[rows, :] + jnp.dot(y_cat, wo_ref[...], preferred_element_type=F32)
    x1_ref[rows, :] = x1
    h = _rms(x1, nw_ref[...])
    h_hi = h.astype(BF16)
    h_rt = h_hi.astype(F32)
    bits = lax.bitcast_convert_type(h_rt, U32)
    hp_ref[rows, :] = bits[:, D_HALF:] | (bits[:, :D_HALF] >> 16)
    logits = jnp.dot(h_hi, wr_ref[...], preferred_element_type=F32) + rb_ref[...]

    lane = lax.broadcasted_iota(jnp.int32, (tm, LANES), 1)
    lane_f = lane.astype(F32)
    is_c = lane < MOE_GROUPS
    lc = jnp.where(is_c, logits, NEG_BIG)
    m_c, g_idx = _first_argmax(lc, lane_f)
    g_w = 1.0 / jnp.sum(jnp.where(is_c, jnp.exp(lc - m_c), 0.0), axis=-1, keepdims=True)
    lo = float(MOE_GROUPS) + float(EXPERTS_PER_GROUP) * g_idx
    is_f = (lane_f >= lo) & (lane_f < lo + float(EXPERTS_PER_GROUP))
    lf = jnp.where(is_f, logits, NEG_BIG)
    v1, i1 = _first_argmax(lf, lane_f)
    v2, i2 = _first_argmax(jnp.where(lane_f == i1, NEG_BIG, lf), lane_f)
    ex = jnp.exp(v2 - v1)
    w1 = g_w / (1.0 + ex)
    w2 = g_w * ex / (1.0 + ex)

    oh1 = (lane_f == i1).astype(F32)
    oh2 = (lane_f == i2).astype(F32)
    both = oh1 + oh2
    before = jnp.dot(stril[...], both.astype(BF16), preferred_element_type=F32) + running[...]
    r1 = jnp.sum(oh1 * before, axis=-1, keepdims=True)
    r2 = jnp.sum(oh2 * before, axis=-1, keepdims=True)
    running[...] = running[...] + jnp.sum(both, axis=0, keepdims=True)

    e1 = i1 - float(MOE_GROUPS)
    e2 = i2 - float(MOE_GROUPS)
    route = jnp.zeros((tm, LANES), F32)
    for off, (a, b) in ((ROUTE_E, (e1, e2)), (ROUTE_W, (w1, w2)), (ROUTE_R, (r1, r2))):
        route = jnp.where(lane == off, a, jnp.where(lane == off + 1, b, route))
    route_ref[rows, :] = route


def _outproj(layer, yl, ys, x, wo, nw, wr, rb):
    n = x.shape[0]
    tm = TM_PROJ
    row = lambda w: pl.BlockSpec((tm, w), lambda i: (i, 0))
    return pl.pallas_call(
        _outproj_kernel,
        grid=(n // tm,),
        in_specs=[row(D_LRU), row(D_SSD), row(D_MODEL),
                  pl.BlockSpec((None,) + wo.shape[1:], lambda i: (layer, 0, 0)),
                  _const_spec((1, D_MODEL)),
                  _const_spec(wr.shape), _const_spec((1, LANES))],
        out_specs=[row(D_MODEL), row(D_HALF), row(LANES), _const_spec((1, LANES))],
        out_shape=[jax.ShapeDtypeStruct((n, D_MODEL), F32),
                   jax.ShapeDtypeStruct((n, D_HALF), U32),
                   jax.ShapeDtypeStruct((n, LANES), F32),
                   jax.ShapeDtypeStruct((1, LANES), F32)],
        scratch_shapes=[pltpu.VMEM((TM_SUB, TM_SUB), BF16), pltpu.VMEM((1, LANES), F32)],
        compiler_params=_cparams(("arbitrary",)),
        name="outproj",
    )(yl, ys, x, wo, nw, wr, rb)


SCHED_EXPERT, SCHED_FIRST, SCHED_NEXT, SCHED_SLOT = 0, 1, 2, 3


def _expert_kernel(sch_ref, nu_ref, rt_ref, rtn_ref, hp_hbm, wg_hbm, wu_hbm, wd_hbm, yb_ref,
                   hp_v, xs0, xs1, wg_f, wu_f, wd_f, wg_s, wu_s, wd_s, sem, wsem, *, layer):
    j = pl.program_id(0)
    out_rows = TM_EXP * ROW_TILES

    def weight_copies(expert, s):
        return (pltpu.make_async_copy(wg_hbm.at[layer, expert], wg_f.at[s], wsem.at[s, 0]),
                pltpu.make_async_copy(wu_hbm.at[layer, expert], wu_f.at[s], wsem.at[s, 1]),
                pltpu.make_async_copy(wd_hbm.at[layer, expert], wd_f.at[s], wsem.at[s, 2]))

    def gather(idx_ref, row, dst):
        for r in range(TM_EXP):
            dst[r:r + 1, :] = hp_v[pl.ds(idx_ref[row, r], 1), :]

    @pl.when(j == 0)
    def _():
        for cp in weight_copies(sch_ref[SCHED_EXPERT, 0], sch_ref[SCHED_SLOT, 0]):
            cp.start()
        cp = pltpu.make_async_copy(hp_hbm, hp_v, sem)
        cp.start()
        cp.wait()
        gather(rt_ref, 0, xs0)

    def block(half, src, prefetch):
        i = 2 * j + half
        e = sch_ref[SCHED_EXPERT, i]
        slot = sch_ref[SCHED_SLOT, i]
        active = i < nu_ref[0]
        out = pl.ds(half * out_rows, out_rows)

        @pl.when(active & (sch_ref[SCHED_FIRST, i] == 1))
        def _():
            for cp in weight_copies(e, slot):
                cp.wait()
            wg_s[...] = wg_f[slot].astype(BF16)
            wu_s[...] = wu_f[slot].astype(BF16)
            wd_s[...] = wd_f[slot].astype(BF16)
            nxt = sch_ref[SCHED_NEXT, i]

            @pl.when(nxt != e)
            def _():
                for cp in weight_copies(nxt, 1 - slot):
                    cp.start()

        @pl.when(active)
        def _():
            prefetch()
            packed = src[...]
            x_lo = lax.bitcast_convert_type(packed << 16, F32).astype(BF16)
            x_hi = lax.bitcast_convert_type((packed >> 16) << 16, F32).astype(BF16)
            gate = (jnp.dot(x_lo, wg_s[0:D_HALF, :], preferred_element_type=F32)
                    + jnp.dot(x_hi, wg_s[D_HALF:, :], preferred_element_type=F32))
            up = (jnp.dot(x_lo, wu_s[0:D_HALF, :], preferred_element_type=F32)
                  + jnp.dot(x_hi, wu_s[D_HALF:, :], preferred_element_type=F32))
            hid = (gate * _sigmoid(gate) * up).astype(BF16)
            y = jnp.dot(hid, wd_s[...], preferred_element_type=F32)
            for c in range(ROW_TILES):
                yb_ref[pl.ds(half * out_rows + c, TM_EXP, stride=ROW_TILES), :] = (
                    y[:, c * LANES:(c + 1) * LANES])

        @pl.when(jnp.logical_not(active))
        def _():
            yb_ref[out, :] = jnp.zeros((out_rows, LANES), F32)

    block(0, xs0, lambda: gather(rt_ref, 1, xs1))
    block(1, xs1, lambda: gather(rtn_ref, 0, xs0))


def _experts(layer, sched, n_used, row_tok, hp, wg, wu, wd):
    n_blocks = sched.shape[1]
    tm = TM_EXP
    n_steps = n_blocks // 2
    hbm = pl.BlockSpec(memory_space=pl.ANY)
    grid_spec = pltpu.PrefetchScalarGridSpec(
        num_scalar_prefetch=2,
        grid=(n_steps,),
        in_specs=[pl.BlockSpec((None, 2, tm), lambda j, sch, nu: (j, 0, 0), memory_space=pltpu.SMEM),
                  pl.BlockSpec((None, 2, tm), lambda j, sch, nu: (jnp.minimum(j + 1, n_steps - 1), 0, 0),
                               memory_space=pltpu.SMEM),
                  hbm, hbm, hbm, hbm],
        out_specs=pl.BlockSpec((2 * tm * ROW_TILES, LANES), lambda j, sch, nu: (j, 0)),
        scratch_shapes=[pltpu.VMEM(hp.shape, U32),
                        pltpu.VMEM((tm, D_HALF), U32),
                        pltpu.VMEM((tm, D_HALF), U32),
                        pltpu.VMEM((2, D_MODEL, D_EXPERT), F32),
                        pltpu.VMEM((2, D_MODEL, D_EXPERT), F32),
                        pltpu.VMEM((2, D_EXPERT, D_MODEL), F32),
                        pltpu.VMEM((D_MODEL, D_EXPERT), BF16),
                        pltpu.VMEM((D_MODEL, D_EXPERT), BF16),
                        pltpu.VMEM((D_EXPERT, D_MODEL), BF16),
                        pltpu.SemaphoreType.DMA(()),
                        pltpu.SemaphoreType.DMA((2, 3))],
    )
    return pl.pallas_call(
        functools.partial(_expert_kernel, layer=layer),
        grid_spec=grid_spec,
        out_shape=jax.ShapeDtypeStruct((n_blocks * tm * ROW_TILES, LANES), F32),
        compiler_params=_cparams(("arbitrary",)),
        name="experts",
    )(sched, n_used, row_tok.reshape(n_steps, 2, tm), row_tok.reshape(n_steps, 2, tm), hp, wg, wu, wd)


def _combine_kernel(dcur_ref, dnxt_ref, x_ref, w_ref, nw_ref, yb_ref, o_ref, gbuf, sems,
                    *, final_norm):
    t = T_TOK
    i = pl.program_id(0)
    n_steps = pl.num_programs(0)
    slot = i % 2

    def copy(dref, s, k, tok):
        src = pl.ds(pl.multiple_of(dref[k, tok] * ROW_TILES, ROW_TILES), ROW_TILES)
        row0 = tok * ROW_TILES
        dst = pl.ds(row0 if isinstance(tok, int) else pl.multiple_of(row0, ROW_TILES), ROW_TILES)
        return pltpu.make_async_copy(yb_ref.at[src, :], gbuf.at[s, k, dst, :], sems.at[s])

    def issue(dref, s):
        def body(tok, c):
            for k in range(TOP_K):
                copy(dref, s, k, tok).start()
            return c
        lax.fori_loop(0, t, body, 0, unroll=8)

    @pl.when(i == 0)
    def _():
        issue(dcur_ref, 0)

    @pl.when(i + 1 < n_steps)
    def _():
        for tok in range(t):
            for k in range(TOP_K):
                copy(dnxt_ref, 1 - slot, k, tok).start(priority=k)

    def wait(tok, c):
        for k in range(TOP_K):
            copy(dcur_ref, slot, k, tok).wait()
        return c

    lax.fori_loop(0, t, wait, 0, unroll=8)

    w0 = w_ref[:, ROUTE_W:ROUTE_W + 1]
    w1 = w_ref[:, ROUTE_W + 1:ROUTE_W + 2]
    parts = []
    for c in range(ROW_TILES):
        g0 = gbuf[slot, 0, pl.ds(c, t, stride=ROW_TILES), :]
        g1 = gbuf[slot, 1, pl.ds(c, t, stride=ROW_TILES), :]
        parts.append(x_ref[:, c * LANES:(c + 1) * LANES] + w0 * g0 + w1 * g1)
    out = jnp.concatenate(parts, axis=-1)
    if final_norm:
        out = _rms(out, nw_ref[...])
    o_ref[...] = out


def _combine(dest3, x, route, nw, yb, final_norm):
    n = x.shape[0]
    t = T_TOK
    n_steps = n // t
    dspec = lambda f: pl.BlockSpec((None, TOP_K, t), lambda i: (f(i), 0, 0), memory_space=pltpu.SMEM)
    return pl.pallas_call(
        functools.partial(_combine_kernel, final_norm=final_norm),
        grid=(n_steps,),
        in_specs=[dspec(lambda i: i), dspec(lambda i: jnp.minimum(i + 1, n_steps - 1)),
                  pl.BlockSpec((t, D_MODEL), lambda i: (i, 0)),
                  pl.BlockSpec((t, LANES), lambda i: (i, 0)),
                  _const_spec((1, D_MODEL)),
                  pl.BlockSpec(memory_space=pl.ANY)],
        out_specs=pl.BlockSpec((t, D_MODEL), lambda i: (i, 0)),
        out_shape=jax.ShapeDtypeStruct((n, D_MODEL), F32),
        scratch_shapes=[pltpu.VMEM((2, TOP_K, t * ROW_TILES, LANES), F32),
                        pltpu.SemaphoreType.DMA((2,))],
        compiler_params=_cparams(("arbitrary",)),
        name="combine",
    )(dest3, dest3, x, route, nw, yb)


T_SLOT = 1024


def _slots_kernel(route_ref, ps_ref, dest_ref):
    route = route_ref[...]
    lane_f = lax.broadcasted_iota(jnp.int32, route.shape, 1).astype(F32)
    ps = ps_ref[...]
    slots = []
    for k in range(TOP_K):
        e = route[:, ROUTE_E + k:ROUTE_E + k + 1] + float(MOE_GROUPS)
        base = jnp.sum(jnp.where(lane_f == e, ps, 0.0), axis=-1, keepdims=True)
        slots.append(base + route[:, ROUTE_R + k:ROUTE_R + k + 1])
    slab = jnp.where(lane_f == 0.0, slots[0], jnp.where(lane_f == 1.0, slots[1], 0.0))
    for i in range(T_SLOT // T_TOK):
        part = slab[i * T_TOK:(i + 1) * T_TOK, :].T
        dest_ref[i] = part[0:TOP_K, :].astype(jnp.int32)


def _slots(route, pad_start):
    n = route.shape[0]
    ps = jnp.pad(pad_start.astype(F32), (MOE_GROUPS, LANES - MOE_GROUPS - N_EXPERTS)).reshape(1, LANES)
    per = T_SLOT // T_TOK
    return pl.pallas_call(
        _slots_kernel,
        grid=(n // T_SLOT,),
        in_specs=[pl.BlockSpec((T_SLOT, LANES), lambda i: (i, 0)), _const_spec((1, LANES))],
        out_specs=pl.BlockSpec((per, TOP_K, T_TOK), lambda i: (i, 0, 0)),
        out_shape=jax.ShapeDtypeStruct((n // T_TOK, TOP_K, T_TOK), jnp.int32),
        compiler_params=_cparams(("arbitrary",)),
        name="slots",
    )(route, ps)


def _route(route, cnt, n_blocks):
    n = route.shape[0]
    counts = cnt[0, MOE_GROUPS:MOE_GROUPS + N_EXPERTS].astype(jnp.int32)
    padded = (counts + TM_EXP - 1) // TM_EXP * TM_EXP
    pad_end = jnp.cumsum(padded)
    pad_start = pad_end - padded
    dest3 = _slots(route, pad_start)
    n_used = (pad_end[-1:] // TM_EXP).astype(jnp.int32)
    blk = jnp.arange(n_blocks, dtype=jnp.int32)
    blk_expert = jnp.minimum(jnp.sum(pad_end[None, :] <= (blk * TM_EXP)[:, None], axis=1),
                             N_EXPERTS - 1).astype(jnp.int32)
    eid = jnp.arange(N_EXPERTS, dtype=jnp.int32)
    nonempty = counts > 0
    later = (eid[None, :] > eid[:, None]) & nonempty[None, :]
    next_used = jnp.min(jnp.where(later, eid[None, :], N_EXPERTS), axis=1)
    next_used = jnp.where(next_used == N_EXPERTS, eid, next_used)
    ordinal = jnp.cumsum(nonempty.astype(jnp.int32)) - 1
    first = ((blk * TM_EXP == pad_start[blk_expert]) & (blk < n_used[0])).astype(jnp.int32)
    sched = jnp.stack([blk_expert, first, next_used[blk_expert], ordinal[blk_expert] % 2])
    token = jnp.arange(n, dtype=jnp.int32).reshape(n // T_TOK, 1, T_TOK)
    token = jnp.broadcast_to(token, dest3.shape)
    row_tok = jnp.zeros((n_blocks * TM_EXP,), jnp.int32).at[dest3.reshape(-1)].add(
        token.reshape(-1), unique_indices=True)
    return dest3, sched.astype(jnp.int32), n_used, row_tok


def _gate_blocks(wa, wx):
    per = GATE_W // LRU_BW
    eye = jnp.eye(per, dtype=F32)

    def bd(w):
        w = w.reshape(LRU_HEADS // per, per, LRU_BW, LRU_BW)
        full = jnp.einsum('gpij,pq->gpiqj', w, eye)
        return full.reshape(LRU_HEADS // per, GATE_W, GATE_W)

    return jnp.concatenate([bd(wa), bd(wx)], axis=-1).astype(BF16)


def kernel(x, norm_mix, w_in, lru_conv_w, lru_conv_b, lru_wa, lru_ba, lru_wx, lru_bx, lru_lambda, lru_norm, ssd_conv_w, ssd_conv_b, ssd_dt_bias, ssd_a_log, ssd_d, ssd_norm, w_out, norm_ffn, w_coarse, b_coarse, w_fine, b_fine, w_gate, w_up, w_down, final_norm):
    bsz, seq, d = x.shape
    n = bsz * seq
    depth = w_in.shape[0]
    n_assign = n * TOP_K
    n_blocks = -(-(n_assign + N_EXPERTS * (TM_EXP - 1)) // TM_EXP)
    n_blocks += n_blocks % 2
    o_dt = 2 * D_LRU + D_SSD + SSD_XBC

    lane_pad = lambda v: jnp.pad(v, ((0, 0), (0, LANES - v.shape[1])))[:, None, :]
    row_pad = lambda w: jnp.pad(w, ((0, 0), (0, SUBLANES - w.shape[1]), (0, 0)))
    w_in_bf = w_in[:, :, :o_dt].astype(BF16)
    wdt_all = jnp.pad(w_in[:, :, o_dt:].astype(BF16), ((0, 0), (0, 0), (0, LANES - SSD_HEADS)))
    w_out_bf = w_out.astype(BF16)
    gates_all = jax.vmap(_gate_blocks)(lru_wa, lru_wx)
    lru_cw, ssd_cw = row_pad(lru_conv_w), row_pad(ssd_conv_w)
    dtb_all, alog_all = lane_pad(ssd_dt_bias), lane_pad(ssd_a_log)
    dvec_all = jnp.repeat(ssd_d, SSD_HEAD_DIM, axis=1)[:, None, :]
    w_r = jnp.concatenate([w_coarse, w_fine.transpose(0, 2, 1, 3).reshape(depth, d, N_EXPERTS)], axis=2)
    w_r_all = jnp.pad(w_r, ((0, 0), (0, 0), (0, LANES - w_r.shape[2]))).astype(BF16)
    r_bias_all = lane_pad(jnp.concatenate([b_coarse, b_fine.reshape(depth, -1)], axis=1))
    vec = lambda p, i: p[i].reshape(1, -1)

    xt = x.reshape(n, d)
    for i in range(depth):
        lx, lg, z, xbc, dt = _inproj(i, xt, vec(norm_mix, i), w_in_bf, wdt_all[i])
        y_lru = _lru(lx, lg, lru_cw[i], vec(lru_conv_b, i), gates_all[i], vec(lru_ba, i),
                     vec(lru_bx, i), vec(lru_lambda, i), vec(lru_norm, i), bsz, seq)
        y_ssd = _ssd(xbc, z, dt, ssd_cw[i], vec(ssd_conv_b, i), dtb_all[i], alog_all[i],
                     dvec_all[i], vec(ssd_norm, i), bsz, seq)
        x1, hp, route, cnt = _outproj(i, y_lru, y_ssd, xt, w_out_bf, vec(norm_ffn, i),
                                      w_r_all[i], r_bias_all[i])
        dest3, sched, n_used, row_tok = _route(route, cnt, n_blocks)
        yb = _experts(i, sched, n_used, row_tok, hp, w_gate, w_up, w_down)
        xt = _combine(dest3, x1, route, final_norm.reshape(1, d), yb, final_norm=(i == depth - 1))
    return xt.reshape(bsz, seq, d)
```

```python
import functools

import jax
import jax.numpy as jnp
from jax import lax
from jax.experimental import pallas as pl
from jax.experimental.pallas import tpu as pltpu

F32 = jnp.float32
BF16 = jnp.bfloat16
U32 = jnp.uint32

D_MODEL = 1024
D_LRU = 1024
LRU_HEADS = 16
LRU_BW = 64
RG_C = 8.0
CONV_K = 4
D_SSD = 1024
SSD_HEAD_DIM = 64
SSD_HEADS = 16
SSD_GROUPS = 4
SSD_STATE = 128
SSD_CHUNK = 128
MOE_GROUPS = 4
EXPERTS_PER_GROUP = 8
N_EXPERTS = 32
TOP_K = 2
D_EXPERT = 512
EPS = 1e-6
LOG2E = 1.4426950408889634

LANES = 128
SUBLANES = 8
VMEM_LIMIT = 60 * 1024 * 1024

TM_PROJ = 512
TM_SUB = 256
T_LRU = 512
T_SSD = 512
TM_EXP = 256
T_TOK = 256
GATE_W = 256
D_HALF = D_MODEL // 2
ROW_TILES = D_MODEL // LANES


def _cparams(sem):
    return pltpu.CompilerParams(dimension_semantics=sem, vmem_limit_bytes=VMEM_LIMIT)


def _const_spec(shape):
    n = len(shape)
    return pl.BlockSpec(shape, lambda *_: (0,) * n)


def _sigmoid(x):
    return 1.0 / (1.0 + jnp.exp(-x))


def _log1p(e):
    u = 1.0 + e
    d = u - 1.0
    return jnp.where(d == 0.0, e, jnp.log(u) * (e / jnp.where(d == 0.0, 1.0, d)))


def _softplus(x):
    return jnp.maximum(x, 0.0) + _log1p(jnp.exp(-jnp.abs(x)))


def _rms(x, w):
    ms = jnp.mean(x * x, axis=-1, keepdims=True)
    return x * lax.rsqrt(ms + EPS) * w


W_COL = 1024
SSD_XBC = D_SSD + 2 * SSD_GROUPS * SSD_STATE


def _inproj_kernel(x_ref, nw_ref, wlx_ref, wlg_ref, wz_ref, wx_ref, wbc_ref, wdt_ref,
                   lx_ref, lg_ref, z_ref, xbc_ref, dt_ref):
    h = _rms(x_ref[...], nw_ref[...]).astype(BF16)
    lx_ref[...] = jnp.dot(h, wlx_ref[...], preferred_element_type=F32).astype(BF16)
    lg_ref[...] = jnp.dot(h, wlg_ref[...], preferred_element_type=F32).astype(BF16)
    z_ref[...] = jnp.dot(h, wz_ref[...], preferred_element_type=F32).astype(BF16)
    xbc_ref[:, 0:W_COL] = jnp.dot(h, wx_ref[...], preferred_element_type=F32).astype(BF16)
    xbc_ref[:, W_COL:] = jnp.dot(h, wbc_ref[...], preferred_element_type=F32).astype(BF16)
    dt_ref[...] = jnp.dot(h, wdt_ref[...], preferred_element_type=F32)


def _inproj(layer, x, nw, w_in, wdt):
    n = x.shape[0]
    tm = TM_PROJ
    row = lambda w: pl.BlockSpec((tm, w), lambda i: (i, 0))
    wcol = lambda c: pl.BlockSpec((None, D_MODEL, W_COL), lambda i: (layer, 0, c))
    return pl.pallas_call(
        _inproj_kernel,
        grid=(n // tm,),
        in_specs=[row(D_MODEL), _const_spec((1, D_MODEL)),
                  wcol(0), wcol(1), wcol(2), wcol(3), wcol(4), _const_spec(wdt.shape)],
        out_specs=[row(D_LRU), row(D_LRU), row(D_SSD), row(SSD_XBC), row(LANES)],
        out_shape=[jax.ShapeDtypeStruct((n, D_LRU), BF16),
                   jax.ShapeDtypeStruct((n, D_LRU), BF16),
                   jax.ShapeDtypeStruct((n, D_SSD), BF16),
                   jax.ShapeDtypeStruct((n, SSD_XBC), BF16),
                   jax.ShapeDtypeStruct((n, LANES), F32)],
        compiler_params=_cparams(("arbitrary",)),
        name="inproj",
    )(x, nw, w_in, w_in, w_in, w_in, w_in, wdt)


def _causal_conv(xbuf, cw_ref, cb_ref, t):
    cw = cw_ref[...]
    n = t + SUBLANES
    full = xbuf[...]
    acc = cb_ref[...] + cw[CONV_K - 1:CONV_K, :] * full[SUBLANES:, :]
    for k in range(CONV_K - 1):
        shifted = pltpu.roll(full, n - (SUBLANES - 3 + k), 0)[0:t, :]
        acc = acc + cw[k:k + 1, :] * shifted
    return acc


def _lru_kernel(lx_ref, lg_ref, cw_ref, cb_ref, wg_ref, ba_ref, bx_ref, lam_ref, nw_ref,
                y_ref, xbuf, a_s, v_s, h_s, hcarry):
    t = T_LRU
    j = pl.program_id(1)

    @pl.when(j == 0)
    def _():
        xbuf[0:SUBLANES, :] = jnp.zeros((SUBLANES, D_LRU), F32)
        hcarry[...] = jnp.zeros_like(hcarry)

    xbuf[SUBLANES:SUBLANES + t, :] = lx_ref[...].astype(F32)
    u = _causal_conv(xbuf, cw_ref, cb_ref, t)
    xbuf[0:SUBLANES, :] = xbuf[t:t + SUBLANES, :]

    lam = lam_ref[...]
    log_sig = jnp.minimum(lam, 0.0) - _log1p(jnp.exp(-jnp.abs(lam)))
    rate = (RG_C * LOG2E) * log_sig
    for g in range(D_LRU // GATE_W):
        sl = slice(g * GATE_W, (g + 1) * GATE_W)
        ug = u[:, sl]
        gates = jnp.dot(ug.astype(BF16), wg_ref[g], preferred_element_type=F32)
        r = _sigmoid(gates[:, :GATE_W] + ba_ref[:, sl])
        i = _sigmoid(gates[:, GATE_W:] + bx_ref[:, sl])
        a = jnp.exp2(r * rate[:, sl])
        mult = jnp.sqrt(1.0 - a * a)
        a_s[:, sl] = a
        v_s[:, sl] = mult * (i * ug)

    row = lax.broadcasted_iota(jnp.int32, (SUBLANES, D_LRU), 0)

    def scan_body(k, hprev):
        r0 = pl.multiple_of(k * SUBLANES, SUBLANES)
        a = a_s[pl.ds(r0, SUBLANES), :]
        v = v_s[pl.ds(r0, SUBLANES), :]
        for s in (1, 2, 4):
            keep = row >= s
            a_sh = jnp.where(keep, pltpu.roll(a, s, 0), 1.0)
            v_sh = jnp.where(keep, pltpu.roll(v, s, 0), 0.0)
            v = v + a * v_sh
            a = a * a_sh
        h = v + a * hprev
        h_s[pl.ds(r0, SUBLANES), :] = h
        return h[SUBLANES - 1:SUBLANES, :]

    hcarry[...] = lax.fori_loop(0, t // SUBLANES, scan_body, hcarry[...], unroll=True)

    g_in = lg_ref[...].astype(F32)
    inner = g_in * (0.7978845608028654 + 0.035677408136300125 * (g_in * g_in))
    half_g = 0.5 * g_in
    gelu = half_g + half_g * jnp.tanh(inner)
    y_ref[...] = _rms(gelu * h_s[...], nw_ref[...]).astype(BF16)


def _lru(lx, lg, cw, cb, wg, ba, bx, lam, nw, bsz, seq):
    t = T_LRU
    nj = seq // t
    row = pl.BlockSpec((t, D_LRU), lambda b, j: (b * nj + j, 0))
    vec = _const_spec((1, D_LRU))
    return pl.pallas_call(
        _lru_kernel,
        grid=(bsz, nj),
        in_specs=[row, row, _const_spec(cw.shape), vec, _const_spec(wg.shape),
                  vec, vec, vec, vec],
        out_specs=row,
        out_shape=jax.ShapeDtypeStruct((bsz * seq, D_LRU), BF16),
        scratch_shapes=[pltpu.VMEM((t + SUBLANES, D_LRU), F32),
                        pltpu.VMEM((t, D_LRU), F32),
                        pltpu.VMEM((t, D_LRU), F32),
                        pltpu.VMEM((t, D_LRU), F32),
                        pltpu.VMEM((1, D_LRU), F32)],
        compiler_params=_cparams(("arbitrary", "arbitrary")),
        name="rglru",
    )(lx, lg, cw, cb, wg, ba, bx, lam, nw)


def _split3(x):
    hi = x.astype(BF16)
    r1 = x - hi.astype(F32)
    mid = r1.astype(BF16)
    lo = (r1 - mid.astype(F32)).astype(BF16)
    return hi, mid, lo


def _ssd_kernel(xbc_ref, z_ref, dt_ref, cw_ref, cb_ref, dtb_ref, alog_ref, dvec_ref, nw_ref,
                tril_ref, fut_ref, y_ref, xbuf, xc_s, state, y_s):
    tt = xbc_ref.shape[0]
    j = pl.program_id(1)

    @pl.when(j == 0)
    def _():
        xbuf[0:SUBLANES, :] = jnp.zeros((SUBLANES, xbuf.shape[1]), F32)
        state[...] = jnp.zeros_like(state)

    xbuf[SUBLANES:SUBLANES + tt, :] = xbc_ref[...].astype(F32)
    u = _causal_conv(xbuf, cw_ref, cb_ref, tt)
    xbuf[0:SUBLANES, :] = xbuf[tt:tt + SUBLANES, :]
    xc_s[...] = u * _sigmoid(u)
    for c in range(tt // SSD_CHUNK):
        _ssd_chunk(c, xc_s, dt_ref, dtb_ref, alog_ref, tril_ref, fut_ref, state, y_s)

    zf = z_ref[...].astype(F32)
    y = (y_s[...] + xc_s[:, 0:D_SSD] * dvec_ref[...]) * (zf * _sigmoid(zf))
    y_ref[...] = _rms(y, nw_ref[...]).astype(BF16)


def _ssd_chunk(c, xc_s, dt_ref, dtb_ref, alog_ref, tril_ref, fut_ref, state, y_s):
    t = SSD_CHUNK
    rows = slice(c * t, (c + 1) * t)
    gn = SSD_GROUPS * SSD_STATE
    xs = xc_s[rows, 0:D_SSD]
    bm = xc_s[rows, D_SSD:D_SSD + gn]
    cm = xc_s[rows, D_SSD + gn:]

    dt = _softplus(dt_ref[rows, :] + dtb_ref[...])
    d_a = dt * (-jnp.exp(alog_ref[...]))
    tril = tril_ref[...]
    future = fut_ref[...]
    hi, mid, lo = _split3(d_a)
    a_cs = (jnp.dot(tril, hi, preferred_element_type=F32)
            + jnp.dot(tril, mid, preferred_element_type=F32)
            + jnp.dot(tril, lo, preferred_element_type=F32))
    a_cs_t = a_cs.T
    dt_t = dt.T
    a_last_t = a_cs_t[:, t - 1:t]
    w_state_t = jnp.exp(a_last_t - a_cs_t) * dt_t
    chunk_decay_t = jnp.exp(a_last_t)
    src_t = a_cs_t - jnp.log(dt_t)

    lane = lax.broadcasted_iota(jnp.int32, (1, LANES), 1)
    first = lane < SSD_HEAD_DIM

    heads_per_group = SSD_HEADS // SSD_GROUPS
    for g in range(SSD_GROUPS):
        gsl = slice(g * SSD_STATE, (g + 1) * SSD_STATE)
        c_g = cm[:, gsl].astype(BF16)
        b_g = bm[:, gsl]
        scores = lax.dot_general(c_g, b_g.astype(BF16), (((1,), (1,)), ((), ())),
                                 preferred_element_type=F32)
        b_t = b_g.T
        st_g = state[g]
        y_off = jnp.dot(c_g, st_g.astype(BF16), preferred_element_type=F32)
        for q in range(heads_per_group // 2):
            h0 = g * heads_per_group + 2 * q
            psl = slice(h0 * SSD_HEAD_DIM, (h0 + 2) * SSD_HEAD_DIM)
            lsl = slice(2 * q * SSD_HEAD_DIM, (2 * q + 2) * SSD_HEAD_DIM)
            x_pair = xs[:, psl].astype(BF16)
            yd, ns, cols, cd = [], [], [], []
            for h in (h0, h0 + 1):
                col = jnp.broadcast_to(a_cs[:, h:h + 1], (t, t))
                lmat_dt = jnp.exp(col - src_t[h:h + 1, :] + future)
                m = (scores * lmat_dt).astype(BF16)
                yd.append(jnp.dot(m, x_pair, preferred_element_type=F32))
                bw = (b_t * w_state_t[h:h + 1, :]).astype(BF16)
                ns.append(jnp.dot(bw, x_pair, preferred_element_type=F32))
                cols.append(col)
                cd.append(jnp.broadcast_to(chunk_decay_t[h:h + 1, :], (1, LANES)))
            y_pair = (jnp.where(first, yd[0], yd[1])
                      + jnp.exp(jnp.where(first, cols[0], cols[1])) * y_off[:, lsl])
            y_s[rows, psl] = y_pair
            state[g, :, lsl] = (st_g[:, lsl] * jnp.where(first, cd[0], cd[1])
                                + jnp.where(first, ns[0], ns[1]))


def _ssd(xbc, z, dt, cw, cb, dtb, alog, dvec, nw, bsz, seq):
    t = T_SSD
    nj = seq // t
    dx = xbc.shape[1]
    row = lambda w: pl.BlockSpec((t, w), lambda b, j: (b * nj + j, 0))
    ch = SSD_CHUNK
    causal = jnp.arange(ch)[:, None] >= jnp.arange(ch)[None, :]
    return pl.pallas_call(
        _ssd_kernel,
        grid=(bsz, nj),
        in_specs=[row(dx), row(D_SSD), row(LANES), _const_spec(cw.shape), _const_spec((1, dx)),
                  _const_spec((1, LANES)), _const_spec((1, LANES)),
                  _const_spec((1, D_SSD)), _const_spec((1, D_SSD)),
                  _const_spec((ch, ch)), _const_spec((ch, ch))],
        out_specs=row(D_SSD),
        out_shape=jax.ShapeDtypeStruct((bsz * seq, D_SSD), BF16),
        scratch_shapes=[pltpu.VMEM((t + SUBLANES, dx), F32),
                        pltpu.VMEM((t, dx), F32),
                        pltpu.VMEM((SSD_GROUPS, SSD_STATE, D_SSD // SSD_GROUPS), F32),
                        pltpu.VMEM((t, D_SSD), F32)],
        compiler_params=_cparams(("arbitrary", "arbitrary")),
        name="ssd",
    )(xbc, z, dt, cw, cb, dtb, alog, dvec, nw, causal.astype(BF16),
      jnp.where(causal, 0.0, NEG_BIG).astype(F32))


ROUTE_E, ROUTE_W, ROUTE_R = 0, 2, 4
NEG_BIG = -1e30


def _first_argmax(vals, lane_f):
    m = jnp.max(vals, axis=-1, keepdims=True)
    idx = jnp.min(jnp.where(vals == m, lane_f, float(LANES)), axis=-1, keepdims=True)
    return m, idx


def _outproj_kernel(yl_ref, ys_ref, x_ref, wo_ref, nw_ref, wr_ref, rb_ref,
                    x1_ref, hp_ref, route_ref, cnt_ref, stril, running):
    @pl.when(pl.program_id(0) == 0)
    def _():
        ri = lax.broadcasted_iota(jnp.int32, (TM_SUB, TM_SUB), 0)
        ci = lax.broadcasted_iota(jnp.int32, (TM_SUB, TM_SUB), 1)
        stril[...] = (ri > ci).astype(BF16)
        running[...] = jnp.zeros_like(running)

    for s in range(x_ref.shape[0] // TM_SUB):
        _outproj_rows(pl.ds(s * TM_SUB, TM_SUB), yl_ref, ys_ref, x_ref, wo_ref, nw_ref, wr_ref, rb_ref,
                      x1_ref, hp_ref, route_ref, stril, running)
    cnt_ref[...] = running[...]


def _outproj_rows(rows, yl_ref, ys_ref, x_ref, wo_ref, nw_ref, wr_ref, rb_ref,
                  x1_ref, hp_ref, route_ref, stril, running):
    tm = TM_SUB
    y_cat = jnp.concatenate([yl_ref[rows, :], ys_ref[rows, :]], axis=1)
    x1 = x_ref[rows, :] + jnp.dot(y_cat, wo_ref[...], preferred_element_type=F32)
    x1_ref[rows, :] = x1
    h = _rms(x1, nw_ref[...])
    h_hi = h.astype(BF16)
    h_rt = h_hi.astype(F32)
    bits = lax.bitcast_convert_type(h_rt, U32)
    hp_ref[rows, :] = bits[:, D_HALF:] | (bits[:, :D_HALF] >> 16)
    logits = jnp.dot(h_hi, wr_ref[...], preferred_element_type=F32) + rb_ref[...]

    lane = lax.broadcasted_iota(jnp.int32, (tm, LANES), 1)
    lane_f = lane.astype(F32)
    is_c = lane < MOE_GROUPS
    lc = jnp.where(is_c, logits, NEG_BIG)
    m_c, g_idx = _first_argmax(lc, lane_f)
    g_w = 1.0 / jnp.sum(jnp.where(is_c, jnp.exp(lc - m_c), 0.0), axis=-1, keepdims=True)
    lo = float(MOE_GROUPS) + float(EXPERTS_PER_GROUP) * g_idx
    is_f = (lane_f >= lo) & (lane_f < lo + float(EXPERTS_PER_GROUP))
    lf = jnp.where(is_f, logits, NEG_BIG)
    v1, i1 = _first_argmax(lf, lane_f)
    v2, i2 = _first_argmax(jnp.where(lane_f == i1, NEG_BIG, lf), lane_f)
    ex = jnp.exp(v2 - v1)
    w1 = g_w / (1.0 + ex)
    w2 = g_w * ex / (1.0 + ex)

    oh1 = (lane_f == i1).astype(F32)
    oh2 = (lane_f == i2).astype(F32)
    both = oh1 + oh2
    before = jnp.dot(stril[...], both.astype(BF16), preferred_element_type=F32) + running[...]
    r1 = jnp.sum(oh1 * before, axis=-1, keepdims=True)
    r2 = jnp.sum(oh2 * before, axis=-1, keepdims=True)
    running[...] = running[...] + jnp.sum(both, axis=0, keepdims=True)

    e1 = i1 - float(MOE_GROUPS)
    e2 = i2 - float(MOE_GROUPS)
    route = jnp.zeros((tm, LANES), F32)
    for off, (a, b) in ((ROUTE_E, (e1, e2)), (ROUTE_W, (w1, w2)), (ROUTE_R, (r1, r2))):
        route = jnp.where(lane == off, a, jnp.where(lane == off + 1, b, route))
    route_ref[rows, :] = route


def _outproj(layer, yl, ys, x, wo, nw, wr, rb):
    n = x.shape[0]
    tm = TM_PROJ
    row = lambda w: pl.BlockSpec((tm, w), lambda i: (i, 0))
    return pl.pallas_call(
        _outproj_kernel,
        grid=(n // tm,),
        in_specs=[row(D_LRU), row(D_SSD), row(D_MODEL),
                  pl.BlockSpec((None,) + wo.shape[1:], lambda i: (layer, 0, 0)),
                  _const_spec((1, D_MODEL)),
                  _const_spec(wr.shape), _const_spec((1, LANES))],
        out_specs=[row(D_MODEL), row(D_HALF), row(LANES), _const_spec((1, LANES))],
        out_shape=[jax.ShapeDtypeStruct((n, D_MODEL), F32),
                   jax.ShapeDtypeStruct((n, D_HALF), U32),
                   jax.ShapeDtypeStruct((n, LANES), F32),
                   jax.ShapeDtypeStruct((1, LANES), F32)],
        scratch_shapes=[pltpu.VMEM((TM_SUB, TM_SUB), BF16), pltpu.VMEM((1, LANES), F32)],
        compiler_params=_cparams(("arbitrary",)),
        name="outproj",
    )(yl, ys, x, wo, nw, wr, rb)


SCHED_EXPERT, SCHED_FIRST, SCHED_NEXT, SCHED_SLOT = 0, 1, 2, 3


def _expert_kernel(sch_ref, nu_ref, rt_ref, rtn_ref, hp_hbm, wg_hbm, wu_hbm, wd_hbm, yb_ref,
                   hp_v, xs0, xs1, wg_f, wu_f, wd_f, wg_s, wu_s, wd_s, sem, wsem, *, layer):
    j = pl.program_id(0)
    out_rows = TM_EXP * ROW_TILES

    def weight_copies(expert, s):
        return (pltpu.make_async_copy(wg_hbm.at[layer, expert], wg_f.at[s], wsem.at[s, 0]),
                pltpu.make_async_copy(wu_hbm.at[layer, expert], wu_f.at[s], wsem.at[s, 1]),
                pltpu.make_async_copy(wd_hbm.at[layer, expert], wd_f.at[s], wsem.at[s, 2]))

    def gather(idx_ref, row, dst):
        for r in range(TM_EXP):
            dst[r:r + 1, :] = hp_v[pl.ds(idx_ref[row, r], 1), :]

    @pl.when(j == 0)
    def _():
        for cp in weight_copies(sch_ref[SCHED_EXPERT, 0], sch_ref[SCHED_SLOT, 0]):
            cp.start()
        cp = pltpu.make_async_copy(hp_hbm, hp_v, sem)
        cp.start()
        cp.wait()
        gather(rt_ref, 0, xs0)

    def block(half, src, prefetch):
        i = 2 * j + half
        e = sch_ref[SCHED_EXPERT, i]
        slot = sch_ref[SCHED_SLOT, i]
        active = i < nu_ref[0]
        out = pl.ds(half * out_rows, out_rows)

        @pl.when(active & (sch_ref[SCHED_FIRST, i] == 1))
        def _():
            for cp in weight_copies(e, slot):
                cp.wait()
            wg_s[...] = wg_f[slot].astype(BF16)
            wu_s[...] = wu_f[slot].astype(BF16)
            wd_s[...] = wd_f[slot].astype(BF16)
            nxt = sch_ref[SCHED_NEXT, i]

            @pl.when(nxt != e)
            def _():
                for cp in weight_copies(nxt, 1 - slot):
                    cp.start()

        @pl.when(active)
        def _():
            prefetch()
            packed = src[...]
            x_lo = lax.bitcast_convert_type(packed << 16, F32).astype(BF16)
            x_hi = lax.bitcast_convert_type((packed >> 16) << 16, F32).astype(BF16)
            gate = (jnp.dot(x_lo, wg_s[0:D_HALF, :], preferred_element_type=F32)
                    + jnp.dot(x_hi, wg_s[D_HALF:, :], preferred_element_type=F32))
            up = (jnp.dot(x_lo, wu_s[0:D_HALF, :], preferred_element_type=F32)
                  + jnp.dot(x_hi, wu_s[D_HALF:, :], preferred_element_type=F32))
            hid = (gate * _sigmoid(gate) * up).astype(BF16)
            y = jnp.dot(hid, wd_s[...], preferred_element_type=F32)
            for c in range(ROW_TILES):
                yb_ref[pl.ds(half * out_rows + c, TM_EXP, stride=ROW_TILES), :] = (
                    y[:, c * LANES:(c + 1) * LANES])

        @pl.when(jnp.logical_not(active))
        def _():
            yb_ref[out, :] = jnp.zeros((out_rows, LANES), F32)

    block(0, xs0, lambda: gather(rt_ref, 1, xs1))
    block(1, xs1, lambda: gather(rtn_ref, 0, xs0))


def _experts(layer, sched, n_used, row_tok, hp, wg, wu, wd):
    n_blocks = sched.shape[1]
    tm = TM_EXP
    n_steps = n_blocks // 2
    hbm = pl.BlockSpec(memory_space=pl.ANY)
    grid_spec = pltpu.PrefetchScalarGridSpec(
        num_scalar_prefetch=2,
        grid=(n_steps,),
        in_specs=[pl.BlockSpec((None, 2, tm), lambda j, sch, nu: (j, 0, 0), memory_space=pltpu.SMEM),
                  pl.BlockSpec((None, 2, tm), lambda j, sch, nu: (jnp.minimum(j + 1, n_steps - 1), 0, 0),
                               memory_space=pltpu.SMEM),
                  hbm, hbm, hbm, hbm],
        out_specs=pl.BlockSpec((2 * tm * ROW_TILES, LANES), lambda j, sch, nu: (j, 0)),
        scratch_shapes=[pltpu.VMEM(hp.shape, U32),
                        pltpu.VMEM((tm, D_HALF), U32),
                        pltpu.VMEM((tm, D_HALF), U32),
                        pltpu.VMEM((2, D_MODEL, D_EXPERT), F32),
                        pltpu.VMEM((2, D_MODEL, D_EXPERT), F32),
                        pltpu.VMEM((2, D_EXPERT, D_MODEL), F32),
                        pltpu.VMEM((D_MODEL, D_EXPERT), BF16),
                        pltpu.VMEM((D_MODEL, D_EXPERT), BF16),
                        pltpu.VMEM((D_EXPERT, D_MODEL), BF16),
                        pltpu.SemaphoreType.DMA(()),
                        pltpu.SemaphoreType.DMA((2, 3))],
    )
    return pl.pallas_call(
        functools.partial(_expert_kernel, layer=layer),
        grid_spec=grid_spec,
        out_shape=jax.ShapeDtypeStruct((n_blocks * tm * ROW_TILES, LANES), F32),
        compiler_params=_cparams(("arbitrary",)),
        name="experts",
    )(sched, n_used, row_tok.reshape(n_steps, 2, tm), row_tok.reshape(n_steps, 2, tm), hp, wg, wu, wd)


def _combine_kernel(dcur_ref, dnxt_ref, x_ref, w_ref, nw_ref, yb_ref, o_ref, gbuf, sems,
                    *, final_norm):
    t = T_TOK
    i = pl.program_id(0)
    n_steps = pl.num_programs(0)
    slot = i % 2

    def copy(dref, s, k, tok):
        src = pl.ds(pl.multiple_of(dref[k, tok] * ROW_TILES, ROW_TILES), ROW_TILES)
        row0 = tok * ROW_TILES
        dst = pl.ds(row0 if isinstance(tok, int) else pl.multiple_of(row0, ROW_TILES), ROW_TILES)
        return pltpu.make_async_copy(yb_ref.at[src, :], gbuf.at[s, k, dst, :], sems.at[s])

    def issue(dref, s):
        def body(tok, c):
            for k in range(TOP_K):
                copy(dref, s, k, tok).start()
            return c
        lax.fori_loop(0, t, body, 0, unroll=8)

    @pl.when(i == 0)
    def _():
        issue(dcur_ref, 0)

    @pl.when(i + 1 < n_steps)
    def _():
        for tok in range(t):
            for k in range(TOP_K):
                copy(dnxt_ref, 1 - slot, k, tok).start(priority=k)

    def wait(tok, c):
        for k in range(TOP_K):
            copy(dcur_ref, slot, k, tok).wait()
        return c

    lax.fori_loop(0, t, wait, 0, unroll=8)

    w0 = w_ref[:, ROUTE_W:ROUTE_W + 1]
    w1 = w_ref[:, ROUTE_W + 1:ROUTE_W + 2]
    parts = []
    for c in range(ROW_TILES):
        g0 = gbuf[slot, 0, pl.ds(c, t, stride=ROW_TILES), :]
        g1 = gbuf[slot, 1, pl.ds(c, t, stride=ROW_TILES), :]
        parts.append(x_ref[:, c * LANES:(c + 1) * LANES] + w0 * g0 + w1 * g1)
    out = jnp.concatenate(parts, axis=-1)
    if final_norm:
        out = _rms(out, nw_ref[...])
    o_ref[...] = out


def _combine(dest3, x, route, nw, yb, final_norm):
    n = x.shape[0]
    t = T_TOK
    n_steps = n // t
    dspec = lambda f: pl.BlockSpec((None, TOP_K, t), lambda i: (f(i), 0, 0), memory_space=pltpu.SMEM)
    return pl.pallas_call(
        functools.partial(_combine_kernel, final_norm=final_norm),
        grid=(n_steps,),
        in_specs=[dspec(lambda i: i), dspec(lambda i: jnp.minimum(i + 1, n_steps - 1)),
                  pl.BlockSpec((t, D_MODEL), lambda i: (i, 0)),
                  pl.BlockSpec((t, LANES), lambda i: (i, 0)),
                  _const_spec((1, D_MODEL)),
                  pl.BlockSpec(memory_space=pl.ANY)],
        out_specs=pl.BlockSpec((t, D_MODEL), lambda i: (i, 0)),
        out_shape=jax.ShapeDtypeStruct((n, D_MODEL), F32),
        scratch_shapes=[pltpu.VMEM((2, TOP_K, t * ROW_TILES, LANES), F32),
                        pltpu.SemaphoreType.DMA((2,))],
        compiler_params=_cparams(("arbitrary",)),
        name="combine",
    )(dest3, dest3, x, route, nw, yb)


T_SLOT = 1024


def _slots_kernel(route_ref, ps_ref, dest_ref):
    route = route_ref[...]
    lane_f = lax.broadcasted_iota(jnp.int32, route.shape, 1).astype(F32)
    ps = ps_ref[...]
    slots = []
    for k in range(TOP_K):
        e = route[:, ROUTE_E + k:ROUTE_E + k + 1] + float(MOE_GROUPS)
        base = jnp.sum(jnp.where(lane_f == e, ps, 0.0), axis=-1, keepdims=True)
        slots.append(base + route[:, ROUTE_R + k:ROUTE_R + k + 1])
    slab = jnp.where(lane_f == 0.0, slots[0], jnp.where(lane_f == 1.0, slots[1], 0.0))
    for i in range(T_SLOT // T_TOK):
        part = slab[i * T_TOK:(i + 1) * T_TOK, :].T
        dest_ref[i] = part[0:TOP_K, :].astype(jnp.int32)


def _slots(route, pad_start):
    n = route.shape[0]
    ps = jnp.pad(pad_start.astype(F32), (MOE_GROUPS, LANES - MOE_GROUPS - N_EXPERTS)).reshape(1, LANES)
    per = T_SLOT // T_TOK
    return pl.pallas_call(
        _slots_kernel,
        grid=(n // T_SLOT,),
        in_specs=[pl.BlockSpec((T_SLOT, LANES), lambda i: (i, 0)), _const_spec((1, LANES))],
        out_specs=pl.BlockSpec((per, TOP_K, T_TOK), lambda i: (i, 0, 0)),
        out_shape=jax.ShapeDtypeStruct((n // T_TOK, TOP_K, T_TOK), jnp.int32),
        compiler_params=_cparams(("arbitrary",)),
        name="slots",
    )(route, ps)


def _route(route, cnt, n_blocks):
    n = route.shape[0]
    counts = cnt[0, MOE_GROUPS:MOE_GROUPS + N_EXPERTS].astype(jnp.int32)
    padded = (counts + TM_EXP - 1) // TM_EXP * TM_EXP
    pad_end = jnp.cumsum(padded)
    pad_start = pad_end - padded
    dest3 = _slots(route, pad_start)
    n_used = (pad_end[-1:] // TM_EXP).astype(jnp.int32)
    blk = jnp.arange(n_blocks, dtype=jnp.int32)
    blk_expert = jnp.minimum(jnp.sum(pad_end[None, :] <= (blk * TM_EXP)[:, None], axis=1),
                             N_EXPERTS - 1).astype(jnp.int32)
    eid = jnp.arange(N_EXPERTS, dtype=jnp.int32)
    nonempty = counts > 0
    later = (eid[None, :] > eid[:, None]) & nonempty[None, :]
    next_used = jnp.min(jnp.where(later, eid[None, :], N_EXPERTS), axis=1)
    next_used = jnp.where(next_used == N_EXPERTS, eid, next_used)
    ordinal = jnp.cumsum(nonempty.astype(jnp.int32)) - 1
    first = ((blk * TM_EXP == pad_start[blk_expert]) & (blk < n_used[0])).astype(jnp.int32)
    sched = jnp.stack([blk_expert, first, next_used[blk_expert], ordinal[blk_expert] % 2])
    token = jnp.arange(n, dtype=jnp.int32).reshape(n // T_TOK, 1, T_TOK)
    token = jnp.broadcast_to(token, dest3.shape)
    row_tok = jnp.zeros((n_blocks * TM_EXP,), jnp.int32).at[dest3.reshape(-1)].add(
        token.reshape(-1), unique_indices=True)
    return dest3, sched.astype(jnp.int32), n_used, row_tok


def _gate_blocks(wa, wx):
    per = GATE_W // LRU_BW
    eye = jnp.eye(per, dtype=F32)

    def bd(w):
        w = w.reshape(LRU_HEADS // per, per, LRU_BW, LRU_BW)
        full = jnp.einsum('gpij,pq->gpiqj', w, eye)
        return full.reshape(LRU_HEADS // per, GATE_W, GATE_W)

    return jnp.concatenate([bd(wa), bd(wx)], axis=-1).astype(BF16)


def kernel(x, norm_mix, w_in, lru_conv_w, lru_conv_b, lru_wa, lru_ba, lru_wx, lru_bx, lru_lambda, lru_norm, ssd_conv_w, ssd_conv_b, ssd_dt_bias, ssd_a_log, ssd_d, ssd_norm, w_out, norm_ffn, w_coarse, b_coarse, w_fine, b_fine, w_gate, w_up, w_down, final_norm):
    bsz, seq, d = x.shape
    n = bsz * seq
    depth = w_in.shape[0]
    n_assign = n * TOP_K
    n_blocks = -(-(n_assign + N_EXPERTS * (TM_EXP - 1)) // TM_EXP)
    n_blocks += n_blocks % 2
    o_dt = 2 * D_LRU + D_SSD + SSD_XBC

    lane_pad = lambda v: jnp.pad(v, ((0, 0), (0, LANES - v.shape[1])))[:, None, :]
    row_pad = lambda w: jnp.pad(w, ((0, 0), (0, SUBLANES - w.shape[1]), (0, 0)))
    w_in_bf = w_in[:, :, :o_dt].astype(BF16)
    wdt_all = jnp.pad(w_in[:, :, o_dt:].astype(BF16), ((0, 0), (0, 0), (0, LANES - SSD_HEADS)))
    w_out_bf = w_out.astype(BF16)
    gates_all = jax.vmap(_gate_blocks)(lru_wa, lru_wx)
    lru_cw, ssd_cw = row_pad(lru_conv_w), row_pad(ssd_conv_w)
    dtb_all, alog_all = lane_pad(ssd_dt_bias), lane_pad(ssd_a_log)
    dvec_all = jnp.repeat(ssd_d, SSD_HEAD_DIM, axis=1)[:, None, :]
    w_r = jnp.concatenate([w_coarse, w_fine.transpose(0, 2, 1, 3).reshape(depth, d, N_EXPERTS)], axis=2)
    w_r_all = jnp.pad(w_r, ((0, 0), (0, 0), (0, LANES - w_r.shape[2]))).astype(BF16)
    r_bias_all = lane_pad(jnp.concatenate([b_coarse, b_fine.reshape(depth, -1)], axis=1))
    vec = lambda p, i: p[i].reshape(1, -1)

    xt = x.reshape(n, d)
    for i in range(depth):
        lx, lg, z, xbc, dt = _inproj(i, xt, vec(norm_mix, i), w_in_bf, wdt_all[i])
        y_lru = _lru(lx, lg, lru_cw[i], vec(lru_conv_b, i), gates_all[i], vec(lru_ba, i),
                     vec(lru_bx, i), vec(lru_lambda, i), vec(lru_norm, i), bsz, seq)
        y_ssd = _ssd(xbc, z, dt, ssd_cw[i], vec(ssd_conv_b, i), dtb_all[i], alog_all[i],
                     dvec_all[i], vec(ssd_norm, i), bsz, seq)
        x1, hp, route, cnt = _outproj(i, y_lru, y_ssd, xt, w_out_bf, vec(norm_ffn, i),
                                      w_r_all[i], r_bias_all[i])
        dest3, sched, n_used, row_tok = _route(route, cnt, n_blocks)
        yb = _experts(i, sched, n_used, row_tok, hp, w_gate, w_up, w_down)
        xt = _combine(dest3, x1, route, final_norm.reshape(1, d), yb, final_norm=(i == depth - 1))
    return xt.reshape(bsz, seq, d)
```
